```python
import math
import jax, jax.numpy as jnp
from jax import lax
import numpy as np

D_MODEL = 1024
BATCH = 1
SEQ = 16384
DEPTH = 4
DEC_BATCH = 4
DEC_SEQ = 4096
PAST_LEN = 128

D_HYENA = 512
HYENA_GROUPS = 8
D_HGRN = D_MODEL - D_HYENA
HGRN_HEADS = 4
HGRN_DK = D_HGRN // HGRN_HEADS
HGRN_DV = D_HGRN // HGRN_HEADS
D_IN = 3 * D_HYENA + 5 * D_HGRN
D_FF = 4 * D_MODEL
CHUNK = 64
FILTER_EMB = 33
FILTER_BANDS = (FILTER_EMB - 1) // 2
FILTER_HIDDEN = 64
DECAY_TARGET = 1e-2
FAST_DECAY_PCT = 0.3
SLOW_DECAY_PCT = 1.5
EPS = 1e-6

kernel_name = "hyena_hgrn2_parallel_encoder"

F32 = jnp.float32


def _rmsnorm(x, gain):
    xf = x.astype(F32)
    y = xf * lax.rsqrt(jnp.mean(xf * xf, axis=-1, keepdims=True) + EPS)
    return (y * gain.astype(F32)).astype(x.dtype)


def _group_rmsnorm(x, gain, n_groups):
    b, l, c = x.shape
    y = _rmsnorm(x.reshape(b, l, n_groups, c // n_groups), gain.reshape(n_groups, c // n_groups))
    return y.reshape(b, l, c)


def _dwconv3(x, w, b):
    L = x.shape[1]
    xp = jnp.pad(x, ((0, 0), (1, 1), (0, 0)))
    return xp[:, :L] * w[0] + xp[:, 1:L + 1] * w[1] + xp[:, 2:] * w[2] + b


def _hyena_filter(L, w1, b1, w2, b2, w3, b3, w4, freq):
    t = jnp.linspace(0.0, 1.0, L, dtype=F32)[:, None]
    ang = (2.0 * math.pi / L) * jnp.arange(L, dtype=F32)[:, None]
    bands = jnp.linspace(1e-4, FILTER_BANDS - 1, FILTER_BANDS, dtype=F32)[None, :]
    feats = jnp.concatenate([t, jnp.cos(bands * ang), -jnp.sin(bands * ang)], axis=-1)
    fr = freq.astype(F32)
    h = jnp.sin(fr * (feats @ w1.astype(F32) + b1.astype(F32)))
    h = jnp.sin(fr * (h @ w2.astype(F32) + b2.astype(F32)))
    h = jnp.sin(fr * (h @ w3.astype(F32) + b3.astype(F32)))
    h = h @ w4.astype(F32)
    deltas = jnp.abs(jnp.linspace(math.log(DECAY_TARGET) / SLOW_DECAY_PCT,
                                  math.log(DECAY_TARGET) / FAST_DECAY_PCT, D_HYENA, dtype=F32))
    window = jnp.exp(-t * deltas)
    h = h.reshape(L, 2, D_HYENA) * window[:, None, :]
    h_fwd, h_bwd = h[:, 0], h[:, 1]
    return jnp.concatenate([h_fwd, jnp.zeros((1, D_HYENA), F32), h_bwd[:0:-1]], axis=0)


def _hyena(u, kernel, skip):
    L = u.shape[1]
    uf = u.astype(F32)
    x0, x1, v = jnp.split(uf, 3, axis=-1)
    z = x1 * v
    Z = jnp.fft.rfft(z, n=2 * L, axis=1)
    K = jnp.fft.rfft(kernel, axis=0)
    y = jnp.fft.irfft(Z * K[None], n=2 * L, axis=1)[:, :L] + skip.astype(F32) * z
    return (x0 * y).astype(u.dtype)


def _chunk_scan(q, k, logf, v):
    B, L, H, dk = q.shape
    dv = v.shape[-1]
    n = L // CHUNK

    def to_chunks(a):
        return a.reshape(B, n, CHUNK, H, a.shape[-1]).transpose(1, 0, 3, 2, 4)

    causal = jnp.tril(jnp.ones((CHUNK, CHUNK), dtype=bool))[:, :, None]

    def step(S, xs):
        qc, kc, gc, vc = xs
        b = jnp.cumsum(gc, axis=2)
        o_inter = jnp.einsum('bhtd,bhde->bhte', qc * jnp.exp(b), S)
        rel = b[:, :, :, None, :] - b[:, :, None, :, :]
        decay = jnp.exp(jnp.where(causal, rel, -jnp.inf))
        scores = jnp.einsum('bhtd,bhsd,bhtsd->bhts', qc, kc, decay)
        o_intra = jnp.einsum('bhts,bhse->bhte', scores, vc)
        b_last = b[:, :, -1:, :]
        S_new = jnp.exp(b_last[:, :, 0, :, None]) * S + jnp.einsum(
            'bhsd,bhse->bhde', kc * jnp.exp(b_last - b), vc)
        return S_new, o_inter + o_intra

    S0 = jnp.zeros((B, H, dk, dv), F32)
    _, o = lax.scan(step, S0, (to_chunks(q), to_chunks(k), to_chunks(logf), to_chunks(v)))
    return o.transpose(1, 0, 3, 2, 4).reshape(B, L, H, dv)


def _hgrn2(hg, lb_fwd, lb_bwd, out_gain):
    B, L, _ = hg.shape
    q, f_fwd, f_bwd, i, g = jnp.split(hg.astype(F32), 5, axis=-1)
    q = jax.nn.silu(q)

    def heads(a):
        return a.reshape(B, L, HGRN_HEADS, a.shape[-1] // HGRN_HEADS)

    def log_forget(z, lb):
        lb = lb.astype(F32)
        return jnp.logaddexp(jnp.log(lb), jnp.log1p(-lb) + jax.nn.log_sigmoid(z))

    def flip(a):
        return jnp.flip(a, axis=1)

    gf = heads(log_forget(f_fwd, lb_fwd))
    gb = heads(log_forget(f_bwd, lb_bwd))
    qh, ih = heads(q), heads(i)
    o_f = _chunk_scan(qh, -jnp.expm1(gf), gf, ih)
    o_b = flip(_chunk_scan(flip(qh), flip(-jnp.expm1(gb)), flip(gb), flip(ih)))
    o = _rmsnorm(o_f + o_b, out_gain.reshape(HGRN_HEADS, HGRN_DV))
    o = o.reshape(B, L, D_HGRN) * jax.nn.silu(g)
    return o.astype(hg.dtype)


def _lower_bounds(table):
    c = jnp.cumsum(jax.nn.softmax(table.astype(F32), axis=1), axis=1)
    return c - c[:, :1]


def _trunk(x, lbs, norm_mix_pre, norm_mix_post, norm_ffn_pre, norm_ffn_post, w_in,
           hyena_conv_w, hyena_conv_b, filt_w1, filt_b1, filt_w2, filt_b2, filt_w3, filt_b3,
           filt_w4, filt_freq, hyena_skip, hyena_out_norm, hgrn_out_norm, w_out,
           ffn_w_up, ffn_conv_w, ffn_conv_b, ffn_w_down):
    L = x.shape[1]
    for l in range(DEPTH):
        h = _rmsnorm(x, norm_mix_pre[l])
        proj = h @ w_in[l]
        hy = _dwconv3(proj[..., :3 * D_HYENA], hyena_conv_w[l], hyena_conv_b[l])
        hg = proj[..., 3 * D_HYENA:]
        kernel = _hyena_filter(L, filt_w1[l], filt_b1[l], filt_w2[l], filt_b2[l], filt_w3[l],
                               filt_b3[l], filt_w4[l], filt_freq[l])
        y_hy = _group_rmsnorm(_hyena(hy, kernel, hyena_skip[l]), hyena_out_norm[l], HYENA_GROUPS)
        y_hg = _hgrn2(hg, lbs[0, l], lbs[1, l], hgrn_out_norm[l])
        mix = jnp.concatenate([y_hy, y_hg], axis=-1) @ w_out[l]
        x = x + _rmsnorm(mix, norm_mix_post[l])
        h = _rmsnorm(x, norm_ffn_pre[l])
        u = _dwconv3(h @ ffn_w_up[l], ffn_conv_w[l], ffn_conv_b[l])
        a, b = jnp.split(u, 2, axis=-1)
        ff = (jax.nn.gelu(a, approximate=True) * b) @ ffn_w_down[l]
        x = x + _rmsnorm(ff, norm_ffn_post[l])
    return x


def setup_inputs(seed: int = 0) -> dict:
    key = jax.random.key(seed)
    ks = jax.random.split(key, 32)

    def nrm(k, shape, scale):
        return jax.random.normal(k, shape, F32) * scale

    def gain(k, shape):
        return 1.0 + 0.05 * jax.random.normal(k, shape, F32)

    centre = jnp.array([0.0, 1.0, 0.0], F32)[:, None]
    return {
        "x_prompt": nrm(ks[0], (BATCH, SEQ, D_MODEL), 1.0),
        "x_sample": nrm(ks[1], (DEC_BATCH, DEC_SEQ, D_MODEL), 1.0),
        "norm_mix_pre": gain(ks[2], (DEPTH, D_MODEL)),
        "norm_mix_post": gain(ks[3], (DEPTH, D_MODEL)),
        "norm_ffn_pre": gain(ks[4], (DEPTH, D_MODEL)),
        "norm_ffn_post": gain(ks[5], (DEPTH, D_MODEL)),
        "w_in": nrm(ks[6], (DEPTH, D_MODEL, D_IN), D_MODEL ** -0.5),
        "hyena_conv_w": centre + nrm(ks[7], (DEPTH, 3, 3 * D_HYENA), 0.3),
        "hyena_conv_b": nrm(ks[8], (DEPTH, 3 * D_HYENA), 0.01),
        "filt_w1": nrm(ks[9], (DEPTH, FILTER_EMB, FILTER_HIDDEN), FILTER_EMB ** -0.5),
        "filt_b1": nrm(ks[10], (DEPTH, FILTER_HIDDEN), 0.1),
        "filt_w2": nrm(ks[11], (DEPTH, FILTER_HIDDEN, FILTER_HIDDEN), FILTER_HIDDEN ** -0.5),
        "filt_b2": nrm(ks[12], (DEPTH, FILTER_HIDDEN), 0.1),
        "filt_w3": nrm(ks[13], (DEPTH, FILTER_HIDDEN, FILTER_HIDDEN), FILTER_HIDDEN ** -0.5),
        "filt_b3": nrm(ks[14], (DEPTH, FILTER_HIDDEN), 0.1),
        "filt_w4": nrm(ks[15], (DEPTH, FILTER_HIDDEN, 2 * D_HYENA), FILTER_HIDDEN ** -0.5),
        "filt_freq": gain(ks[16], (DEPTH, FILTER_HIDDEN)),
        "hyena_skip": nrm(ks[17], (DEPTH, D_HYENA), 1.0),
        "hyena_out_norm": gain(ks[18], (DEPTH, D_HYENA)),
        "hgrn_lower_bounds": nrm(ks[19], (2, DEPTH, D_HGRN), 0.1),
        "hgrn_out_norm": gain(ks[20], (DEPTH, D_HGRN)),
        "w_out": nrm(ks[21], (DEPTH, D_MODEL, D_MODEL), D_MODEL ** -0.5),
        "ffn_w_up": nrm(ks[22], (DEPTH, D_MODEL, 2 * D_FF), D_MODEL ** -0.5),
        "ffn_conv_w": centre + nrm(ks[23], (DEPTH, 3, 2 * D_FF), 0.3),
        "ffn_conv_b": nrm(ks[24], (DEPTH, 2 * D_FF), 0.01),
        "ffn_w_down": nrm(ks[25], (DEPTH, D_FF, D_MODEL), D_FF ** -0.5),
    }


def reference(x_prompt, x_sample, norm_mix_pre, norm_mix_post, norm_ffn_pre, norm_ffn_post, w_in,
              hyena_conv_w, hyena_conv_b, filt_w1, filt_b1, filt_w2, filt_b2, filt_w3, filt_b3,
              filt_w4, filt_freq, hyena_skip, hyena_out_norm, hgrn_lower_bounds, hgrn_out_norm,
              w_out, ffn_w_up, ffn_conv_w, ffn_conv_b, ffn_w_down):
    lbs = _lower_bounds(hgrn_lower_bounds)
    y_prompt = _trunk(x_prompt, lbs, norm_mix_pre, norm_mix_post, norm_ffn_pre, norm_ffn_post,
                      w_in, hyena_conv_w, hyena_conv_b, filt_w1, filt_b1, filt_w2, filt_b2,
                      filt_w3, filt_b3, filt_w4, filt_freq, hyena_skip, hyena_out_norm,
                      hgrn_out_norm, w_out, ffn_w_up, ffn_conv_w, ffn_conv_b, ffn_w_down)
    y_sample = _trunk(x_sample, lbs, norm_mix_pre, norm_mix_post, norm_ffn_pre, norm_ffn_post,
                      w_in, hyena_conv_w, hyena_conv_b, filt_w1, filt_b1, filt_w2, filt_b2,
                      filt_w3, filt_b3, filt_w4, filt_freq, hyena_skip, hyena_out_norm,
                      hgrn_out_norm, w_out, ffn_w_up, ffn_conv_w, ffn_conv_b, ffn_w_down)
    return (y_prompt, y_sample)
```

```python
import functools
import math

import numpy as np
import jax
import jax.numpy as jnp
from jax import lax
from jax.experimental import pallas as pl
from jax.experimental.pallas import tpu as pltpu

F32 = jnp.float32
BF16 = jnp.bfloat16
HIGHEST = lax.Precision.HIGHEST

D_MODEL = 1024
D_HYENA = 512
HYENA_GROUPS = 8
D_HGRN = 512
HGRN_HEADS = 4
HEAD_DIM = 128
D_IN = 3 * D_HYENA + 5 * D_HGRN
D_FF = 4 * D_MODEL
FILTER_EMB = 33
FILTER_BANDS = 16
FILTER_HIDDEN = 64
DECAY_TARGET = 1e-2
FAST_DECAY_PCT = 0.3
SLOW_DECAY_PCT = 1.5
EPS = 1e-6

SUBLANES = 8
LANES = 128
VMEM_LIMIT_BYTES = 56 * 1024 * 1024

FFT_FAST = 128
FFT_KS_GROUP = 8
FFN_HALO = 16
SCAN_CHUNK = 128
SCAN_LEVELS = (64, 32, 16, 8, 4, 2, 1)


def _params(*semantics):
    return pltpu.CompilerParams(dimension_semantics=semantics,
                                vmem_limit_bytes=VMEM_LIMIT_BYTES)


def _rms_scale(x):
    return lax.rsqrt(jnp.mean(x * x, axis=-1, keepdims=True) + EPS)


def _split_dot(a, b_exact):
    hi = a.astype(BF16)
    lo = (a - hi.astype(F32)).astype(BF16)
    return (jnp.dot(hi, b_exact, preferred_element_type=F32)
            + jnp.dot(lo, b_exact, preferred_element_type=F32))


def _in_proj_kernel(x_ref, g_ref, w_ref, o_ref):
    x = x_ref[...]
    h = (x * _rms_scale(x) * g_ref[...]).astype(BF16)
    o_ref[...] = jnp.dot(h, w_ref[...], preferred_element_type=F32)


def _in_proj(x2d, gain, w_bf16, tm=1024, tn=1024):
    T = x2d.shape[0]
    N = w_bf16.shape[1]
    return pl.pallas_call(
        _in_proj_kernel,
        grid=(T // tm, N // tn),
        in_specs=[
            pl.BlockSpec((tm, D_MODEL), lambda i, j: (i, 0)),
            pl.BlockSpec((1, D_MODEL), lambda i, j: (0, 0)),
            pl.BlockSpec((D_MODEL, tn), lambda i, j: (0, j)),
        ],
        out_specs=pl.BlockSpec((tm, tn), lambda i, j: (i, j)),
        out_shape=jax.ShapeDtypeStruct((T, N), F32),
        compiler_params=_params("parallel", "arbitrary"),
    )(x2d, gain.reshape(1, D_MODEL), w_bf16)


def _shift_rows(main, prev_row, next_row):
    tm = main.shape[0]
    rows = lax.broadcasted_iota(jnp.int32, (tm, 1), 0)
    down = jnp.where(rows == 0, prev_row, pltpu.roll(main, 1, 0))
    up = jnp.where(rows == tm - 1, next_row, pltpu.roll(main, tm - 1, 0))
    return down, up


def _hyena_pre_kernel(prev_ref, main_ref, next_ref, w_ref, b_ref, x0_ref, z_ref,
                      *, tiles_per_seq):
    pos = pl.program_id(0) % tiles_per_seq
    main = main_ref[...]
    prev_row = jnp.where(pos == 0, 0.0, prev_ref[SUBLANES - 1:SUBLANES, :])
    next_row = jnp.where(pos == tiles_per_seq - 1, 0.0, next_ref[0:1, :])
    down, up = _shift_rows(main, prev_row, next_row)
    w = w_ref[...]
    u = down * w[0:1, :] + main * w[1:2, :] + up * w[2:3, :] + b_ref[...]
    x0_ref[...] = u[:, :D_HYENA]
    z_ref[...] = u[:, D_HYENA:2 * D_HYENA] * u[:, 2 * D_HYENA:]


def _hyena_pre(proj, conv_w, conv_b, seq_len, tm=512):
    T = proj.shape[0]
    width = 3 * D_HYENA
    halo_per_tile = tm // SUBLANES
    last_halo = T // SUBLANES - 1
    out = jax.ShapeDtypeStruct((T, D_HYENA), F32)
    return pl.pallas_call(
        functools.partial(_hyena_pre_kernel, tiles_per_seq=seq_len // tm),
        grid=(T // tm,),
        in_specs=[
            pl.BlockSpec((SUBLANES, width),
                         lambda i: (jnp.maximum(i * halo_per_tile - 1, 0), 0)),
            pl.BlockSpec((tm, width), lambda i: (i, 0)),
            pl.BlockSpec((SUBLANES, width),
                         lambda i: (jnp.minimum((i + 1) * halo_per_tile, last_halo), 0)),
            pl.BlockSpec((3, width), lambda i: (0, 0)),
            pl.BlockSpec((1, width), lambda i: (0, 0)),
        ],
        out_specs=[pl.BlockSpec((tm, D_HYENA), lambda i: (i, 0)),
                   pl.BlockSpec((tm, D_HYENA), lambda i: (i, 0))],
        out_shape=[out, out],
        compiler_params=_params("parallel"),
    )(proj, proj, proj, conv_w, conv_b.reshape(1, width))


def _filter_kernel(band_ref, w1_ref, b1_ref, w2_ref, b2_ref, w3_ref, b3_ref, w4_ref,
                   freq_ref, delta_ref, o_ref, *, seq_len, tr):
    row0 = pl.program_id(0) * tr
    j = row0 + lax.broadcasted_iota(jnp.int32, (tr, 1), 0)
    idx = jnp.where(j < seq_len, j, 2 * seq_len - j).astype(F32)
    t = idx * (1.0 / (seq_len - 1))
    ang = (2.0 * math.pi / seq_len) * idx
    lane = lax.broadcasted_iota(jnp.int32, (tr, LANES), 1)
    arg = band_ref[...] * ang
    feats = jnp.where(lane == 0, t,
                      jnp.where(lane <= FILTER_BANDS, jnp.cos(arg),
                                jnp.where(lane <= 2 * FILTER_BANDS, -jnp.sin(arg), 0.0)))
    fr = freq_ref[...]

    def dense(a, w_ref_, b_ref_):
        return jnp.dot(a, w_ref_[...], precision=HIGHEST, preferred_element_type=F32) + b_ref_[...]

    h = jnp.sin(fr * dense(feats, w1_ref, b1_ref))
    h = jnp.sin(fr * dense(h, w2_ref, b2_ref))
    h = jnp.sin(fr * dense(h, w3_ref, b3_ref))
    h = jnp.dot(h, w4_ref[...], precision=HIGHEST, preferred_element_type=F32)
    window = jnp.exp(-t * delta_ref[...])
    o_ref[...] = jnp.where(j == seq_len, 0.0, h * window)


def _hyena_filter(seq_len, w1, b1, w2, b2, w3, b3, w4, freq, tr=512):
    n_taps = 2 * seq_len
    bands = np.zeros((1, LANES), np.float32)
    band_vals = np.linspace(1e-4, FILTER_BANDS - 1, FILTER_BANDS, dtype=np.float32)
    bands[0, 1:1 + FILTER_BANDS] = band_vals
    bands[0, 1 + FILTER_BANDS:1 + 2 * FILTER_BANDS] = band_vals
    deltas = np.abs(np.linspace(math.log(DECAY_TARGET) / SLOW_DECAY_PCT,
                                math.log(DECAY_TARGET) / FAST_DECAY_PCT, D_HYENA,
                                dtype=np.float32)).reshape(1, D_HYENA)
    w1p = jnp.zeros((LANES, FILTER_HIDDEN), F32).at[:FILTER_EMB].set(w1)
    tiles_fwd = seq_len // tr
    const = lambda i: (0, 0)
    row = lambda v: v.reshape(1, -1)
    return pl.pallas_call(
        functools.partial(_filter_kernel, seq_len=seq_len, tr=tr),
        grid=(n_taps // tr,),
        in_specs=[
            pl.BlockSpec((1, LANES), const),
            pl.BlockSpec((LANES, FILTER_HIDDEN), const),
            pl.BlockSpec((1, FILTER_HIDDEN), const),
            pl.BlockSpec((FILTER_HIDDEN, FILTER_HIDDEN), const),
            pl.BlockSpec((1, FILTER_HIDDEN), const),
            pl.BlockSpec((FILTER_HIDDEN, FILTER_HIDDEN), const),
            pl.BlockSpec((1, FILTER_HIDDEN), const),
            pl.BlockSpec((FILTER_HIDDEN, D_HYENA), lambda i: (0, i // tiles_fwd)),
            pl.BlockSpec((1, FILTER_HIDDEN), const),
            pl.BlockSpec((1, D_HYENA), const),
        ],
        out_specs=pl.BlockSpec((tr, D_HYENA), lambda i: (i, 0)),
        out_shape=jax.ShapeDtypeStruct((n_taps, D_HYENA), F32),
        compiler_params=_params("parallel"),
    )(jnp.asarray(bands), w1p, row(b1), w2, row(b2), w3, row(b3), w4, row(freq),
      jnp.asarray(deltas))


class _FftPlan:
    def __init__(self, seq_len):
        self.n = 2 * seq_len
        self.fast = FFT_FAST
        self.slow = self.n // FFT_FAST
        self.slow_half = self.slow // 2
        self.ks = self.slow_half + 1
        self.ks_pad = -(-self.ks // FFT_KS_GROUP) * FFT_KS_GROUP
        S, F, N = self.slow, self.fast, self.n
        ks = np.arange(self.ks_pad, dtype=np.float64)[:, None]
        valid = (ks < self.ks)

        def stage_a(n_s):
            s = np.arange(n_s, dtype=np.float64)[None, :]
            ang = 2.0 * np.pi * ks * s / S
            return np.concatenate([np.where(valid, np.cos(ang), 0.0),
                                   np.where(valid, -np.sin(ang), 0.0)], axis=0)

        self.a_half = stage_a(self.slow_half).astype(np.float32)
        self.a_full = stage_a(self.slow).astype(np.float32)
        s = np.arange(self.slow_half, dtype=np.float64)[:, None]
        kk = np.arange(self.ks_pad, dtype=np.float64)[None, :]
        weight = np.where((kk == 0) | (kk == self.slow_half), 1.0, 2.0) * (kk < self.ks) / N
        ang = 2.0 * np.pi * s * kk / S
        self.a_inv = np.concatenate([weight * np.cos(ang), -weight * np.sin(ang)],
                                    axis=1).astype(np.float32)
        f = np.arange(F, dtype=np.float64)
        ang = 2.0 * np.pi * np.outer(np.arange(self.ks_pad, dtype=np.float64), f) / N
        self.tw_re = np.cos(ang).reshape(-1, 1).astype(np.float32)
        self.tw_im = (-np.sin(ang)).reshape(-1, 1).astype(np.float32)
        ang = 2.0 * np.pi * np.outer(f, f) / F
        self.f_re = np.cos(ang).astype(np.float32)
        self.f_im = (-np.sin(ang)).astype(np.float32)


def _left_matmul_kernel(a_ref, x_ref, o_ref):
    o_ref[0] = jnp.dot(a_ref[...], x_ref[0], precision=HIGHEST, preferred_element_type=F32)


def _left_matmul(a, x, tn=2048):
    B, K, N = x.shape
    M = a.shape[0]
    return pl.pallas_call(
        _left_matmul_kernel,
        grid=(B, N // tn),
        in_specs=[pl.BlockSpec((M, K), lambda b, j: (0, 0)),
                  pl.BlockSpec((1, K, tn), lambda b, j: (b, 0, j))],
        out_specs=pl.BlockSpec((1, M, tn), lambda b, j: (b, 0, j)),
        out_shape=jax.ShapeDtypeStruct((B, M, N), F32),
        compiler_params=_params("parallel", "parallel"),
    )(jnp.asarray(a), x)


def _stage_b_tile(fr, fi, re, im):
    both = jnp.concatenate([re, im], axis=-1)
    pr = jnp.dot(fr, both, precision=HIGHEST, preferred_element_type=F32)
    pi = jnp.dot(fi, both, precision=HIGHEST, preferred_element_type=F32)
    w = re.shape[-1]
    return pr[:, :w] - pi[:, w:], pr[:, w:] + pi[:, :w]


def _fft_fwd_kernel(a_ref, twr_ref, twi_ref, fr_ref, fi_ref, o_ref):
    fr, fi = fr_ref[...], fi_ref[...]
    for g in range(FFT_KS_GROUP):
        rows = slice(g * FFT_FAST, (g + 1) * FFT_FAST)
        ar, ai = a_ref[0, 0, rows, :], a_ref[0, 1, rows, :]
        tr, ti = twr_ref[rows, :], twi_ref[rows, :]
        xr, xi = _stage_b_tile(fr, fi, ar * tr - ai * ti, ar * ti + ai * tr)
        o_ref[0, 0, rows, :] = xr
        o_ref[0, 1, rows, :] = xi


def _fft_conv_kernel(a_ref, k_ref, twr_ref, twi_ref, fr_ref, fi_ref, o_ref):
    fr, fi = fr_ref[...], fi_ref[...]
    for g in range(FFT_KS_GROUP):
        rows = slice(g * FFT_FAST, (g + 1) * FFT_FAST)
        ar, ai = a_ref[0, 0, rows, :], a_ref[0, 1, rows, :]
        tr, ti = twr_ref[rows, :], twi_ref[rows, :]
        xr, xi = _stage_b_tile(fr, fi, ar * tr - ai * ti, ar * ti + ai * tr)
        kr, ki = k_ref[0, 0, rows, :], k_ref[0, 1, rows, :]
        yr, yi = xr * kr - xi * ki, xr * ki + xi * kr
        br, bi = _stage_b_tile(fr, -fi, yr, yi)
        o_ref[0, 0, rows, :] = br * tr + bi * ti
        o_ref[0, 1, rows, :] = bi * tr - br * ti


def _fft_stage_b(plan, a, k=None):
    B, _, R, C = a.shape
    rows = FFT_KS_GROUP * FFT_FAST
    data = pl.BlockSpec((1, 2, rows, LANES), lambda c, r, b: (b, 0, r, c))
    tw = pl.BlockSpec((rows, LANES), lambda c, r, b: (r, 0))
    mat = pl.BlockSpec((FFT_FAST, FFT_FAST), lambda c, r, b: (0, 0))
    lane_bcast = lambda col: jnp.broadcast_to(jnp.asarray(col), (col.shape[0], LANES))
    consts = (lane_bcast(plan.tw_re), lane_bcast(plan.tw_im),
              jnp.asarray(plan.f_re), jnp.asarray(plan.f_im))
    common = dict(
        grid=(C // LANES, R // rows, B),
        out_specs=data,
        out_shape=jax.ShapeDtypeStruct(a.shape, F32),
        compiler_params=_params("parallel", "parallel", "arbitrary"),
    )
    if k is None:
        return pl.pallas_call(_fft_fwd_kernel, in_specs=[data, tw, tw, mat, mat],
                              **common)(a, *consts)
    kspec = pl.BlockSpec((1, 2, rows, LANES), lambda c, r, b: (0, 0, r, c))
    return pl.pallas_call(_fft_conv_kernel, in_specs=[data, kspec, tw, tw, mat, mat],
                          **common)(a, k, *consts)


def _fft_long_conv(z, kern, plan):
    B, L, C = z.shape
    S, F, P = plan.slow, plan.fast, plan.ks_pad
    ka = _left_matmul(plan.a_full, kern.reshape(1, S, F * C))
    k_spec = _fft_stage_b(plan, ka.reshape(1, 2, P * F, C))
    za = _left_matmul(plan.a_half, z.reshape(B, S // 2, F * C))
    ya = _fft_stage_b(plan, za.reshape(B, 2, P * F, C), k_spec)
    y = _left_matmul(plan.a_inv, ya.reshape(B, 2 * P, F * C))
    return y.reshape(B, L, C)


def _scan_constants():
    C = SCAN_CHUNK
    n_lv = len(SCAN_LEVELS)
    cum = np.zeros((n_lv + 2, C, C), np.float32)
    upper = np.zeros((n_lv, C, 1), np.float32)
    mask = np.zeros((n_lv + 1, C, C), np.float32)
    for li, h in enumerate(SCAN_LEVELS):
        for t in range(C):
            r = (t // (2 * h)) * 2 * h + h
            if t % (2 * h) >= h:
                upper[li, t, 0] = 1.0
                cum[li, t, r:t + 1] = 1.0
                mask[li, t, r - h:r] = 1.0
            else:
                cum[li, t, t + 1:r] = 1.0
    for t in range(C):
        cum[n_lv, t, :t + 1] = 1.0
        cum[n_lv + 1, t, t + 1:] = 1.0
        mask[n_lv, t, t] = 1.0
    flip = lambda m: m[:, ::-1, ::-1]
    cum2 = np.stack([cum, flip(cum)])
    upper2 = np.stack([upper, upper[:, ::-1]])
    mask2 = np.stack([mask, flip(mask)])
    return cum2, upper2, mask2


def _log_sigmoid(z):
    return jnp.minimum(z, 0.0) - jnp.log1p(jnp.exp(-jnp.abs(z)))


def _hgrn_scan_kernel(q_ref, f_ref, v_ref, lbt_ref, cum_ref, up_ref, mask_ref, o_ref,
                      state_ref, *, layer):
    n_lv = len(SCAN_LEVELS)

    @pl.when(pl.program_id(2) == 0)
    def _():
        state_ref[...] = jnp.zeros_like(state_ref)

    table = lbt_ref[0]
    e = jnp.exp(table - jnp.max(table, axis=0, keepdims=True))
    prob = e / jnp.sum(e, axis=0, keepdims=True)
    lb = jnp.zeros((1, D_HGRN), F32)
    for l in range(1, layer + 1):
        lb = lb + prob[l:l + 1, :]

    q = q_ref[...]
    q = q * (1.0 / (1.0 + jnp.exp(-q)))
    z = f_ref[...]
    log_sig = _log_sigmoid(z)
    log_a = jnp.log(lb)
    log_b = jnp.log1p(-lb) + log_sig
    gate = jnp.maximum(log_a, log_b) + jnp.log1p(jnp.exp(-jnp.abs(log_a - log_b)))
    key = (1.0 - lb) / (1.0 + jnp.exp(z))
    v = v_ref[...]

    g_hi = gate.astype(BF16)
    g_lo = (gate - g_hi.astype(F32)).astype(BF16)

    def gate_sums(i):
        m = cum_ref[0, i]
        return (jnp.dot(m, g_hi, preferred_element_type=F32)
                + jnp.dot(m, g_lo, preferred_element_type=F32))

    scores = [None] * HGRN_HEADS
    for li in range(n_lv):
        x = jnp.where(up_ref[0, li] > 0.5, q, key) * jnp.exp(gate_sums(li))
        xb = x.astype(BF16)
        m = mask_ref[0, li]
        for hd in range(HGRN_HEADS):
            xh = xb[:, hd * HEAD_DIM:(hd + 1) * HEAD_DIM]
            p = lax.dot_general(xh, xh, (((1,), (1,)), ((), ())), preferred_element_type=F32) * m
            scores[hd] = p if scores[hd] is None else scores[hd] + p

    b_inc = gate_sums(n_lv)
    b_rest = gate_sums(n_lv + 1)
    q_dec = (q * jnp.exp(b_inc)).astype(BF16)
    k_dec = (key * jnp.exp(b_rest)).astype(BF16)
    total = jnp.exp(b_inc[0:1, :] + b_rest[0:1, :])
    qb, kb, vb = q.astype(BF16), key.astype(BF16), v.astype(BF16)
    eye = mask_ref[0, n_lv]
    for hd in range(HGRN_HEADS):
        cols = slice(hd * HEAD_DIM, (hd + 1) * HEAD_DIM)
        diag = lax.dot_general(qb[:, cols], kb[:, cols], (((1,), (1,)), ((), ())),
                               preferred_element_type=F32) * eye
        p = (scores[hd] + diag).astype(BF16)
        st = state_ref[hd]
        o = jnp.dot(p, vb[:, cols], preferred_element_type=F32)
        o = o + lax.dot_general(q_dec[:, cols], st.astype(BF16), (((1,), (1,)), ((), ())),
                                preferred_element_type=F32)
        o_ref[0, :, cols] = o
        upd = lax.dot_general(vb[:, cols], k_dec[:, cols], (((0,), (0,)), ((), ())),
                              preferred_element_type=F32)
        state_ref[hd] = st * total[:, cols] + upd


def _hgrn_scan(proj, lb_table, layer, batch, seq_len):
    T = proj.shape[0]
    C = SCAN_CHUNK
    n_chunks = seq_len // C
    cum, upper, mask = _scan_constants()
    col0 = (3 * D_HYENA) // D_HGRN

    def rows(b, d, c):
        return b * n_chunks + c + d * (n_chunks - 1 - 2 * c)

    n_lv = len(SCAN_LEVELS)
    return pl.pallas_call(
        functools.partial(_hgrn_scan_kernel, layer=layer),
        grid=(batch, 2, n_chunks),
        in_specs=[
            pl.BlockSpec((C, D_HGRN), lambda b, d, c: (rows(b, d, c), col0)),
            pl.BlockSpec((C, D_HGRN), lambda b, d, c: (rows(b, d, c), col0 + 1 + d)),
            pl.BlockSpec((C, D_HGRN), lambda b, d, c: (rows(b, d, c), col0 + 3)),
            pl.BlockSpec((1,) + lb_table.shape[1:], lambda b, d, c: (d, 0, 0)),
            pl.BlockSpec((1, n_lv + 2, C, C), lambda b, d, c: (d, 0, 0, 0)),
            pl.BlockSpec((1, n_lv, C, 1), lambda b, d, c: (d, 0, 0, 0)),
            pl.BlockSpec((1, n_lv + 1, C, C), lambda b, d, c: (d, 0, 0, 0)),
        ],
        out_specs=pl.BlockSpec((1, C, D_HGRN), lambda b, d, c: (d, rows(b, d, c), 0)),
        out_shape=jax.ShapeDtypeStruct((2, T, D_HGRN), F32),
        scratch_shapes=[pltpu.VMEM((HGRN_HEADS, HEAD_DIM, HEAD_DIM), F32)],
        compiler_params=_params("parallel", "parallel", "arbitrary"),
    )(proj, proj, proj, lb_table, jnp.asarray(cum, BF16), jnp.asarray(upper),
      jnp.asarray(mask))


def _group_mean_matrix(group):
    idx = np.arange(D_HYENA) // group
    return (idx[:, None] == idx[None, :]).astype(np.float32)


def _mix_out_kernel(x_ref, y_ref, z_ref, x0_ref, of_ref, ob_ref, g_ref, skip_ref,
                    hy_gain_ref, hg_gain_ref, grp_hy_ref, grp_hg_ref, w_ref, post_ref, o_ref):
    z = z_ref[...]
    yh = x0_ref[...] * (y_ref[...] + skip_ref[...] * z)
    ms = _split_dot(yh * yh, grp_hy_ref[...]) * (HYENA_GROUPS / D_HYENA)
    yh = yh * lax.rsqrt(ms + EPS) * hy_gain_ref[...]
    o = of_ref[0] + ob_ref[0]
    ms = _split_dot(o * o, grp_hg_ref[...]) * (1.0 / HEAD_DIM)
    g = g_ref[...]
    o = o * lax.rsqrt(ms + EPS) * hg_gain_ref[...] * (g * (1.0 / (1.0 + jnp.exp(-g))))
    mix = (jnp.dot(yh.astype(BF16), w_ref[:D_HYENA, :], preferred_element_type=F32)
           + jnp.dot(o.astype(BF16), w_ref[D_HYENA:, :], preferred_element_type=F32))
    o_ref[...] = x_ref[...] + mix * _rms_scale(mix) * post_ref[...]


def _mix_out(x2d, y, z, x0, o_scan, proj, skip, hy_gain, hg_gain, w_bf16, post_gain, tm=512):
    T = x2d.shape[0]
    gate_col = D_IN // D_HGRN - 1
    half = lambda: pl.BlockSpec((tm, D_HYENA), lambda i: (i, 0))
    vec = lambda n: pl.BlockSpec((1, n), lambda i: (0, 0))
    sq = lambda: pl.BlockSpec((D_HYENA, D_HYENA), lambda i: (0, 0))
    row = lambda v: v.reshape(1, -1)
    return pl.pallas_call(
        _mix_out_kernel,
        grid=(T // tm,),
        in_specs=[
            pl.BlockSpec((tm, D_MODEL), lambda i: (i, 0)),
            half(), half(), half(),
            pl.BlockSpec((1, tm, D_HGRN), lambda i: (0, i, 0)),
            pl.BlockSpec((1, tm, D_HGRN), lambda i: (1, i, 0)),
            pl.BlockSpec((tm, D_HGRN), lambda i: (i, gate_col)),
            vec(D_HYENA), vec(D_HYENA), vec(D_HGRN), sq(), sq(),
            pl.BlockSpec((D_MODEL, D_MODEL), lambda i: (0, 0)),
            vec(D_MODEL),
        ],
        out_specs=pl.BlockSpec((tm, D_MODEL), lambda i: (i, 0)),
        out_shape=jax.ShapeDtypeStruct((T, D_MODEL), F32),
        compiler_params=_params("parallel"),
    )(x2d, y, z, x0, o_scan, o_scan, proj, row(skip), row(hy_gain), row(hg_gain),
      jnp.asarray(_group_mean_matrix(D_HYENA // HYENA_GROUPS), BF16),
      jnp.asarray(_group_mean_matrix(HEAD_DIM), BF16), w_bf16, row(post_gain))


def _gelu_tanh(x):
    return 0.5 * x * (1.0 + jnp.tanh(math.sqrt(2.0 / math.pi) * (x + 0.044715 * x * x * x)))


def _ffn_kernel(prev_ref, main_ref, next_ref, pre_ref, wa_ref, wb_ref, cwa_ref, cwb_ref,
                cba_ref, cbb_ref, wd_ref, post_ref, o_ref, h_ref, acc_ref, *, tiles_per_seq, tm):
    j = pl.program_id(1)
    ext = tm + 2 * FFN_HALO
    pos = pl.program_id(0) % tiles_per_seq

    @pl.when(j == 0)
    def _():
        gain = pre_ref[...]

        def normed(x):
            return (x * _rms_scale(x) * gain).astype(BF16)

        prev = normed(prev_ref[...])
        nxt = normed(next_ref[...])
        h_ref[0:FFN_HALO, :] = jnp.where(pos == 0, jnp.zeros_like(prev), prev)
        h_ref[FFN_HALO:FFN_HALO + tm, :] = normed(main_ref[...])
        h_ref[FFN_HALO + tm:, :] = jnp.where(pos == tiles_per_seq - 1, jnp.zeros_like(nxt), nxt)
        acc_ref[...] = jnp.zeros_like(acc_ref)

    h = h_ref[...]

    def conv_part(w_ref_, cw_ref_, cb_ref_):
        u = jnp.dot(h, w_ref_[...], preferred_element_type=F32)
        cw = cw_ref_[...]
        down = pltpu.roll(u, 1, 0)[FFN_HALO:FFN_HALO + tm, :]
        up = pltpu.roll(u, ext - 1, 0)[FFN_HALO:FFN_HALO + tm, :]
        mid = u[FFN_HALO:FFN_HALO + tm, :]
        return down * cw[0:1, :] + mid * cw[1:2, :] + up * cw[2:3, :] + cb_ref_[...]

    a = conv_part(wa_ref, cwa_ref, cba_ref)
    b = conv_part(wb_ref, cwb_ref, cbb_ref)
    act = (_gelu_tanh(a) * b).astype(BF16)
    acc_ref[...] += jnp.dot(act, wd_ref[...], preferred_element_type=F32)

    @pl.when(j == pl.num_programs(1) - 1)
    def _():
        ff = acc_ref[...]
        o_ref[...] = main_ref[...] + ff * _rms_scale(ff) * post_ref[...]


def _ffn(x2d, pre_gain, w_up_bf16, conv_w, conv_b, w_down_bf16, post_gain, seq_len,
         tm=512, tf=512):
    T = x2d.shape[0]
    halo_per_tile = tm // FFN_HALO
    last_halo = T // FFN_HALO - 1
    n_f = D_FF // tf
    row = lambda v: v.reshape(1, -1)
    return pl.pallas_call(
        functools.partial(_ffn_kernel, tiles_per_seq=seq_len // tm, tm=tm),
        grid=(T // tm, n_f),
        in_specs=[
            pl.BlockSpec((FFN_HALO, D_MODEL),
                         lambda i, j: (jnp.maximum(i * halo_per_tile - 1, 0), 0)),
            pl.BlockSpec((tm, D_MODEL), lambda i, j: (i, 0)),
            pl.BlockSpec((FFN_HALO, D_MODEL),
                         lambda i, j: (jnp.minimum((i + 1) * halo_per_tile, last_halo), 0)),
            pl.BlockSpec((1, D_MODEL), lambda i, j: (0, 0)),
            pl.BlockSpec((D_MODEL, tf), lambda i, j: (0, j)),
            pl.BlockSpec((D_MODEL, tf), lambda i, j: (0, n_f + j)),
            pl.BlockSpec((3, tf), lambda i, j: (0, j)),
            pl.BlockSpec((3, tf), lambda i, j: (0, n_f + j)),
            pl.BlockSpec((1, tf), lambda i, j: (0, j)),
            pl.BlockSpec((1, tf), lambda i, j: (0, n_f + j)),
            pl.BlockSpec((tf, D_MODEL), lambda i, j: (j, 0)),
            pl.BlockSpec((1, D_MODEL), lambda i, j: (0, 0)),
        ],
        out_specs=pl.BlockSpec((tm, D_MODEL), lambda i, j: (i, 0)),
        out_shape=jax.ShapeDtypeStruct((T, D_MODEL), F32),
        scratch_shapes=[pltpu.VMEM((tm + 2 * FFN_HALO, D_MODEL), BF16),
                        pltpu.VMEM((tm, D_MODEL), F32)],
        compiler_params=_params("parallel", "arbitrary"),
    )(x2d, x2d, x2d, row(pre_gain), w_up_bf16, w_up_bf16, conv_w, conv_w,
      row(conv_b), row(conv_b), w_down_bf16, row(post_gain))


def _trunk(x, p):
    B, L, _ = x.shape
    x2d = x.reshape(B * L, D_MODEL)
    plan = _FftPlan(L)
    for l in range(p["w_in"].shape[0]):
        proj = _in_proj(x2d, p["norm_mix_pre"][l], p["w_in_bf16"][l])
        x0, z = _hyena_pre(proj, p["hyena_conv_w"][l], p["hyena_conv_b"][l], L)
        kern = _hyena_filter(L, p["filt_w1"][l], p["filt_b1"][l], p["filt_w2"][l], p["filt_b2"][l],
                             p["filt_w3"][l], p["filt_b3"][l], p["filt_w4"][l], p["filt_freq"][l])
        y = _fft_long_conv(z.reshape(B, L, D_HYENA), kern, plan).reshape(B * L, D_HYENA)
        o_scan = _hgrn_scan(proj, p["hgrn_lower_bounds"], l, B, L)
        x2d = _mix_out(x2d, y, z, x0, o_scan, proj, p["hyena_skip"][l], p["hyena_out_norm"][l],
                       p["hgrn_out_norm"][l], p["w_out_bf16"][l], p["norm_mix_post"][l])
        x2d = _ffn(x2d, p["norm_ffn_pre"][l], p["ffn_w_up_bf16"][l], p["ffn_conv_w"][l],
                   p["ffn_conv_b"][l], p["ffn_w_down_bf16"][l], p["norm_ffn_post"][l], L)
    return x2d.reshape(B, L, D_MODEL)


def kernel(x_prompt, x_sample, norm_mix_pre, norm_mix_post, norm_ffn_pre, norm_ffn_post, w_in, hyena_conv_w, hyena_conv_b, filt_w1, filt_b1, filt_w2, filt_b2, filt_w3, filt_b3, filt_w4, filt_freq, hyena_skip, hyena_out_norm, hgrn_lower_bounds, hgrn_out_norm, w_out, ffn_w_up, ffn_conv_w, ffn_conv_b, ffn_w_down):
    p = dict(
        norm_mix_pre=norm_mix_pre, norm_mix_post=norm_mix_post, norm_ffn_pre=norm_ffn_pre,
        norm_ffn_post=norm_ffn_post, w_in=w_in, hyena_conv_w=hyena_conv_w,
        hyena_conv_b=hyena_conv_b, filt_w1=filt_w1, filt_b1=filt_b1, filt_w2=filt_w2,
        filt_b2=filt_b2, filt_w3=filt_w3, filt_b3=filt_b3, filt_w4=filt_w4, filt_freq=filt_freq,
        hyena_skip=hyena_skip, hyena_out_norm=hyena_out_norm,
        hgrn_lower_bounds=hgrn_lower_bounds, hgrn_out_norm=hgrn_out_norm,
        ffn_conv_w=ffn_conv_w, ffn_conv_b=ffn_conv_b,
        w_in_bf16=w_in.astype(BF16), w_out_bf16=w_out.astype(BF16),
        ffn_w_up_bf16=ffn_w_up.astype(BF16), ffn_w_down_bf16=ffn_w_down.astype(BF16),
    )
    return (_trunk(x_prompt, p), _trunk(x_sample, p))
```

```python
import functools
import math

import numpy as np
import jax
import jax.numpy as jnp
from jax import lax
from jax.experimental import pallas as pl
from jax.experimental.pallas import tpu as pltpu

F32 = jnp.float32
BF16 = jnp.bfloat16
HIGHEST = lax.Precision.HIGHEST

D_MODEL = 1024
D_HYENA = 512
HYENA_GROUPS = 8
D_HGRN = 512
HGRN_HEADS = 4
HEAD_DIM = 128
D_IN = 3 * D_HYENA + 5 * D_HGRN
D_FF = 4 * D_MODEL
FILTER_EMB = 33
FILTER_BANDS = 16
FILTER_HIDDEN = 64
DECAY_TARGET = 1e-2
FAST_DECAY_PCT = 0.3
SLOW_DECAY_PCT = 1.5
EPS = 1e-6

SUBLANES = 8
LANES = 128
VMEM_LIMIT_BYTES = 56 * 1024 * 1024

FFT_FAST = 128
FFT_KS_GROUP = 8
FFN_HALO = 16
SCAN_CHUNK = 128
SCAN_LEVELS = (64, 32, 16, 8, 4, 2, 1)


def _params(*semantics):
    return pltpu.CompilerParams(dimension_semantics=semantics,
                                vmem_limit_bytes=VMEM_LIMIT_BYTES)


def _rms_scale(x):
    return lax.rsqrt(jnp.mean(x * x, axis=-1, keepdims=True) + EPS)


def _split_dot(a, b_exact):
    hi = a.astype(BF16)
    lo = (a - hi.astype(F32)).astype(BF16)
    return (jnp.dot(hi, b_exact, preferred_element_type=F32)
            + jnp.dot(lo, b_exact, preferred_element_type=F32))


def _in_proj_kernel(x_ref, g_ref, w_ref, o_ref):
    x = x_ref[...]
    h = (x * _rms_scale(x) * g_ref[...]).astype(BF16)
    o_ref[...] = jnp.dot(h, w_ref[...], preferred_element_type=F32)


def _in_proj(x2d, gain, w_bf16, tm=1024, tn=1024):
    T = x2d.shape[0]
    N = w_bf16.shape[1]
    return pl.pallas_call(
        _in_proj_kernel,
        grid=(T // tm, N // tn),
        in_specs=[
            pl.BlockSpec((tm, D_MODEL), lambda i, j: (i, 0)),
            pl.BlockSpec((1, D_MODEL), lambda i, j: (0, 0)),
            pl.BlockSpec((D_MODEL, tn), lambda i, j: (0, j)),
        ],
        out_specs=pl.BlockSpec((tm, tn), lambda i, j: (i, j)),
        out_shape=jax.ShapeDtypeStruct((T, N), F32),
        compiler_params=_params("parallel", "arbitrary"),
    )(x2d, gain.reshape(1, D_MODEL), w_bf16)


def _shift_rows(main, prev_row, next_row):
    tm = main.shape[0]
    rows = lax.broadcasted_iota(jnp.int32, (tm, 1), 0)
    down = jnp.where(rows == 0, prev_row, pltpu.roll(main, 1, 0))
    up = jnp.where(rows == tm - 1, next_row, pltpu.roll(main, tm - 1, 0))
    return down, up


def _hyena_pre_kernel(prev_ref, main_ref, next_ref, w_ref, b_ref, x0_ref, z_ref,
                      *, tiles_per_seq):
    pos = pl.program_id(0) % tiles_per_seq
    main = main_ref[...]
    prev_row = jnp.where(pos == 0, 0.0, prev_ref[SUBLANES - 1:SUBLANES, :])
    next_row = jnp.where(pos == tiles_per_seq - 1, 0.0, next_ref[0:1, :])
    down, up = _shift_rows(main, prev_row, next_row)
    w = w_ref[...]
    u = down * w[0:1, :] + main * w[1:2, :] + up * w[2:3, :] + b_ref[...]
    x0_ref[...] = u[:, :D_HYENA]
    z_ref[...] = u[:, D_HYENA:2 * D_HYENA] * u[:, 2 * D_HYENA:]


def _hyena_pre(proj, conv_w, conv_b, seq_len, tm=512):
    T = proj.shape[0]
    width = 3 * D_HYENA
    halo_per_tile = tm // SUBLANES
    last_halo = T // SUBLANES - 1
    out = jax.ShapeDtypeStruct((T, D_HYENA), F32)
    return pl.pallas_call(
        functools.partial(_hyena_pre_kernel, tiles_per_seq=seq_len // tm),
        grid=(T // tm,),
        in_specs=[
            pl.BlockSpec((SUBLANES, width),
                         lambda i: (jnp.maximum(i * halo_per_tile - 1, 0), 0)),
            pl.BlockSpec((tm, width), lambda i: (i, 0)),
            pl.BlockSpec((SUBLANES, width),
                         lambda i: (jnp.minimum((i + 1) * halo_per_tile, last_halo), 0)),
            pl.BlockSpec((3, width), lambda i: (0, 0)),
            pl.BlockSpec((1, width), lambda i: (0, 0)),
        ],
        out_specs=[pl.BlockSpec((tm, D_HYENA), lambda i: (i, 0)),
                   pl.BlockSpec((tm, D_HYENA), lambda i: (i, 0))],
        out_shape=[out, out],
        compiler_params=_params("parallel"),
    )(proj, proj, proj, conv_w, conv_b.reshape(1, width))


def _filter_kernel(band_ref, w1t_ref, w1c_ref, w1s_ref, b1_ref, w2_ref, b2_ref, w3_ref, b3_ref,
                   w4_ref, freq_ref, delta_ref, o_ref, *, seq_len, tr):
    row0 = pl.program_id(0) * tr

    def tap_index(shape, axis):
        j = row0 + lax.broadcasted_iota(jnp.int32, shape, axis)
        return j, jnp.where(j < seq_len, j, 2 * seq_len - j).astype(F32)

    _, idx = tap_index((1, tr), 1)
    t = idx * (1.0 / (seq_len - 1))
    arg = band_ref[...] * ((2.0 * math.pi / seq_len) * idx)
    fr = freq_ref[...]

    def dense(w_ref_, a):
        return jnp.dot(w_ref_[...], a, precision=HIGHEST, preferred_element_type=F32)

    h = w1t_ref[...] * t + dense(w1c_ref, jnp.cos(arg)) - dense(w1s_ref, jnp.sin(arg))
    h = jnp.sin(fr * (h + b1_ref[...]))
    h = jnp.sin(fr * (dense(w2_ref, h) + b2_ref[...]))
    h = jnp.sin(fr * (dense(w3_ref, h) + b3_ref[...]))
    out = lax.dot_general(h, w4_ref[...], (((0,), (0,)), ((), ())), precision=HIGHEST,
                          preferred_element_type=F32)
    j_col, idx_col = tap_index((tr, 1), 0)
    window = jnp.exp(-(idx_col * (1.0 / (seq_len - 1))) * delta_ref[...])
    o_ref[...] = jnp.where(j_col == seq_len, 0.0, out * window)


def _hyena_filter(seq_len, w1, b1, w2, b2, w3, b3, w4, freq, tr=1024):
    n_taps = 2 * seq_len
    bands = np.linspace(1e-4, FILTER_BANDS - 1, FILTER_BANDS, dtype=np.float32).reshape(-1, 1)
    deltas = np.abs(np.linspace(math.log(DECAY_TARGET) / SLOW_DECAY_PCT,
                                math.log(DECAY_TARGET) / FAST_DECAY_PCT, D_HYENA,
                                dtype=np.float32)).reshape(1, D_HYENA)
    tiles_fwd = seq_len // tr
    const = lambda i: (0, 0)
    col = lambda v: v.reshape(-1, 1)
    full = lambda a: pl.BlockSpec(a.shape, const)
    operands = [jnp.asarray(bands), col(w1[0]), w1[1:1 + FILTER_BANDS].T,
                w1[1 + FILTER_BANDS:].T, col(b1), w2.T, col(b2), w3.T, col(b3)]
    return pl.pallas_call(
        functools.partial(_filter_kernel, seq_len=seq_len, tr=tr),
        grid=(n_taps // tr,),
        in_specs=[full(a) for a in operands] + [
            pl.BlockSpec((FILTER_HIDDEN, D_HYENA), lambda i: (0, i // tiles_fwd)),
            pl.BlockSpec((FILTER_HIDDEN, 1), const),
            pl.BlockSpec((1, D_HYENA), const),
        ],
        out_specs=pl.BlockSpec((tr, D_HYENA), lambda i: (i, 0)),
        out_shape=jax.ShapeDtypeStruct((n_taps, D_HYENA), F32),
        compiler_params=_params("parallel"),
    )(*operands, w4, col(freq), jnp.asarray(deltas))


class _FftPlan:
    def __init__(self, seq_len):
        self.n = 2 * seq_len
        self.fast = FFT_FAST
        self.slow = self.n // FFT_FAST
        self.slow_half = self.slow // 2
        self.ks = self.slow_half + 1
        self.ks_pad = -(-self.ks // FFT_KS_GROUP) * FFT_KS_GROUP
        S, F, N = self.slow, self.fast, self.n
        ks = np.arange(self.ks_pad, dtype=np.float64)[:, None]
        valid = (ks < self.ks)

        def stage_a(n_s):
            s = np.arange(n_s, dtype=np.float64)[None, :]
            ang = 2.0 * np.pi * ks * s / S
            return np.concatenate([np.where(valid, np.cos(ang), 0.0),
                                   np.where(valid, -np.sin(ang), 0.0)], axis=0)

        self.a_half = stage_a(self.slow_half).astype(np.float32)
        self.a_full = stage_a(self.slow).astype(np.float32)
        s = np.arange(self.slow_half, dtype=np.float64)[:, None]
        kk = np.arange(self.ks_pad, dtype=np.float64)[None, :]
        weight = np.where((kk == 0) | (kk == self.slow_half), 1.0, 2.0) * (kk < self.ks) / N
        ang = 2.0 * np.pi * s * kk / S
        self.a_inv = np.concatenate([weight * np.cos(ang), -weight * np.sin(ang)],
                                    axis=1).astype(np.float32)
        f = np.arange(F, dtype=np.float64)
        ang = 2.0 * np.pi * np.outer(np.arange(self.ks_pad, dtype=np.float64), f) / N
        self.tw_re = np.cos(ang).reshape(-1, 1).astype(np.float32)
        self.tw_im = (-np.sin(ang)).reshape(-1, 1).astype(np.float32)
        ang = 2.0 * np.pi * np.outer(f, f) / F
        self.f_re = np.cos(ang).astype(np.float32)
        self.f_im = (-np.sin(ang)).astype(np.float32)


def _hi_lo(m):
    m = jnp.asarray(m, F32)
    hi = m.astype(BF16)
    return jnp.stack([hi, (m - hi.astype(F32)).astype(BF16)])


def _dot_split(a_ref, x):
    a_hi, a_lo = a_ref[0], a_ref[1]
    x_hi = x.astype(BF16)
    x_lo = (x - x_hi.astype(F32)).astype(BF16)
    return (jnp.dot(a_hi, x_hi, preferred_element_type=F32)
            + jnp.dot(a_lo, x_hi, preferred_element_type=F32)
            + jnp.dot(a_hi, x_lo, preferred_element_type=F32))


def _left_matmul_kernel(a_ref, x_ref, o_ref):
    o_ref[0] = _dot_split(a_ref, x_ref[0])


def _left_matmul(a, x, tn=2048):
    B, K, N = x.shape
    M = a.shape[0]
    return pl.pallas_call(
        _left_matmul_kernel,
        grid=(B, N // tn),
        in_specs=[pl.BlockSpec((2, M, K), lambda b, j: (0, 0, 0)),
                  pl.BlockSpec((1, K, tn), lambda b, j: (b, 0, j))],
        out_specs=pl.BlockSpec((1, M, tn), lambda b, j: (b, 0, j)),
        out_shape=jax.ShapeDtypeStruct((B, M, N), F32),
        compiler_params=_params("parallel", "parallel"),
    )(_hi_lo(a), x)


def _stage_b_tile(fr_ref, fi_ref, re, im, conj=False):
    both = jnp.concatenate([re, im], axis=-1)
    pr = _dot_split(fr_ref, both)
    pi = _dot_split(fi_ref, both)
    w = re.shape[-1]
    if conj:
        return pr[:, :w] + pi[:, w:], pr[:, w:] - pi[:, :w]
    return pr[:, :w] - pi[:, w:], pr[:, w:] + pi[:, :w]


def _fft_fwd_kernel(a_ref, twr_ref, twi_ref, fr_ref, fi_ref, o_ref):
    for g in range(FFT_KS_GROUP):
        rows = slice(g * FFT_FAST, (g + 1) * FFT_FAST)
        ar, ai = a_ref[0, 0, rows, :], a_ref[0, 1, rows, :]
        tr, ti = twr_ref[rows, :], twi_ref[rows, :]
        xr, xi = _stage_b_tile(fr_ref, fi_ref, ar * tr - ai * ti, ar * ti + ai * tr)
        o_ref[0, 0, rows, :] = xr
        o_ref[0, 1, rows, :] = xi


def _fft_conv_kernel(a_ref, k_ref, twr_ref, twi_ref, fr_ref, fi_ref, o_ref):
    for g in range(FFT_KS_GROUP):
        rows = slice(g * FFT_FAST, (g + 1) * FFT_FAST)
        ar, ai = a_ref[0, 0, rows, :], a_ref[0, 1, rows, :]
        tr, ti = twr_ref[rows, :], twi_ref[rows, :]
        xr, xi = _stage_b_tile(fr_ref, fi_ref, ar * tr - ai * ti, ar * ti + ai * tr)
        kr, ki = k_ref[0, 0, rows, :], k_ref[0, 1, rows, :]
        yr, yi = xr * kr - xi * ki, xr * ki + xi * kr
        br, bi = _stage_b_tile(fr_ref, fi_ref, yr, yi, conj=True)
        o_ref[0, 0, rows, :] = br * tr + bi * ti
        o_ref[0, 1, rows, :] = bi * tr - br * ti


def _fft_stage_b(plan, a, k=None):
    B, _, R, C = a.shape
    rows = FFT_KS_GROUP * FFT_FAST
    data = pl.BlockSpec((1, 2, rows, LANES), lambda c, r, b: (b, 0, r, c))
    tw = pl.BlockSpec((rows, LANES), lambda c, r, b: (r, 0))
    mat = pl.BlockSpec((2, FFT_FAST, FFT_FAST), lambda c, r, b: (0, 0, 0))
    lane_bcast = lambda col: jnp.broadcast_to(jnp.asarray(col), (col.shape[0], LANES))
    consts = (lane_bcast(plan.tw_re), lane_bcast(plan.tw_im),
              _hi_lo(plan.f_re), _hi_lo(plan.f_im))
    common = dict(
        grid=(C // LANES, R // rows, B),
        out_specs=data,
        out_shape=jax.ShapeDtypeStruct(a.shape, F32),
        compiler_params=_params("parallel", "parallel", "arbitrary"),
    )
    if k is None:
        return pl.pallas_call(_fft_fwd_kernel, in_specs=[data, tw, tw, mat, mat],
                              **common)(a, *consts)
    kspec = pl.BlockSpec((1, 2, rows, LANES), lambda c, r, b: (0, 0, r, c))
    return pl.pallas_call(_fft_conv_kernel, in_specs=[data, kspec, tw, tw, mat, mat],
                          **common)(a, k, *consts)


def _fft_long_conv(z, kern, plan):
    B, L, C = z.shape
    S, F, P = plan.slow, plan.fast, plan.ks_pad
    ka = _left_matmul(plan.a_full, kern.reshape(1, S, F * C))
    k_spec = _fft_stage_b(plan, ka.reshape(1, 2, P * F, C))
    za = _left_matmul(plan.a_half, z.reshape(B, S // 2, F * C))
    ya = _fft_stage_b(plan, za.reshape(B, 2, P * F, C), k_spec)
    y = _left_matmul(plan.a_inv, ya.reshape(B, 2 * P, F * C))
    return y.reshape(B, L, C)


def _scan_constants():
    C = SCAN_CHUNK
    n_lv = len(SCAN_LEVELS)
    tri = np.tril(np.ones((C, C), np.float32))
    upper = np.zeros((n_lv, C, 1), np.float32)
    mask = np.zeros((n_lv + 1, C, C), np.float32)
    for li, h in enumerate(SCAN_LEVELS):
        for t in range(C):
            r = (t // (2 * h)) * 2 * h + h
            if t % (2 * h) >= h:
                upper[li, t, 0] = 1.0
                mask[li, t, r - h:r] = 1.0
    mask[n_lv] = np.eye(C, dtype=np.float32)
    flip = lambda m: m[:, ::-1, ::-1]
    return (np.stack([tri, tri[::-1, ::-1]]), np.stack([upper, upper[:, ::-1]]),
            np.stack([mask, flip(mask)]))


def _ref_rows(h, reverse):
    rows = []
    for g in range(SCAN_CHUNK // SUBLANES):
        pair = []
        for t in (g * SUBLANES, g * SUBLANES + SUBLANES // 2):
            start = (t // (2 * h)) * 2 * h
            pair.append(start + h if reverse else start + h - 1)
        rows.append(tuple(pair))
    return rows


def _log_sigmoid(z):
    return jnp.minimum(z, 0.0) - jnp.log(1.0 + jnp.exp(-jnp.abs(z)))


def _scan_chunk(q_ref, f_ref, v_ref, lb_table, tri, up_ref, mask_ref, o_ref, b_ref, state_ref,
                *, layer, reverse):
    n_lv = len(SCAN_LEVELS)
    C = SCAN_CHUNK
    neg_log2e = -math.log2(math.e)

    e = jnp.exp(lb_table - jnp.max(lb_table, axis=0, keepdims=True))
    prob = e / jnp.sum(e, axis=0, keepdims=True)
    lb = jnp.zeros((1, D_HGRN), F32)
    for l in range(1, layer + 1):
        lb = lb + prob[l:l + 1, :]

    q = q_ref[...]
    q = q / (1.0 + jnp.exp(-q))
    z = f_ref[...]
    log_a = jnp.log(lb)
    log_b = jnp.log1p(-lb) + _log_sigmoid(z)
    gate = jnp.maximum(log_a, log_b) + jnp.log(1.0 + jnp.exp(-jnp.abs(log_a - log_b)))
    key = (1.0 - lb) / (1.0 + jnp.exp(z))
    vb = v_ref[...].astype(BF16)

    g1 = gate.astype(BF16)
    rem = gate - g1.astype(F32)
    g2 = rem.astype(BF16)
    g3 = (rem - g2.astype(F32)).astype(BF16)
    b_inc = (jnp.dot(tri, g1, preferred_element_type=F32)
             + jnp.dot(tri, g2, preferred_element_type=F32)
             + jnp.dot(tri, g3, preferred_element_type=F32))
    b_ref[...] = b_inc

    def bcast_row(r):
        return jnp.broadcast_to(b_ref[pl.ds(r, 1), :], (SUBLANES, D_HGRN))

    first_half = lax.broadcasted_iota(jnp.int32, (SUBLANES, 1), 0) < SUBLANES // 2
    scores = [None] * HGRN_HEADS
    for li, h in enumerate(SCAN_LEVELS):
        is_query = up_ref[li] > 0.5
        if h == 1:
            x = jnp.where(is_query, q * (1.0 - key), key)
        else:
            groups = []
            for r0, r1 in _ref_rows(h, reverse):
                ref = bcast_row(r0)
                groups.append(ref if r1 == r0 else jnp.where(first_half, ref, bcast_row(r1)))
            dist = jnp.abs(b_inc - jnp.concatenate(groups, axis=0))
            x = jnp.where(is_query, q, key) * jnp.exp2(dist * neg_log2e)
        xb = x.astype(BF16)
        keep = mask_ref[li] > 0.5
        for hd in range(HGRN_HEADS):
            xh = xb[:, hd * HEAD_DIM:(hd + 1) * HEAD_DIM]
            p = lax.dot_general(xh, xh, (((1,), (1,)), ((), ())), preferred_element_type=F32)
            scores[hd] = jnp.where(keep, p, 0.0 if scores[hd] is None else scores[hd])

    b_total = bcast_row(0 if reverse else C - 1)
    b_rest = jnp.concatenate([b_total] * (C // SUBLANES), axis=0) - b_inc
    q_dec = (q * jnp.exp2(b_inc * -neg_log2e)).astype(BF16)
    k_dec = (key * jnp.exp2(b_rest * -neg_log2e)).astype(BF16)
    total = jnp.exp(b_total[0:1, :])
    qb, kb = q.astype(BF16), key.astype(BF16)
    on_diag = mask_ref[n_lv] > 0.5
    for hd in range(HGRN_HEADS):
        cols = slice(hd * HEAD_DIM, (hd + 1) * HEAD_DIM)
        diag = lax.dot_general(qb[:, cols], kb[:, cols], (((1,), (1,)), ((), ())),
                               preferred_element_type=F32)
        p = jnp.where(on_diag, diag, scores[hd]).astype(BF16)
        st = state_ref[hd]
        o = jnp.dot(p, vb[:, cols], preferred_element_type=F32)
        o = o + lax.dot_general(q_dec[:, cols], st.astype(BF16), (((1,), (1,)), ((), ())),
                                preferred_element_type=F32)
        o_ref[:, cols] = o
        upd = lax.dot_general(vb[:, cols], k_dec[:, cols], (((0,), (0,)), ((), ())),
                              preferred_element_type=F32)
        state_ref[hd] = st * total[:, cols] + upd


def _hgrn_scan_kernel(qf_ref, ff_ref, vf_ref, qb_ref, fb_ref, vb_ref, lbt_ref, tri_ref, up_ref,
                      mask_ref, of_ref, ob_ref, b_ref, state_ref, *, layer):
    @pl.when(pl.program_id(1) == 0)
    def _():
        state_ref[...] = jnp.zeros_like(state_ref)

    _scan_chunk(qf_ref, ff_ref, vf_ref, lbt_ref[0], tri_ref[0], up_ref.at[0], mask_ref.at[0],
                of_ref, b_ref.at[0], state_ref.at[0], layer=layer, reverse=False)
    _scan_chunk(qb_ref, fb_ref, vb_ref, lbt_ref[1], tri_ref[1], up_ref.at[1], mask_ref.at[1],
                ob_ref, b_ref.at[1], state_ref.at[1], layer=layer, reverse=True)


def _hgrn_scan(proj, lb_table, layer, batch, seq_len):
    T = proj.shape[0]
    C = SCAN_CHUNK
    n_chunks = seq_len // C
    tri, upper, mask = _scan_constants()
    col0 = (3 * D_HYENA) // D_HGRN

    fwd = lambda b, c: b * n_chunks + c
    bwd = lambda b, c: b * n_chunks + n_chunks - 1 - c
    chunk = lambda rows, col: pl.BlockSpec((C, D_HGRN), lambda b, c: (rows(b, c), col))
    whole = lambda a: pl.BlockSpec(a.shape, lambda b, c: (0,) * a.ndim)
    out = jax.ShapeDtypeStruct((T, D_HGRN), F32)
    return pl.pallas_call(
        functools.partial(_hgrn_scan_kernel, layer=layer),
        grid=(batch, n_chunks),
        in_specs=[
            chunk(fwd, col0), chunk(fwd, col0 + 1), chunk(fwd, col0 + 3),
            chunk(bwd, col0), chunk(bwd, col0 + 2), chunk(bwd, col0 + 3),
            whole(lb_table), whole(tri), whole(upper), whole(mask),
        ],
        out_specs=[chunk(fwd, 0), chunk(bwd, 0)],
        out_shape=[out, out],
        scratch_shapes=[pltpu.VMEM((2, C, D_HGRN), F32),
                        pltpu.VMEM((2, HGRN_HEADS, HEAD_DIM, HEAD_DIM), F32)],
        compiler_params=_params("parallel", "arbitrary"),
    )(proj, proj, proj, proj, proj, proj, lb_table, jnp.asarray(tri, BF16), jnp.asarray(upper),
      jnp.asarray(mask))


def _group_mean_matrix(group):
    idx = np.arange(D_HYENA) // group
    return (idx[:, None] == idx[None, :]).astype(np.float32)


def _mix_out_kernel(x_ref, y_ref, z_ref, x0_ref, of_ref, ob_ref, g_ref, skip_ref,
                    hy_gain_ref, hg_gain_ref, grp_hy_ref, grp_hg_ref, w_ref, post_ref, o_ref):
    z = z_ref[...]
    yh = x0_ref[...] * (y_ref[...] + skip_ref[...] * z)
    ms = _split_dot(yh * yh, grp_hy_ref[...]) * (HYENA_GROUPS / D_HYENA)
    yh = yh * lax.rsqrt(ms + EPS) * hy_gain_ref[...]
    o = of_ref[...] + ob_ref[...]
    ms = _split_dot(o * o, grp_hg_ref[...]) * (1.0 / HEAD_DIM)
    g = g_ref[...]
    o = o * lax.rsqrt(ms + EPS) * hg_gain_ref[...] * (g * (1.0 / (1.0 + jnp.exp(-g))))
    mix = (jnp.dot(yh.astype(BF16), w_ref[:D_HYENA, :], preferred_element_type=F32)
           + jnp.dot(o.astype(BF16), w_ref[D_HYENA:, :], preferred_element_type=F32))
    o_ref[...] = x_ref[...] + mix * _rms_scale(mix) * post_ref[...]


def _mix_out(x2d, y, z, x0, o_fwd, o_bwd, proj, skip, hy_gain, hg_gain, w_bf16, post_gain,
             tm=512):
    T = x2d.shape[0]
    gate_col = D_IN // D_HGRN - 1
    half = lambda: pl.BlockSpec((tm, D_HYENA), lambda i: (i, 0))
    vec = lambda n: pl.BlockSpec((1, n), lambda i: (0, 0))
    sq = lambda: pl.BlockSpec((D_HYENA, D_HYENA), lambda i: (0, 0))
    row = lambda v: v.reshape(1, -1)
    return pl.pallas_call(
        _mix_out_kernel,
        grid=(T // tm,),
        in_specs=[
            pl.BlockSpec((tm, D_MODEL), lambda i: (i, 0)),
            half(), half(), half(), half(), half(),
            pl.BlockSpec((tm, D_HGRN), lambda i: (i, gate_col)),
            vec(D_HYENA), vec(D_HYENA), vec(D_HGRN), sq(), sq(),
            pl.BlockSpec((D_MODEL, D_MODEL), lambda i: (0, 0)),
            vec(D_MODEL),
        ],
        out_specs=pl.BlockSpec((tm, D_MODEL), lambda i: (i, 0)),
        out_shape=jax.ShapeDtypeStruct((T, D_MODEL), F32),
        compiler_params=_params("parallel"),
    )(x2d, y, z, x0, o_fwd, o_bwd, proj, row(skip), row(hy_gain), row(hg_gain),
      jnp.asarray(_group_mean_matrix(D_HYENA // HYENA_GROUPS), BF16),
      jnp.asarray(_group_mean_matrix(HEAD_DIM), BF16), w_bf16, row(post_gain))


def _gelu_tanh(x):
    c = -2.0 * math.sqrt(2.0 / math.pi) * math.log2(math.e)
    return x / (1.0 + jnp.exp2(x * (c + (0.044715 * c) * (x * x))))


def _ffn_kernel(prev_ref, main_ref, next_ref, pre_ref, wa_ref, wb_ref, cwa_ref, cwb_ref,
                cba_ref, cbb_ref, wd_ref, post_ref, o_ref, h_ref, acc_ref, ua_ref, ub_ref,
                *, tiles_per_seq, tm):
    j = pl.program_id(1)
    pos = pl.program_id(0) % tiles_per_seq

    @pl.when(j == 0)
    def _():
        gain = pre_ref[...]

        def normed(x):
            return (x * _rms_scale(x) * gain).astype(BF16)

        prev = normed(prev_ref[...])
        nxt = normed(next_ref[...])
        h_ref[0:FFN_HALO, :] = jnp.where(pos == 0, jnp.zeros_like(prev), prev)
        h_ref[FFN_HALO:FFN_HALO + tm, :] = normed(main_ref[...])
        h_ref[FFN_HALO + tm:, :] = jnp.where(pos == tiles_per_seq - 1, jnp.zeros_like(nxt), nxt)
        acc_ref[...] = jnp.zeros_like(acc_ref)

    h = h_ref[...]

    def conv_part(w_ref_, cw_ref_, cb_ref_, u_ref):
        u_ref[...] = jnp.dot(h, w_ref_[...], preferred_element_type=F32)
        cw = cw_ref_[...]
        down = u_ref[pl.ds(FFN_HALO - 1, tm), :]
        mid = u_ref[pl.ds(FFN_HALO, tm), :]
        up = u_ref[pl.ds(FFN_HALO + 1, tm), :]
        return down * cw[0:1, :] + mid * cw[1:2, :] + up * cw[2:3, :] + cb_ref_[...]

    a = conv_part(wa_ref, cwa_ref, cba_ref, ua_ref)
    b = conv_part(wb_ref, cwb_ref, cbb_ref, ub_ref)
    act = (_gelu_tanh(a) * b).astype(BF16)
    acc_ref[...] += jnp.dot(act, wd_ref[...], preferred_element_type=F32)

    @pl.when(j == pl.num_programs(1) - 1)
    def _():
        ff = acc_ref[...]
        o_ref[...] = main_ref[...] + ff * _rms_scale(ff) * post_ref[...]


def _ffn(x2d, pre_gain, w_up_bf16, conv_w, conv_b, w_down_bf16, post_gain, seq_len,
         tm=1024, tf=512):
    T = x2d.shape[0]
    halo_per_tile = tm // FFN_HALO
    last_halo = T // FFN_HALO - 1
    n_f = D_FF // tf
    row = lambda v: v.reshape(1, -1)
    return pl.pallas_call(
        functools.partial(_ffn_kernel, tiles_per_seq=seq_len // tm, tm=tm),
        grid=(T // tm, n_f),
        in_specs=[
            pl.BlockSpec((FFN_HALO, D_MODEL),
                         lambda i, j: (jnp.maximum(i * halo_per_tile - 1, 0), 0)),
            pl.BlockSpec((tm, D_MODEL), lambda i, j: (i, 0)),
            pl.BlockSpec((FFN_HALO, D_MODEL),
                         lambda i, j: (jnp.minimum((i + 1) * halo_per_tile, last_halo), 0)),
            pl.BlockSpec((1, D_MODEL), lambda i, j: (0, 0)),
            pl.BlockSpec((D_MODEL, tf), lambda i, j: (0, j)),
            pl.BlockSpec((D_MODEL, tf), lambda i, j: (0, n_f + j)),
            pl.BlockSpec((3, tf), lambda i, j: (0, j)),
            pl.BlockSpec((3, tf), lambda i, j: (0, n_f + j)),
            pl.BlockSpec((1, tf), lambda i, j: (0, j)),
            pl.BlockSpec((1, tf), lambda i, j: (0, n_f + j)),
            pl.BlockSpec((tf, D_MODEL), lambda i, j: (j, 0)),
            pl.BlockSpec((1, D_MODEL), lambda i, j: (0, 0)),
        ],
        out_specs=pl.BlockSpec((tm, D_MODEL), lambda i, j: (i, 0)),
        out_shape=jax.ShapeDtypeStruct((T, D_MODEL), F32),
        scratch_shapes=[pltpu.VMEM((tm + 2 * FFN_HALO, D_MODEL), BF16),
                        pltpu.VMEM((tm, D_MODEL), F32),
                        pltpu.VMEM((tm + 2 * FFN_HALO, tf), F32),
                        pltpu.VMEM((tm + 2 * FFN_HALO, tf), F32)],
        compiler_params=_params("parallel", "arbitrary"),
    )(x2d, x2d, x2d, row(pre_gain), w_up_bf16, w_up_bf16, conv_w, conv_w,
      row(conv_b), row(conv_b), w_down_bf16, row(post_gain))


def _trunk(x, p):
    B, L, _ = x.shape
    x2d = x.reshape(B * L, D_MODEL)
    plan = _FftPlan(L)
    for l in range(p["w_in"].shape[0]):
        proj = _in_proj(x2d, p["norm_mix_pre"][l], p["w_in_bf16"][l])
        x0, z = _hyena_pre(proj, p["hyena_conv_w"][l], p["hyena_conv_b"][l], L)
        kern = _hyena_filter(L, p["filt_w1"][l], p["filt_b1"][l], p["filt_w2"][l], p["filt_b2"][l],
                             p["filt_w3"][l], p["filt_b3"][l], p["filt_w4"][l], p["filt_freq"][l])
        y = _fft_long_conv(z.reshape(B, L, D_HYENA), kern, plan).reshape(B * L, D_HYENA)
        o_fwd, o_bwd = _hgrn_scan(proj, p["hgrn_lower_bounds"], l, B, L)
        x2d = _mix_out(x2d, y, z, x0, o_fwd, o_bwd, proj, p["hyena_skip"][l], p["hyena_out_norm"][l],
                       p["hgrn_out_norm"][l], p["w_out_bf16"][l], p["norm_mix_post"][l])
        x2d = _ffn(x2d, p["norm_ffn_pre"][l], p["ffn_w_up_bf16"][l], p["ffn_conv_w"][l],
                   p["ffn_conv_b"][l], p["ffn_w_down_bf16"][l], p["norm_ffn_post"][l], L)
    return x2d.reshape(B, L, D_MODEL)


def kernel(x_prompt, x_sample, norm_mix_pre, norm_mix_post, norm_ffn_pre, norm_ffn_post, w_in, hyena_conv_w, hyena_conv_b, filt_w1, filt_b1, filt_w2, filt_b2, filt_w3, filt_b3, filt_w4, filt_freq, hyena_skip, hyena_out_norm, hgrn_lower_bounds, hgrn_out_norm, w_out, ffn_w_up, ffn_conv_w, ffn_conv_b, ffn_w_down):
    p = dict(
        norm_mix_pre=norm_mix_pre, norm_mix_post=norm_mix_post, norm_ffn_pre=norm_ffn_pre,
        norm_ffn_post=norm_ffn_post, w_in=w_in, hyena_conv_w=hyena_conv_w,
        hyena_conv_b=hyena_conv_b, filt_w1=filt_w1, filt_b1=filt_b1, filt_w2=filt_w2,
        filt_b2=filt_b2, filt_w3=filt_w3, filt_b3=filt_b3, filt_w4=filt_w4, filt_freq=filt_freq,
        hyena_skip=hyena_skip, hyena_out_norm=hyena_out_norm,
        hgrn_lower_bounds=hgrn_lower_bounds, hgrn_out_norm=hgrn_out_norm,
        ffn_conv_w=ffn_conv_w, ffn_conv_b=ffn_conv_b,
        w_in_bf16=w_in.astype(BF16), w_out_bf16=w_out.astype(BF16),
        ffn_w_up_bf16=ffn_w_up.astype(BF16), ffn_w_down_bf16=ffn_w_down.astype(BF16),
    )
    return (_trunk(x_prompt, p), _trunk(x_sample, p))
```

```python
import functools
import math

import numpy as np
import jax
import jax.numpy as jnp
from jax import lax
from jax.experimental import pallas as pl
from jax.experimental.pallas import tpu as pltpu

F32 = jnp.float32
BF16 = jnp.bfloat16
HIGHEST = lax.Precision.HIGHEST

D_MODEL = 1024
D_HYENA = 512
HYENA_GROUPS = 8
D_HGRN = 512
HGRN_HEADS = 4
HEAD_DIM = 128
D_IN = 3 * D_HYENA + 5 * D_HGRN
D_FF = 4 * D_MODEL
FILTER_EMB = 33
FILTER_BANDS = 16
FILTER_HIDDEN = 64
DECAY_TARGET = 1e-2
FAST_DECAY_PCT = 0.3
SLOW_DECAY_PCT = 1.5
EPS = 1e-6

SUBLANES = 8
LANES = 128
VMEM_LIMIT_BYTES = 56 * 1024 * 1024

FFT_FAST = 128
FFT_KS_GROUP = 8
FFT_F_TILE = 16
FFN_HALO = 16
SCAN_CHUNK = 128
SCAN_LEVELS = (64, 32, 16, 8, 4, 2, 1)


def _params(*semantics):
    return pltpu.CompilerParams(dimension_semantics=semantics,
                                vmem_limit_bytes=VMEM_LIMIT_BYTES)


def _rms_scale(x):
    return lax.rsqrt(jnp.mean(x * x, axis=-1, keepdims=True) + EPS)


def _split_dot(a, b_exact):
    hi = a.astype(BF16)
    lo = (a - hi.astype(F32)).astype(BF16)
    return (jnp.dot(hi, b_exact, preferred_element_type=F32)
            + jnp.dot(lo, b_exact, preferred_element_type=F32))


def _in_proj_kernel(x_ref, g_ref, w_ref, o_ref):
    x = x_ref[...]
    h = (x * _rms_scale(x) * g_ref[...]).astype(BF16)
    o_ref[...] = jnp.dot(h, w_ref[...], preferred_element_type=F32)


def _in_proj(x2d, gain, w_bf16, tm=1024, tn=1024):
    T = x2d.shape[0]
    N = w_bf16.shape[1]
    return pl.pallas_call(
        _in_proj_kernel,
        grid=(T // tm, N // tn),
        in_specs=[
            pl.BlockSpec((tm, D_MODEL), lambda i, j: (i, 0)),
            pl.BlockSpec((1, D_MODEL), lambda i, j: (0, 0)),
            pl.BlockSpec((D_MODEL, tn), lambda i, j: (0, j)),
        ],
        out_specs=pl.BlockSpec((tm, tn), lambda i, j: (i, j)),
        out_shape=jax.ShapeDtypeStruct((T, N), F32),
        compiler_params=_params("parallel", "arbitrary"),
    )(x2d, gain.reshape(1, D_MODEL), w_bf16)


def _shift_rows(main, prev_row, next_row):
    tm = main.shape[0]
    rows = lax.broadcasted_iota(jnp.int32, (tm, 1), 0)
    down = jnp.where(rows == 0, prev_row, pltpu.roll(main, 1, 0))
    up = jnp.where(rows == tm - 1, next_row, pltpu.roll(main, tm - 1, 0))
    return down, up


def _hyena_pre_kernel(prev_ref, main_ref, next_ref, w_ref, b_ref, x0_ref, z_ref,
                      *, tiles_per_seq):
    pos = pl.program_id(0) % tiles_per_seq
    main = main_ref[...]
    prev_row = jnp.where(pos == 0, 0.0, prev_ref[SUBLANES - 1:SUBLANES, :])
    next_row = jnp.where(pos == tiles_per_seq - 1, 0.0, next_ref[0:1, :])
    down, up = _shift_rows(main, prev_row, next_row)
    w = w_ref[...]
    u = down * w[0:1, :] + main * w[1:2, :] + up * w[2:3, :] + b_ref[...]
    x0_ref[...] = u[:, :D_HYENA]
    z_ref[...] = u[:, D_HYENA:2 * D_HYENA] * u[:, 2 * D_HYENA:]


def _hyena_pre(proj, conv_w, conv_b, seq_len, tm=512):
    T = proj.shape[0]
    width = 3 * D_HYENA
    halo_per_tile = tm // SUBLANES
    last_halo = T // SUBLANES - 1
    out = jax.ShapeDtypeStruct((T, D_HYENA), F32)
    return pl.pallas_call(
        functools.partial(_hyena_pre_kernel, tiles_per_seq=seq_len // tm),
        grid=(T // tm,),
        in_specs=[
            pl.BlockSpec((SUBLANES, width),
                         lambda i: (jnp.maximum(i * halo_per_tile - 1, 0), 0)),
            pl.BlockSpec((tm, width), lambda i: (i, 0)),
            pl.BlockSpec((SUBLANES, width),
                         lambda i: (jnp.minimum((i + 1) * halo_per_tile, last_halo), 0)),
            pl.BlockSpec((3, width), lambda i: (0, 0)),
            pl.BlockSpec((1, width), lambda i: (0, 0)),
        ],
        out_specs=[pl.BlockSpec((tm, D_HYENA), lambda i: (i, 0)),
                   pl.BlockSpec((tm, D_HYENA), lambda i: (i, 0))],
        out_shape=[out, out],
        compiler_params=_params("parallel"),
    )(proj, proj, proj, conv_w, conv_b.reshape(1, width))


def _filter_kernel(band_ref, w1t_ref, w1c_ref, w1s_ref, b1_ref, w2_ref, b2_ref, w3_ref, b3_ref,
                   w4_ref, freq_ref, delta_ref, o_ref, *, seq_len, tr):
    row0 = pl.program_id(0) * tr

    def tap_index(shape, axis):
        j = row0 + lax.broadcasted_iota(jnp.int32, shape, axis)
        return j, jnp.where(j < seq_len, j, 2 * seq_len - j).astype(F32)

    _, idx = tap_index((1, tr), 1)
    t = idx * (1.0 / (seq_len - 1))
    arg = band_ref[...] * ((2.0 * math.pi / seq_len) * idx)
    fr = freq_ref[...]

    def dense(w_ref_, a):
        return jnp.dot(w_ref_[...], a, precision=HIGHEST, preferred_element_type=F32)

    h = w1t_ref[...] * t + dense(w1c_ref, jnp.cos(arg)) - dense(w1s_ref, jnp.sin(arg))
    h = jnp.sin(fr * (h + b1_ref[...]))
    h = jnp.sin(fr * (dense(w2_ref, h) + b2_ref[...]))
    h = jnp.sin(fr * (dense(w3_ref, h) + b3_ref[...]))
    out = lax.dot_general(h, w4_ref[...], (((0,), (0,)), ((), ())), precision=HIGHEST,
                          preferred_element_type=F32)
    j_col, idx_col = tap_index((tr, 1), 0)
    window = jnp.exp(-(idx_col * (1.0 / (seq_len - 1))) * delta_ref[...])
    o_ref[...] = jnp.where(j_col == seq_len, 0.0, out * window)


def _hyena_filter(seq_len, w1, b1, w2, b2, w3, b3, w4, freq, tr=1024):
    n_taps = 2 * seq_len
    bands = np.linspace(1e-4, FILTER_BANDS - 1, FILTER_BANDS, dtype=np.float32).reshape(-1, 1)
    deltas = np.abs(np.linspace(math.log(DECAY_TARGET) / SLOW_DECAY_PCT,
                                math.log(DECAY_TARGET) / FAST_DECAY_PCT, D_HYENA,
                                dtype=np.float32)).reshape(1, D_HYENA)
    tiles_fwd = seq_len // tr
    const = lambda i: (0, 0)
    col = lambda v: v.reshape(-1, 1)
    full = lambda a: pl.BlockSpec(a.shape, const)
    operands = [jnp.asarray(bands), col(w1[0]), w1[1:1 + FILTER_BANDS].T,
                w1[1 + FILTER_BANDS:].T, col(b1), w2.T, col(b2), w3.T, col(b3)]
    return pl.pallas_call(
        functools.partial(_filter_kernel, seq_len=seq_len, tr=tr),
        grid=(n_taps // tr,),
        in_specs=[full(a) for a in operands] + [
            pl.BlockSpec((FILTER_HIDDEN, D_HYENA), lambda i: (0, i // tiles_fwd)),
            pl.BlockSpec((FILTER_HIDDEN, 1), const),
            pl.BlockSpec((1, D_HYENA), const),
        ],
        out_specs=pl.BlockSpec((tr, D_HYENA), lambda i: (i, 0)),
        out_shape=jax.ShapeDtypeStruct((n_taps, D_HYENA), F32),
        compiler_params=_params("parallel"),
    )(*operands, w4, col(freq), jnp.asarray(deltas))


class _FftPlan:
    def __init__(self, seq_len):
        self.n = 2 * seq_len
        self.fast = FFT_FAST
        self.slow = self.n // FFT_FAST
        self.slow_half = self.slow // 2
        self.ks = self.slow_half + 1
        self.ks_pad = -(-self.ks // FFT_KS_GROUP) * FFT_KS_GROUP
        S, F, N = self.slow, self.fast, self.n
        ks = np.arange(self.ks_pad, dtype=np.float64)[:, None]
        valid = (ks < self.ks)

        def stage_a(n_s):
            s = np.arange(n_s, dtype=np.float64)[None, :]
            ang = 2.0 * np.pi * ks * s / S
            return np.concatenate([np.where(valid, np.cos(ang), 0.0),
                                   np.where(valid, -np.sin(ang), 0.0)], axis=0)

        self.a_half = stage_a(self.slow_half).astype(np.float32)
        self.a_full = stage_a(self.slow).astype(np.float32)
        s = np.arange(self.slow_half, dtype=np.float64)[:, None]
        kk = np.arange(self.ks_pad, dtype=np.float64)[None, :]
        weight = np.where((kk == 0) | (kk == self.slow_half), 1.0, 2.0) * (kk < self.ks) / N
        ang = 2.0 * np.pi * s * kk / S
        self.a_inv = np.concatenate([weight * np.cos(ang), -weight * np.sin(ang)],
                                    axis=1).astype(np.float32)
        f = np.arange(F, dtype=np.float64)
        ang = 2.0 * np.pi * np.outer(np.arange(self.ks_pad, dtype=np.float64), f) / N
        self.tw_re = np.cos(ang).reshape(-1, 1).astype(np.float32)
        self.tw_im = (-np.sin(ang)).reshape(-1, 1).astype(np.float32)
        ang = 2.0 * np.pi * np.outer(f, f) / F
        self.f_re = np.cos(ang).astype(np.float32)
        self.f_im = (-np.sin(ang)).astype(np.float32)


def _dft_dot(a_ref, x):
    return jnp.dot(a_ref[...], x.astype(BF16), preferred_element_type=F32)


def _stage_a_kernel(a_ref, x_ref, o_ref, *, batch):
    m, k = a_ref.shape
    x2d = x_ref.reshape(batch * k * FFT_F_TILE, LANES)
    o2d = o_ref.reshape(batch * m * FFT_F_TILE, LANES)
    for b in range(batch):
        for j in range(FFT_F_TILE):
            x = x2d[pl.ds(b * k * FFT_F_TILE + j, k, stride=FFT_F_TILE), :]
            o2d[pl.ds(b * m * FFT_F_TILE + j, m, stride=FFT_F_TILE), :] = _dft_dot(a_ref, x)


def _stage_a(a, x, inverse):
    F, C = x.shape[-2:]
    B = x.shape[0]
    M, K = a.shape
    x_dims, o_dims = ((2, K // 2), (M,)) if inverse else ((K,), (2, M // 2))

    def spec(dims):
        zeros = (0,) * len(dims)
        return pl.BlockSpec((B,) + dims + (FFT_F_TILE, LANES), lambda f, c: (0,) + zeros + (f, c))

    return pl.pallas_call(
        functools.partial(_stage_a_kernel, batch=B),
        grid=(F // FFT_F_TILE, C // LANES),
        in_specs=[pl.BlockSpec((M, K), lambda f, c: (0, 0)), spec(x_dims)],
        out_specs=spec(o_dims),
        out_shape=jax.ShapeDtypeStruct((B,) + o_dims + (F, C), F32),
        compiler_params=_params("parallel", "parallel"),
    )(jnp.asarray(a, BF16), x)


def _stage_b_tile(fr_ref, fi_ref, re, im, conj=False):
    both = jnp.concatenate([re, im], axis=-1)
    pr = _dft_dot(fr_ref, both)
    pi = _dft_dot(fi_ref, both)
    w = re.shape[-1]
    if conj:
        return pr[:, :w] + pi[:, w:], pr[:, w:] - pi[:, :w]
    return pr[:, :w] - pi[:, w:], pr[:, w:] + pi[:, :w]


def _fft_fwd_kernel(a_ref, twr_ref, twi_ref, fr_ref, fi_ref, o_ref):
    for g in range(FFT_KS_GROUP):
        rows = slice(g * FFT_FAST, (g + 1) * FFT_FAST)
        ar, ai = a_ref[0, 0, rows, :], a_ref[0, 1, rows, :]
        tr, ti = twr_ref[rows, :], twi_ref[rows, :]
        xr, xi = _stage_b_tile(fr_ref, fi_ref, ar * tr - ai * ti, ar * ti + ai * tr)
        o_ref[0, 0, rows, :] = xr
        o_ref[0, 1, rows, :] = xi


def _fft_conv_kernel(a_ref, k_ref, twr_ref, twi_ref, fr_ref, fi_ref, o_ref):
    for g in range(FFT_KS_GROUP):
        rows = slice(g * FFT_FAST, (g + 1) * FFT_FAST)
        ar, ai = a_ref[0, 0, rows, :], a_ref[0, 1, rows, :]
        tr, ti = twr_ref[rows, :], twi_ref[rows, :]
        xr, xi = _stage_b_tile(fr_ref, fi_ref, ar * tr - ai * ti, ar * ti + ai * tr)
        kr, ki = k_ref[0, 0, rows, :], k_ref[0, 1, rows, :]
        yr, yi = xr * kr - xi * ki, xr * ki + xi * kr
        br, bi = _stage_b_tile(fr_ref, fi_ref, yr, yi, conj=True)
        o_ref[0, 0, rows, :] = br * tr + bi * ti
        o_ref[0, 1, rows, :] = bi * tr - br * ti


def _fft_stage_b(plan, a, k=None):
    B, _, R, C = a.shape
    rows = FFT_KS_GROUP * FFT_FAST
    data = pl.BlockSpec((1, 2, rows, LANES), lambda r, c, b: (b, 0, r, c))
    tw = pl.BlockSpec((rows, LANES), lambda r, c, b: (r, 0))
    mat = pl.BlockSpec((FFT_FAST, FFT_FAST), lambda r, c, b: (0, 0))
    lane_bcast = lambda col: jnp.broadcast_to(jnp.asarray(col), (col.shape[0], LANES))
    consts = (lane_bcast(plan.tw_re), lane_bcast(plan.tw_im),
              jnp.asarray(plan.f_re, BF16), jnp.asarray(plan.f_im, BF16))
    common = dict(
        grid=(R // rows, C // LANES, B),
        out_specs=data,
        out_shape=jax.ShapeDtypeStruct(a.shape, F32),
        compiler_params=_params("parallel", "parallel", "arbitrary"),
    )
    if k is None:
        return pl.pallas_call(_fft_fwd_kernel, in_specs=[data, tw, tw, mat, mat],
                              **common)(a, *consts)
    kspec = pl.BlockSpec((1, 2, rows, LANES), lambda r, c, b: (0, 0, r, c))
    return pl.pallas_call(_fft_conv_kernel, in_specs=[data, kspec, tw, tw, mat, mat],
                          **common)(a, k, *consts)


def _fft_long_conv(z, kern, plan):
    B, L, C = z.shape
    S, F, P = plan.slow, plan.fast, plan.ks_pad
    ka = _stage_a(plan.a_full, kern.reshape(1, S, F, C), inverse=False)
    k_spec = _fft_stage_b(plan, ka.reshape(1, 2, P * F, C))
    za = _stage_a(plan.a_half, z.reshape(B, S // 2, F, C), inverse=False)
    ya = _fft_stage_b(plan, za.reshape(B, 2, P * F, C), k_spec)
    y = _stage_a(plan.a_inv, ya.reshape(B, 2, P, F, C), inverse=True)
    return y.reshape(B, L, C)


def _scan_constants():
    C = SCAN_CHUNK
    n_lv = len(SCAN_LEVELS)
    tri = np.tril(np.ones((C, C), np.float32))
    upper = np.zeros((n_lv, C, 1), np.float32)
    mask = np.zeros((n_lv + 1, C, C), np.float32)
    for li, h in enumerate(SCAN_LEVELS):
        for t in range(C):
            r = (t // (2 * h)) * 2 * h + h
            if t % (2 * h) >= h:
                upper[li, t, 0] = 1.0
                mask[li, t, r - h:r] = 1.0
    mask[n_lv] = np.eye(C, dtype=np.float32)
    flip = lambda m: m[:, ::-1, ::-1]
    return (np.stack([tri, tri[::-1, ::-1]]), np.stack([upper, upper[:, ::-1]]),
            np.stack([mask, flip(mask)]))


def _ref_rows(h, reverse):
    rows = []
    for g in range(SCAN_CHUNK // SUBLANES):
        pair = []
        for t in (g * SUBLANES, g * SUBLANES + SUBLANES // 2):
            start = (t // (2 * h)) * 2 * h
            pair.append(start + h if reverse else start + h - 1)
        rows.append(tuple(pair))
    return rows


def _scan_chunk(q_ref, f_ref, v_ref, lb_table, tri, up_ref, mask_ref, o_ref, b_ref, state_ref,
                *, layer, reverse):
    n_lv = len(SCAN_LEVELS)
    C = SCAN_CHUNK
    neg_log2e = -math.log2(math.e)

    e = jnp.exp(lb_table - jnp.max(lb_table, axis=0, keepdims=True))
    prob = e / jnp.sum(e, axis=0, keepdims=True)
    lb = jnp.zeros((1, D_HGRN), F32)
    for l in range(1, layer + 1):
        lb = lb + prob[l:l + 1, :]

    q = q_ref[...]
    q = q / (1.0 + jnp.exp(-q))
    z = f_ref[...]
    w = jnp.exp(-jnp.abs(z))
    inv = 1.0 / (1.0 + w)
    log_sig = jnp.minimum(z, 0.0) + jnp.log(inv)
    key = (1.0 - lb) * (jnp.where(z >= 0.0, w, 1.0) * inv)
    log_a = jnp.log(lb)
    log_b = jnp.log1p(-lb) + log_sig
    gate = jnp.maximum(log_a, log_b) + jnp.log(1.0 + jnp.exp(-jnp.abs(log_a - log_b)))
    vb = v_ref[...].astype(BF16)

    g1 = gate.astype(BF16)
    rem = gate - g1.astype(F32)
    g2 = rem.astype(BF16)
    g3 = (rem - g2.astype(F32)).astype(BF16)
    b_inc = (jnp.dot(tri, g1, preferred_element_type=F32)
             + jnp.dot(tri, g2, preferred_element_type=F32)
             + jnp.dot(tri, g3, preferred_element_type=F32))
    b_ref[...] = b_inc

    def bcast_row(r):
        return jnp.broadcast_to(b_ref[pl.ds(r, 1), :], (SUBLANES, D_HGRN))

    first_half = lax.broadcasted_iota(jnp.int32, (SUBLANES, 1), 0) < SUBLANES // 2
    scores = [None] * HGRN_HEADS
    for li, h in enumerate(SCAN_LEVELS):
        is_query = up_ref[li] > 0.5
        if h == 1:
            x = jnp.where(is_query, q * (1.0 - key), key)
        else:
            groups = []
            for r0, r1 in _ref_rows(h, reverse):
                ref = bcast_row(r0)
                groups.append(ref if r1 == r0 else jnp.where(first_half, ref, bcast_row(r1)))
            dist = jnp.abs(b_inc - jnp.concatenate(groups, axis=0))
            x = jnp.where(is_query, q, key) * jnp.exp2(dist * neg_log2e)
        xb = x.astype(BF16)
        keep = mask_ref[li] > 0.5
        for hd in range(HGRN_HEADS):
            xh = xb[:, hd * HEAD_DIM:(hd + 1) * HEAD_DIM]
            p = lax.dot_general(xh, xh, (((1,), (1,)), ((), ())), preferred_element_type=F32)
            scores[hd] = jnp.where(keep, p, 0.0 if scores[hd] is None else scores[hd])

    b_total = bcast_row(0 if reverse else C - 1)
    b_rest = jnp.concatenate([b_total] * (C // SUBLANES), axis=0) - b_inc
    q_dec = (q * jnp.exp2(b_inc * -neg_log2e)).astype(BF16)
    k_dec = (key * jnp.exp2(b_rest * -neg_log2e)).astype(BF16)
    total = jnp.exp(b_total[0:1, :])
    qb, kb = q.astype(BF16), key.astype(BF16)
    on_diag = mask_ref[n_lv] > 0.5
    for hd in range(HGRN_HEADS):
        cols = slice(hd * HEAD_DIM, (hd + 1) * HEAD_DIM)
        diag = lax.dot_general(qb[:, cols], kb[:, cols], (((1,), (1,)), ((), ())),
                               preferred_element_type=F32)
        p = jnp.where(on_diag, diag, scores[hd]).astype(BF16)
        st = state_ref[hd]
        o = jnp.dot(p, vb[:, cols], preferred_element_type=F32)
        o = o + lax.dot_general(q_dec[:, cols], st.astype(BF16), (((1,), (1,)), ((), ())),
                                preferred_element_type=F32)
        o_ref[:, cols] = o
        upd = lax.dot_general(vb[:, cols], k_dec[:, cols], (((0,), (0,)), ((), ())),
                              preferred_element_type=F32)
        state_ref[hd] = st * total[:, cols] + upd


def _hgrn_scan_kernel(qf_ref, ff_ref, vf_ref, qb_ref, fb_ref, vb_ref, lbt_ref, tri_ref, up_ref,
                      mask_ref, of_ref, ob_ref, b_ref, state_ref, *, layer):
    @pl.when(pl.program_id(1) == 0)
    def _():
        state_ref[...] = jnp.zeros_like(state_ref)

    _scan_chunk(qf_ref, ff_ref, vf_ref, lbt_ref[0], tri_ref[0], up_ref.at[0], mask_ref.at[0],
                of_ref, b_ref.at[0], state_ref.at[0], layer=layer, reverse=False)
    _scan_chunk(qb_ref, fb_ref, vb_ref, lbt_ref[1], tri_ref[1], up_ref.at[1], mask_ref.at[1],
                ob_ref, b_ref.at[1], state_ref.at[1], layer=layer, reverse=True)


def _hgrn_scan(proj, lb_table, layer, batch, seq_len):
    T = proj.shape[0]
    C = SCAN_CHUNK
    n_chunks = seq_len // C
    tri, upper, mask = _scan_constants()
    col0 = (3 * D_HYENA) // D_HGRN

    fwd = lambda b, c: b * n_chunks + c
    bwd = lambda b, c: b * n_chunks + n_chunks - 1 - c
    chunk = lambda rows, col: pl.BlockSpec((C, D_HGRN), lambda b, c: (rows(b, c), col))
    whole = lambda a: pl.BlockSpec(a.shape, lambda b, c: (0,) * a.ndim)
    out = jax.ShapeDtypeStruct((T, D_HGRN), F32)
    return pl.pallas_call(
        functools.partial(_hgrn_scan_kernel, layer=layer),
        grid=(batch, n_chunks),
        in_specs=[
            chunk(fwd, col0), chunk(fwd, col0 + 1), chunk(fwd, col0 + 3),
            chunk(bwd, col0), chunk(bwd, col0 + 2), chunk(bwd, col0 + 3),
            whole(lb_table), whole(tri), whole(upper), whole(mask),
        ],
        out_specs=[chunk(fwd, 0), chunk(bwd, 0)],
        out_shape=[out, out],
        scratch_shapes=[pltpu.VMEM((2, C, D_HGRN), F32),
                        pltpu.VMEM((2, HGRN_HEADS, HEAD_DIM, HEAD_DIM), F32)],
        compiler_params=_params("parallel", "arbitrary"),
    )(proj, proj, proj, proj, proj, proj, lb_table, jnp.asarray(tri, BF16), jnp.asarray(upper),
      jnp.asarray(mask))


def _group_mean_matrix(group):
    idx = np.arange(D_HYENA) // group
    return (idx[:, None] == idx[None, :]).astype(np.float32)


def _mix_out_kernel(x_ref, y_ref, z_ref, x0_ref, of_ref, ob_ref, g_ref, skip_ref,
                    hy_gain_ref, hg_gain_ref, grp_hy_ref, grp_hg_ref, w_ref, post_ref, o_ref):
    z = z_ref[...]
    yh = x0_ref[...] * (y_ref[...] + skip_ref[...] * z)
    ms = _split_dot(yh * yh, grp_hy_ref[...]) * (HYENA_GROUPS / D_HYENA)
    yh = yh * lax.rsqrt(ms + EPS) * hy_gain_ref[...]
    o = of_ref[...] + ob_ref[...]
    ms = _split_dot(o * o, grp_hg_ref[...]) * (1.0 / HEAD_DIM)
    g = g_ref[...]
    o = o * lax.rsqrt(ms + EPS) * hg_gain_ref[...] * (g * (1.0 / (1.0 + jnp.exp(-g))))
    mix = (jnp.dot(yh.astype(BF16), w_ref[:D_HYENA, :], preferred_element_type=F32)
           + jnp.dot(o.astype(BF16), w_ref[D_HYENA:, :], preferred_element_type=F32))
    o_ref[...] = x_ref[...] + mix * _rms_scale(mix) * post_ref[...]


def _mix_out(x2d, y, z, x0, o_fwd, o_bwd, proj, skip, hy_gain, hg_gain, w_bf16, post_gain,
             tm=512):
    T = x2d.shape[0]
    gate_col = D_IN // D_HGRN - 1
    half = lambda: pl.BlockSpec((tm, D_HYENA), lambda i: (i, 0))
    vec = lambda n: pl.BlockSpec((1, n), lambda i: (0, 0))
    sq = lambda: pl.BlockSpec((D_HYENA, D_HYENA), lambda i: (0, 0))
    row = lambda v: v.reshape(1, -1)
    return pl.pallas_call(
        _mix_out_kernel,
        grid=(T // tm,),
        in_specs=[
            pl.BlockSpec((tm, D_MODEL), lambda i: (i, 0)),
            half(), half(), half(), half(), half(),
            pl.BlockSpec((tm, D_HGRN), lambda i: (i, gate_col)),
            vec(D_HYENA), vec(D_HYENA), vec(D_HGRN), sq(), sq(),
            pl.BlockSpec((D_MODEL, D_MODEL), lambda i: (0, 0)),
            vec(D_MODEL),
        ],
        out_specs=pl.BlockSpec((tm, D_MODEL), lambda i: (i, 0)),
        out_shape=jax.ShapeDtypeStruct((T, D_MODEL), F32),
        compiler_params=_params("parallel"),
    )(x2d, y, z, x0, o_fwd, o_bwd, proj, row(skip), row(hy_gain), row(hg_gain),
      jnp.asarray(_group_mean_matrix(D_HYENA // HYENA_GROUPS), BF16),
      jnp.asarray(_group_mean_matrix(HEAD_DIM), BF16), w_bf16, row(post_gain))


def _gelu_tanh(x):
    c = -2.0 * math.sqrt(2.0 / math.pi) * math.log2(math.e)
    return x / (1.0 + jnp.exp2(x * (c + (0.044715 * c) * (x * x))))


def _ffn_kernel(prev_ref, main_ref, next_ref, pre_ref, wa_ref, wb_ref, cwa_ref, cwb_ref,
                cba_ref, cbb_ref, wd_ref, post_ref, o_ref, h_ref, acc_ref, ua_ref, ub_ref,
                *, tiles_per_seq, tm):
    j = pl.program_id(1)
    pos = pl.program_id(0) % tiles_per_seq

    @pl.when(j == 0)
    def _():
        gain = pre_ref[...]

        def normed(x):
            return (x * _rms_scale(x) * gain).astype(BF16)

        prev = normed(prev_ref[...])
        nxt = normed(next_ref[...])
        h_ref[0:FFN_HALO, :] = jnp.where(pos == 0, jnp.zeros_like(prev), prev)
        h_ref[FFN_HALO:FFN_HALO + tm, :] = normed(main_ref[...])
        h_ref[FFN_HALO + tm:, :] = jnp.where(pos == tiles_per_seq - 1, jnp.zeros_like(nxt), nxt)
        acc_ref[...] = jnp.zeros_like(acc_ref)

    h = h_ref[...]

    def conv_part(w_ref_, cw_ref_, cb_ref_, u_ref):
        u_ref[...] = jnp.dot(h, w_ref_[...], preferred_element_type=F32)
        cw = cw_ref_[...]
        down = u_ref[pl.ds(FFN_HALO - 1, tm), :]
        mid = u_ref[pl.ds(FFN_HALO, tm), :]
        up = u_ref[pl.ds(FFN_HALO + 1, tm), :]
        return down * cw[0:1, :] + mid * cw[1:2, :] + up * cw[2:3, :] + cb_ref_[...]

    a = conv_part(wa_ref, cwa_ref, cba_ref, ua_ref)
    b = conv_part(wb_ref, cwb_ref, cbb_ref, ub_ref)
    act = (_gelu_tanh(a) * b).astype(BF16)
    acc_ref[...] += jnp.dot(act, wd_ref[...], preferred_element_type=F32)

    @pl.when(j == pl.num_programs(1) - 1)
    def _():
        ff = acc_ref[...]
        o_ref[...] = main_ref[...] + ff * _rms_scale(ff) * post_ref[...]


def _ffn(x2d, pre_gain, w_up_bf16, conv_w, conv_b, w_down_bf16, post_gain, seq_len,
         tm=1024, tf=512):
    T = x2d.shape[0]
    halo_per_tile = tm // FFN_HALO
    last_halo = T // FFN_HALO - 1
    n_f = D_FF // tf
    row = lambda v: v.reshape(1, -1)
    return pl.pallas_call(
        functools.partial(_ffn_kernel, tiles_per_seq=seq_len // tm, tm=tm),
        grid=(T // tm, n_f),
        in_specs=[
            pl.BlockSpec((FFN_HALO, D_MODEL),
                         lambda i, j: (jnp.maximum(i * halo_per_tile - 1, 0), 0)),
            pl.BlockSpec((tm, D_MODEL), lambda i, j: (i, 0)),
            pl.BlockSpec((FFN_HALO, D_MODEL),
                         lambda i, j: (jnp.minimum((i + 1) * halo_per_tile, last_halo), 0)),
            pl.BlockSpec((1, D_MODEL), lambda i, j: (0, 0)),
            pl.BlockSpec((D_MODEL, tf), lambda i, j: (0, j)),
            pl.BlockSpec((D_MODEL, tf), lambda i, j: (0, n_f + j)),
            pl.BlockSpec((3, tf), lambda i, j: (0, j)),
            pl.BlockSpec((3, tf), lambda i, j: (0, n_f + j)),
            pl.BlockSpec((1, tf), lambda i, j: (0, j)),
            pl.BlockSpec((1, tf), lambda i, j: (0, n_f + j)),
            pl.BlockSpec((tf, D_MODEL), lambda i, j: (j, 0)),
            pl.BlockSpec((1, D_MODEL), lambda i, j: (0, 0)),
        ],
        out_specs=pl.BlockSpec((tm, D_MODEL), lambda i, j: (i, 0)),
        out_shape=jax.ShapeDtypeStruct((T, D_MODEL), F32),
        scratch_shapes=[pltpu.VMEM((tm + 2 * FFN_HALO, D_MODEL), BF16),
                        pltpu.VMEM((tm, D_MODEL), F32)]
        + [pltpu.VMEM((tm + 2 * FFN_HALO, tf), F32)] * 2,
        compiler_params=_params("parallel", "arbitrary"),
    )(x2d, x2d, x2d, row(pre_gain), w_up_bf16, w_up_bf16, conv_w, conv_w,
      row(conv_b), row(conv_b), w_down_bf16, row(post_gain))


def _trunk(x, p):
    B, L, _ = x.shape
    x2d = x.reshape(B * L, D_MODEL)
    plan = _FftPlan(L)
    for l in range(p["w_in"].shape[0]):
        proj = _in_proj(x2d, p["norm_mix_pre"][l], p["w_in_bf16"][l])
        x0, z = _hyena_pre(proj, p["hyena_conv_w"][l], p["hyena_conv_b"][l], L)
        kern = _hyena_filter(L, p["filt_w1"][l], p["filt_b1"][l], p["filt_w2"][l], p["filt_b2"][l],
                             p["filt_w3"][l], p["filt_b3"][l], p["filt_w4"][l], p["filt_freq"][l])
        y = _fft_long_conv(z.reshape(B, L, D_HYENA), kern, plan).reshape(B * L, D_HYENA)
        o_fwd, o_bwd = _hgrn_scan(proj, p["hgrn_lower_bounds"], l, B, L)
        x2d = _mix_out(x2d, y, z, x0, o_fwd, o_bwd, proj, p["hyena_skip"][l], p["hyena_out_norm"][l],
                       p["hgrn_out_norm"][l], p["w_out_bf16"][l], p["norm_mix_post"][l])
        x2d = _ffn(x2d, p["norm_ffn_pre"][l], p["ffn_w_up_bf16"][l], p["ffn_conv_w"][l],
                   p["ffn_conv_b"][l], p["ffn_w_down_bf16"][l], p["norm_ffn_post"][l], L)
    return x2d.reshape(B, L, D_MODEL)


def kernel(x_prompt, x_sample, norm_mix_pre, norm_mix_post, norm_ffn_pre, norm_ffn_post, w_in, hyena_conv_w, hyena_conv_b, filt_w1, filt_b1, filt_w2, filt_b2, filt_w3, filt_b3, filt_w4, filt_freq, hyena_skip, hyena_out_norm, hgrn_lower_bounds, hgrn_out_norm, w_out, ffn_w_up, ffn_conv_w, ffn_conv_b, ffn_w_down):
    p = dict(
        norm_mix_pre=norm_mix_pre, norm_mix_post=norm_mix_post, norm_ffn_pre=norm_ffn_pre,
        norm_ffn_post=norm_ffn_post, w_in=w_in, hyena_conv_w=hyena_conv_w,
        hyena_conv_b=hyena_conv_b, filt_w1=filt_w1, filt_b1=filt_b1, filt_w2=filt_w2,
        filt_b2=filt_b2, filt_w3=filt_w3, filt_b3=filt_b3, filt_w4=filt_w4, filt_freq=filt_freq,
        hyena_skip=hyena_skip, hyena_out_norm=hyena_out_norm,
        hgrn_lower_bounds=hgrn_lower_bounds, hgrn_out_norm=hgrn_out_norm,
        ffn_conv_w=ffn_conv_w, ffn_conv_b=ffn_conv_b,
        w_in_bf16=w_in.astype(BF16), w_out_bf16=w_out.astype(BF16),
        ffn_w_up_bf16=ffn_w_up.astype(BF16), ffn_w_down_bf16=ffn_w_down.astype(BF16),
    )
    return (_trunk(x_prompt, p), _trunk(x_sample, p))
```

```python
import functools
import math

import numpy as np
import jax
import jax.numpy as jnp
from jax import lax
from jax.experimental import pallas as pl
from jax.experimental.pallas import tpu as pltpu

F32 = jnp.float32
BF16 = jnp.bfloat16
HIGHEST = lax.Precision.HIGHEST

D_MODEL = 1024
D_HYENA = 512
HYENA_GROUPS = 8
D_HGRN = 512
HGRN_HEADS = 4
HEAD_DIM = 128
D_IN = 3 * D_HYENA + 5 * D_HGRN
D_FF = 4 * D_MODEL
FILTER_EMB = 33
FILTER_BANDS = 16
FILTER_HIDDEN = 64
DECAY_TARGET = 1e-2
FAST_DECAY_PCT = 0.3
SLOW_DECAY_PCT = 1.5
EPS = 1e-6

SUBLANES = 8
LANES = 128
VMEM_LIMIT_BYTES = 56 * 1024 * 1024

FFT_FAST = 128
FFT_KS_GROUP = 8
FFT_F_TILE = 16
FFN_HALO = 16
SCAN_CHUNK = 128
SCAN_CHUNKS_PER_STEP = 2
SCAN_LEVELS = (64, 32, 16, 8, 4, 2, 1)


def _params(*semantics):
    return pltpu.CompilerParams(dimension_semantics=semantics,
                                vmem_limit_bytes=VMEM_LIMIT_BYTES)


def _rms_scale(x):
    return lax.rsqrt(jnp.mean(x * x, axis=-1, keepdims=True) + EPS)


def _group_sums(sq, same_group):
    return jnp.dot(sq.astype(BF16), same_group, preferred_element_type=F32)


def _in_proj_kernel(x_ref, g_ref, w_ref, o_ref, h_ref):
    @pl.when(pl.program_id(1) == 0)
    def _():
        x = x_ref[...]
        h_ref[...] = (x * _rms_scale(x) * g_ref[...]).astype(BF16)

    o_ref[...] = jnp.dot(h_ref[...], w_ref[...], preferred_element_type=F32)


def _in_proj(x2d, gain, w_bf16, tm=1024, tn=2048):
    T = x2d.shape[0]
    N = w_bf16.shape[1]
    return pl.pallas_call(
        _in_proj_kernel,
        grid=(T // tm, N // tn),
        in_specs=[
            pl.BlockSpec((tm, D_MODEL), lambda i, j: (i, 0)),
            pl.BlockSpec((1, D_MODEL), lambda i, j: (0, 0)),
            pl.BlockSpec((D_MODEL, tn), lambda i, j: (0, j)),
        ],
        out_specs=pl.BlockSpec((tm, tn), lambda i, j: (i, j)),
        out_shape=jax.ShapeDtypeStruct((T, N), F32),
        scratch_shapes=[pltpu.VMEM((tm, D_MODEL), BF16)],
        compiler_params=_params("parallel", "arbitrary"),
    )(x2d, gain.reshape(1, D_MODEL), w_bf16)


def _shift_rows(main, prev_row, next_row):
    tm = main.shape[0]
    rows = lax.broadcasted_iota(jnp.int32, (tm, 1), 0)
    down = jnp.where(rows == 0, prev_row, pltpu.roll(main, 1, 0))
    up = jnp.where(rows == tm - 1, next_row, pltpu.roll(main, tm - 1, 0))
    return down, up


def _hyena_pre_kernel(prev_ref, main_ref, next_ref, w_ref, b_ref, x0_ref, z_ref,
                      *, tiles_per_seq):
    pos = pl.program_id(0) % tiles_per_seq
    main = main_ref[...]
    prev_row = jnp.where(pos == 0, 0.0, prev_ref[SUBLANES - 1:SUBLANES, :])
    next_row = jnp.where(pos == tiles_per_seq - 1, 0.0, next_ref[0:1, :])
    down, up = _shift_rows(main, prev_row, next_row)
    w = w_ref[...]
    u = down * w[0:1, :] + main * w[1:2, :] + up * w[2:3, :] + b_ref[...]
    x0_ref[...] = u[:, :D_HYENA]
    z_ref[...] = u[:, D_HYENA:2 * D_HYENA] * u[:, 2 * D_HYENA:]


def _hyena_pre(proj, conv_w, conv_b, seq_len, tm=512):
    T = proj.shape[0]
    width = 3 * D_HYENA
    halo_per_tile = tm // SUBLANES
    last_halo = T // SUBLANES - 1
    out = jax.ShapeDtypeStruct((T, D_HYENA), F32)
    return pl.pallas_call(
        functools.partial(_hyena_pre_kernel, tiles_per_seq=seq_len // tm),
        grid=(T // tm,),
        in_specs=[
            pl.BlockSpec((SUBLANES, width),
                         lambda i: (jnp.maximum(i * halo_per_tile - 1, 0), 0)),
            pl.BlockSpec((tm, width), lambda i: (i, 0)),
            pl.BlockSpec((SUBLANES, width),
                         lambda i: (jnp.minimum((i + 1) * halo_per_tile, last_halo), 0)),
            pl.BlockSpec((3, width), lambda i: (0, 0)),
            pl.BlockSpec((1, width), lambda i: (0, 0)),
        ],
        out_specs=[pl.BlockSpec((tm, D_HYENA), lambda i: (i, 0)),
                   pl.BlockSpec((tm, D_HYENA), lambda i: (i, 0))],
        out_shape=[out, out],
        compiler_params=_params("parallel"),
    )(proj, proj, proj, conv_w, conv_b.reshape(1, width))


def _filter_kernel(band_ref, w1t_ref, w1c_ref, w1s_ref, b1_ref, w2_ref, b2_ref, w3_ref, b3_ref,
                   w4_ref, freq_ref, delta_ref, o_ref, *, seq_len, tr):
    row0 = pl.program_id(0) * tr

    def tap_index(shape, axis):
        j = row0 + lax.broadcasted_iota(jnp.int32, shape, axis)
        return j, jnp.where(j < seq_len, j, 2 * seq_len - j).astype(F32)

    _, idx = tap_index((1, tr), 1)
    t = idx * (1.0 / (seq_len - 1))
    arg = band_ref[...] * ((2.0 * math.pi / seq_len) * idx)
    fr = freq_ref[...]

    def dense(w_ref_, a):
        return jnp.dot(w_ref_[...], a, precision=HIGHEST, preferred_element_type=F32)

    h = w1t_ref[...] * t + dense(w1c_ref, jnp.cos(arg)) - dense(w1s_ref, jnp.sin(arg))
    h = jnp.sin(fr * (h + b1_ref[...]))
    h = jnp.sin(fr * (dense(w2_ref, h) + b2_ref[...]))
    h = jnp.sin(fr * (dense(w3_ref, h) + b3_ref[...]))
    out = lax.dot_general(h.astype(BF16), w4_ref[...].astype(BF16), (((0,), (0,)), ((), ())),
                          preferred_element_type=F32)
    j_col, idx_col = tap_index((tr, 1), 0)
    window = jnp.exp(-(idx_col * (1.0 / (seq_len - 1))) * delta_ref[...])
    o_ref[...] = jnp.where(j_col == seq_len, 0.0, out * window)


def _hyena_filter(seq_len, w1, b1, w2, b2, w3, b3, w4, freq, tr=1024):
    n_taps = 2 * seq_len
    bands = np.linspace(1e-4, FILTER_BANDS - 1, FILTER_BANDS, dtype=np.float32).reshape(-1, 1)
    deltas = np.abs(np.linspace(math.log(DECAY_TARGET) / SLOW_DECAY_PCT,
                                math.log(DECAY_TARGET) / FAST_DECAY_PCT, D_HYENA,
                                dtype=np.float32)).reshape(1, D_HYENA)
    tiles_fwd = seq_len // tr
    const = lambda i: (0, 0)
    col = lambda v: v.reshape(-1, 1)
    full = lambda a: pl.BlockSpec(a.shape, const)
    operands = [jnp.asarray(bands), col(w1[0]), w1[1:1 + FILTER_BANDS].T,
                w1[1 + FILTER_BANDS:].T, col(b1), w2.T, col(b2), w3.T, col(b3)]
    return pl.pallas_call(
        functools.partial(_filter_kernel, seq_len=seq_len, tr=tr),
        grid=(n_taps // tr,),
        in_specs=[full(a) for a in operands] + [
            pl.BlockSpec((FILTER_HIDDEN, D_HYENA), lambda i: (0, i // tiles_fwd)),
            pl.BlockSpec((FILTER_HIDDEN, 1), const),
            pl.BlockSpec((1, D_HYENA), const),
        ],
        out_specs=pl.BlockSpec((tr, D_HYENA), lambda i: (i, 0)),
        out_shape=jax.ShapeDtypeStruct((n_taps, D_HYENA), F32),
        compiler_params=_params("parallel"),
    )(*operands, w4, col(freq), jnp.asarray(deltas))


class _FftPlan:
    def __init__(self, seq_len):
        self.n = 2 * seq_len
        self.fast = FFT_FAST
        self.slow = self.n // FFT_FAST
        self.slow_half = self.slow // 2
        self.ks = self.slow_half + 1
        self.ks_pad = -(-self.ks // FFT_KS_GROUP) * FFT_KS_GROUP
        S, F, N = self.slow, self.fast, self.n
        ks = np.arange(self.ks_pad, dtype=np.float64)[:, None]
        valid = (ks < self.ks)

        def stage_a(n_s):
            s = np.arange(n_s, dtype=np.float64)[None, :]
            ang = 2.0 * np.pi * ks * s / S
            return np.concatenate([np.where(valid, np.cos(ang), 0.0),
                                   np.where(valid, -np.sin(ang), 0.0)], axis=0)

        self.a_half = stage_a(self.slow_half).astype(np.float32)
        self.a_full = stage_a(self.slow).astype(np.float32)
        s = np.arange(self.slow_half, dtype=np.float64)[:, None]
        kk = np.arange(self.ks_pad, dtype=np.float64)[None, :]
        weight = np.where((kk == 0) | (kk == self.slow_half), 1.0, 2.0) * (kk < self.ks) / N
        ang = 2.0 * np.pi * s * kk / S
        self.a_inv = np.concatenate([weight * np.cos(ang), -weight * np.sin(ang)],
                                    axis=1).astype(np.float32)
        f = np.arange(F, dtype=np.float64)
        ang = 2.0 * np.pi * np.outer(np.arange(self.ks_pad, dtype=np.float64), f) / N
        self.tw_re = np.cos(ang).reshape(-1, 1).astype(np.float32)
        self.tw_im = (-np.sin(ang)).reshape(-1, 1).astype(np.float32)
        ang = 2.0 * np.pi * np.outer(f, f) / F
        self.f_re = np.cos(ang).astype(np.float32)
        self.f_im = (-np.sin(ang)).astype(np.float32)


def _dft_dot(a_ref, x):
    return jnp.dot(a_ref[...], x.astype(BF16), preferred_element_type=F32)


def _stage_a_kernel(a_ref, x_ref, o_ref, *, batch):
    m, k = a_ref.shape
    x2d = x_ref.reshape(batch * k * FFT_F_TILE, LANES)
    o2d = o_ref.reshape(batch * m * FFT_F_TILE, LANES)
    for b in range(batch):
        for j in range(FFT_F_TILE):
            x = x2d[pl.ds(b * k * FFT_F_TILE + j, k, stride=FFT_F_TILE), :]
            o2d[pl.ds(b * m * FFT_F_TILE + j, m, stride=FFT_F_TILE), :] = _dft_dot(a_ref, x)


def _stage_a(a, x, inverse):
    F, C = x.shape[-2:]
    B = x.shape[0]
    M, K = a.shape
    x_dims, o_dims = ((2, K // 2), (M,)) if inverse else ((K,), (2, M // 2))

    def spec(dims):
        zeros = (0,) * len(dims)
        return pl.BlockSpec((B,) + dims + (FFT_F_TILE, LANES), lambda f, c: (0,) + zeros + (f, c))

    return pl.pallas_call(
        functools.partial(_stage_a_kernel, batch=B),
        grid=(F // FFT_F_TILE, C // LANES),
        in_specs=[pl.BlockSpec((M, K), lambda f, c: (0, 0)), spec(x_dims)],
        out_specs=spec(o_dims),
        out_shape=jax.ShapeDtypeStruct((B,) + o_dims + (F, C), F32),
        compiler_params=_params("parallel", "parallel"),
    )(jnp.asarray(a, BF16), x)


def _stage_b_tile(fr_ref, fi_ref, re, im, conj=False):
    both = jnp.concatenate([re, im], axis=-1)
    pr = _dft_dot(fr_ref, both)
    pi = _dft_dot(fi_ref, both)
    w = re.shape[-1]
    if conj:
        return pr[:, :w] + pi[:, w:], pr[:, w:] - pi[:, :w]
    return pr[:, :w] - pi[:, w:], pr[:, w:] + pi[:, :w]


def _fft_fwd_kernel(a_ref, twr_ref, twi_ref, fr_ref, fi_ref, o_ref):
    for g in range(FFT_KS_GROUP):
        rows = slice(g * FFT_FAST, (g + 1) * FFT_FAST)
        ar, ai = a_ref[0, 0, rows, :], a_ref[0, 1, rows, :]
        tr, ti = twr_ref[rows, :], twi_ref[rows, :]
        xr, xi = _stage_b_tile(fr_ref, fi_ref, ar * tr - ai * ti, ar * ti + ai * tr)
        o_ref[0, 0, rows, :] = xr
        o_ref[0, 1, rows, :] = xi


def _fft_conv_kernel(a_ref, k_ref, twr_ref, twi_ref, fr_ref, fi_ref, o_ref):
    for g in range(FFT_KS_GROUP):
        rows = slice(g * FFT_FAST, (g + 1) * FFT_FAST)
        ar, ai = a_ref[0, 0, rows, :], a_ref[0, 1, rows, :]
        tr, ti = twr_ref[rows, :], twi_ref[rows, :]
        xr, xi = _stage_b_tile(fr_ref, fi_ref, ar * tr - ai * ti, ar * ti + ai * tr)
        kr, ki = k_ref[0, 0, rows, :], k_ref[0, 1, rows, :]
        yr, yi = xr * kr - xi * ki, xr * ki + xi * kr
        br, bi = _stage_b_tile(fr_ref, fi_ref, yr, yi, conj=True)
        o_ref[0, 0, rows, :] = br * tr + bi * ti
        o_ref[0, 1, rows, :] = bi * tr - br * ti


def _fft_stage_b(plan, a, k=None):
    B, _, R, C = a.shape
    rows = FFT_KS_GROUP * FFT_FAST
    data = pl.BlockSpec((1, 2, rows, LANES), lambda r, c, b: (b, 0, r, c))
    tw = pl.BlockSpec((rows, LANES), lambda r, c, b: (r, 0))
    mat = pl.BlockSpec((FFT_FAST, FFT_FAST), lambda r, c, b: (0, 0))
    lane_bcast = lambda col: jnp.broadcast_to(jnp.asarray(col), (col.shape[0], LANES))
    consts = (lane_bcast(plan.tw_re), lane_bcast(plan.tw_im),
              jnp.asarray(plan.f_re, BF16), jnp.asarray(plan.f_im, BF16))
    common = dict(
        grid=(R // rows, C // LANES, B),
        out_specs=data,
        out_shape=jax.ShapeDtypeStruct(a.shape, F32),
        compiler_params=_params("parallel", "parallel", "arbitrary"),
    )
    if k is None:
        return pl.pallas_call(_fft_fwd_kernel, in_specs=[data, tw, tw, mat, mat],
                              **common)(a, *consts)
    kspec = pl.BlockSpec((1, 2, rows, LANES), lambda r, c, b: (0, 0, r, c))
    return pl.pallas_call(_fft_conv_kernel, in_specs=[data, kspec, tw, tw, mat, mat],
                          **common)(a, k, *consts)


def _fft_long_conv(z, kern, plan):
    B, L, C = z.shape
    S, F, P = plan.slow, plan.fast, plan.ks_pad
    ka = _stage_a(plan.a_full, kern.reshape(1, S, F, C), inverse=False)
    k_spec = _fft_stage_b(plan, ka.reshape(1, 2, P * F, C))
    za = _stage_a(plan.a_half, z.reshape(B, S // 2, F, C), inverse=False)
    ya = _fft_stage_b(plan, za.reshape(B, 2, P * F, C), k_spec)
    y = _stage_a(plan.a_inv, ya.reshape(B, 2, P, F, C), inverse=True)
    return y.reshape(B, L, C)


def _scan_constants():
    C = SCAN_CHUNK
    n_lv = len(SCAN_LEVELS)
    tri = np.tril(np.ones((C, C), np.float32))
    upper = np.zeros((n_lv, C, 1), np.float32)
    mask = np.zeros((n_lv + 1, C, C), np.float32)
    for li, h in enumerate(SCAN_LEVELS):
        for t in range(C):
            r = (t // (2 * h)) * 2 * h + h
            if t % (2 * h) >= h:
                upper[li, t, 0] = 1.0
                mask[li, t, r - h:r] = 1.0
    mask[n_lv] = np.eye(C, dtype=np.float32)
    flip = lambda m: m[:, ::-1, ::-1]
    return (np.stack([tri, tri[::-1, ::-1]]), np.stack([upper, upper[:, ::-1]]),
            np.stack([mask, flip(mask)]))


def _ref_rows(h, reverse):
    rows = []
    for g in range(SCAN_CHUNK // SUBLANES):
        pair = []
        for t in (g * SUBLANES, g * SUBLANES + SUBLANES // 2):
            start = (t // (2 * h)) * 2 * h
            pair.append(start + h if reverse else start + h - 1)
        rows.append(tuple(pair))
    return rows


def _scan_chunk(q_ref, f_ref, v_ref, lb_table, tri, up_ref, mask_ref, o_ref, b_ref, state_ref,
                *, layer, reverse):
    n_lv = len(SCAN_LEVELS)
    C = SCAN_CHUNK

    e = jnp.exp(lb_table - jnp.max(lb_table, axis=0, keepdims=True))
    prob = e / jnp.sum(e, axis=0, keepdims=True)
    lb = jnp.zeros((1, D_HGRN), F32)
    for l in range(1, layer + 1):
        lb = lb + prob[l:l + 1, :]

    q = q_ref[...]
    q = q / (1.0 + jnp.exp(-q))
    z = f_ref[...]
    w = jnp.exp(-jnp.abs(z))
    inv = 1.0 / (1.0 + w)
    log_sig = jnp.minimum(z, 0.0) + jnp.log(inv)
    key = (1.0 - lb) * (jnp.where(z >= 0.0, w, 1.0) * inv)
    log_a = jnp.log(lb)
    log_b = jnp.log1p(-lb) + log_sig
    gate = jnp.maximum(log_a, log_b) + jnp.log(1.0 + jnp.exp(-jnp.abs(log_a - log_b)))
    vb = v_ref[...].astype(BF16)

    gate2 = gate * math.log2(math.e)
    g1 = gate2.astype(BF16)
    rem = gate2 - g1.astype(F32)
    g2 = rem.astype(BF16)
    g3 = (rem - g2.astype(F32)).astype(BF16)
    b_inc = (jnp.dot(tri, g1, preferred_element_type=F32)
             + jnp.dot(tri, g2, preferred_element_type=F32)
             + jnp.dot(tri, g3, preferred_element_type=F32))
    b_ref[...] = b_inc

    def bcast_row(r):
        return jnp.broadcast_to(b_ref[pl.ds(r, 1), :], (SUBLANES, D_HGRN))

    first_half = lax.broadcasted_iota(jnp.int32, (SUBLANES, 1), 0) < SUBLANES // 2
    scores = [None] * HGRN_HEADS
    for li, h in enumerate(SCAN_LEVELS):
        is_query = up_ref[li] > 0.5
        if h == 1:
            x = jnp.where(is_query, q * (1.0 - key), key)
        else:
            groups = []
            for r0, r1 in _ref_rows(h, reverse):
                ref = bcast_row(r0)
                groups.append(ref if r1 == r0 else jnp.where(first_half, ref, bcast_row(r1)))
            diff = pltpu.bitcast(b_inc - jnp.concatenate(groups, axis=0), jnp.int32)
            neg_dist = pltpu.bitcast(diff | jnp.int32(-2 ** 31), F32)
            x = jnp.where(is_query, q, key) * jnp.exp2(neg_dist)
        xb = x.astype(BF16)
        keep = mask_ref[li] > 0.5
        for hd in range(HGRN_HEADS):
            xh = xb[:, hd * HEAD_DIM:(hd + 1) * HEAD_DIM]
            p = lax.dot_general(xh, xh, (((1,), (1,)), ((), ())), preferred_element_type=F32)
            scores[hd] = jnp.where(keep, p, 0.0 if scores[hd] is None else scores[hd])

    b_total = bcast_row(0 if reverse else C - 1)
    b_rest = jnp.concatenate([b_total] * (C // SUBLANES), axis=0) - b_inc
    q_dec = (q * jnp.exp2(b_inc)).astype(BF16)
    k_dec = (key * jnp.exp2(b_rest)).astype(BF16)
    total = jnp.exp2(b_total[0:1, :])
    qb, kb = q.astype(BF16), key.astype(BF16)
    on_diag = mask_ref[n_lv] > 0.5
    for hd in range(HGRN_HEADS):
        cols = slice(hd * HEAD_DIM, (hd + 1) * HEAD_DIM)
        diag = lax.dot_general(qb[:, cols], kb[:, cols], (((1,), (1,)), ((), ())),
                               preferred_element_type=F32)
        p = jnp.where(on_diag, diag, scores[hd]).astype(BF16)
        st = state_ref[hd]
        o = jnp.dot(p, vb[:, cols], preferred_element_type=F32)
        o = o + lax.dot_general(q_dec[:, cols], st.astype(BF16), (((1,), (1,)), ((), ())),
                                preferred_element_type=F32)
        o_ref[:, cols] = o
        upd = lax.dot_general(vb[:, cols], k_dec[:, cols], (((0,), (0,)), ((), ())),
                              preferred_element_type=F32)
        state_ref[hd] = st * total[:, cols] + upd


def _hgrn_scan_kernel(qf_ref, ff_ref, vf_ref, qb_ref, fb_ref, vb_ref, lbt_ref, tri_ref, up_ref,
                      mask_ref, of_ref, ob_ref, b_ref, state_ref, *, layer):
    @pl.when(pl.program_id(1) == 0)
    def _():
        state_ref[...] = jnp.zeros_like(state_ref)

    for s in range(SCAN_CHUNKS_PER_STEP):
        rows = pl.ds(s * SCAN_CHUNK, SCAN_CHUNK)
        _scan_chunk(qf_ref.at[rows], ff_ref.at[rows], vf_ref.at[rows], lbt_ref[0], tri_ref[0],
                    up_ref.at[0], mask_ref.at[0], of_ref.at[rows], b_ref.at[0], state_ref.at[0],
                    layer=layer, reverse=False)
        rows = pl.ds((SCAN_CHUNKS_PER_STEP - 1 - s) * SCAN_CHUNK, SCAN_CHUNK)
        _scan_chunk(qb_ref.at[rows], fb_ref.at[rows], vb_ref.at[rows], lbt_ref[1], tri_ref[1],
                    up_ref.at[1], mask_ref.at[1], ob_ref.at[rows], b_ref.at[1], state_ref.at[1],
                    layer=layer, reverse=True)


def _hgrn_scan(proj, lb_table, layer, batch, seq_len):
    T = proj.shape[0]
    C = SCAN_CHUNK
    rows_per_step = SCAN_CHUNKS_PER_STEP * C
    n_chunks = seq_len // rows_per_step
    tri, upper, mask = _scan_constants()
    col0 = (3 * D_HYENA) // D_HGRN

    fwd = lambda b, c: b * n_chunks + c
    bwd = lambda b, c: b * n_chunks + n_chunks - 1 - c
    chunk = lambda rows, col: pl.BlockSpec((rows_per_step, D_HGRN), lambda b, c: (rows(b, c), col))
    whole = lambda a: pl.BlockSpec(a.shape, lambda b, c: (0,) * a.ndim)
    out = jax.ShapeDtypeStruct((T, D_HGRN), F32)
    return pl.pallas_call(
        functools.partial(_hgrn_scan_kernel, layer=layer),
        grid=(batch, n_chunks),
        in_specs=[
            chunk(fwd, col0), chunk(fwd, col0 + 1), chunk(fwd, col0 + 3),
            chunk(bwd, col0), chunk(bwd, col0 + 2), chunk(bwd, col0 + 3),
            whole(lb_table), whole(tri), whole(upper), whole(mask),
        ],
        out_specs=[chunk(fwd, 0), chunk(bwd, 0)],
        out_shape=[out, out],
        scratch_shapes=[pltpu.VMEM((2, C, D_HGRN), F32),
                        pltpu.VMEM((2, HGRN_HEADS, HEAD_DIM, HEAD_DIM), F32)],
        compiler_params=_params("parallel", "arbitrary"),
    )(proj, proj, proj, proj, proj, proj, lb_table, jnp.asarray(tri, BF16), jnp.asarray(upper),
      jnp.asarray(mask))


def _group_mean_matrix(group):
    idx = np.arange(D_HYENA) // group
    return (idx[:, None] == idx[None, :]).astype(np.float32)


def _mix_out_kernel(x_ref, y_ref, z_ref, x0_ref, of_ref, ob_ref, g_ref, skip_ref,
                    hy_gain_ref, hg_gain_ref, grp_hy_ref, grp_hg_ref, w_ref, post_ref, o_ref):
    z = z_ref[...]
    yh = x0_ref[...] * (y_ref[...] + skip_ref[...] * z)
    ms = _group_sums(yh * yh, grp_hy_ref[...]) * (HYENA_GROUPS / D_HYENA)
    yh = yh * lax.rsqrt(ms + EPS) * hy_gain_ref[...]
    o = of_ref[...] + ob_ref[...]
    ms = _group_sums(o * o, grp_hg_ref[...]) * (1.0 / HEAD_DIM)
    g = g_ref[...]
    o = o * lax.rsqrt(ms + EPS) * hg_gain_ref[...] * (g * (1.0 / (1.0 + jnp.exp(-g))))
    mix = (jnp.dot(yh.astype(BF16), w_ref[:D_HYENA, :], preferred_element_type=F32)
           + jnp.dot(o.astype(BF16), w_ref[D_HYENA:, :], preferred_element_type=F32))
    o_ref[...] = x_ref[...] + mix * _rms_scale(mix) * post_ref[...]


def _mix_out(x2d, y, z, x0, o_fwd, o_bwd, proj, skip, hy_gain, hg_gain, w_bf16, post_gain,
             tm=512):
    T = x2d.shape[0]
    gate_col = D_IN // D_HGRN - 1
    half = lambda: pl.BlockSpec((tm, D_HYENA), lambda i: (i, 0))
    vec = lambda n: pl.BlockSpec((1, n), lambda i: (0, 0))
    sq = lambda: pl.BlockSpec((D_HYENA, D_HYENA), lambda i: (0, 0))
    row = lambda v: v.reshape(1, -1)
    return pl.pallas_call(
        _mix_out_kernel,
        grid=(T // tm,),
        in_specs=[
            pl.BlockSpec((tm, D_MODEL), lambda i: (i, 0)),
            half(), half(), half(), half(), half(),
            pl.BlockSpec((tm, D_HGRN), lambda i: (i, gate_col)),
            vec(D_HYENA), vec(D_HYENA), vec(D_HGRN), sq(), sq(),
            pl.BlockSpec((D_MODEL, D_MODEL), lambda i: (0, 0)),
            vec(D_MODEL),
        ],
        out_specs=pl.BlockSpec((tm, D_MODEL), lambda i: (i, 0)),
        out_shape=jax.ShapeDtypeStruct((T, D_MODEL), F32),
        compiler_params=_params("parallel"),
    )(x2d, y, z, x0, o_fwd, o_bwd, proj, row(skip), row(hy_gain), row(hg_gain),
      jnp.asarray(_group_mean_matrix(D_HYENA // HYENA_GROUPS), BF16),
      jnp.asarray(_group_mean_matrix(HEAD_DIM), BF16), w_bf16, row(post_gain))


def _gelu_tanh(x):
    c = -2.0 * math.sqrt(2.0 / math.pi) * math.log2(math.e)
    return x / (1.0 + jnp.exp2(x * (c + (0.044715 * c) * (x * x))))


def _ffn_kernel(prev_ref, main_ref, next_ref, pre_ref, wa_ref, wb_ref, cwa_ref, cwb_ref,
                cba_ref, cbb_ref, wd_ref, post_ref, o_ref, h_ref, acc_ref, ua_ref, ub_ref,
                *, tiles_per_seq, tm):
    j = pl.program_id(1)
    pos = pl.program_id(0) % tiles_per_seq

    @pl.when(j == 0)
    def _():
        gain = pre_ref[...]

        def normed(x):
            return (x * _rms_scale(x) * gain).astype(BF16)

        prev = normed(prev_ref[...])
        nxt = normed(next_ref[...])
        h_ref[0:FFN_HALO, :] = jnp.where(pos == 0, jnp.zeros_like(prev), prev)
        h_ref[FFN_HALO:FFN_HALO + tm, :] = normed(main_ref[...])
        h_ref[FFN_HALO + tm:, :] = jnp.where(pos == tiles_per_seq - 1, jnp.zeros_like(nxt), nxt)
        acc_ref[...] = jnp.zeros_like(acc_ref)

    h = h_ref[...]

    def conv_part(w_ref_, cw_ref_, cb_ref_, u_ref):
        u_ref[...] = jnp.dot(h, w_ref_[...], preferred_element_type=F32)
        cw = cw_ref_[...]
        down = u_ref[pl.ds(FFN_HALO - 1, tm), :]
        mid = u_ref[pl.ds(FFN_HALO, tm), :]
        up = u_ref[pl.ds(FFN_HALO + 1, tm), :]
        return down * cw[0:1, :] + mid * cw[1:2, :] + up * cw[2:3, :] + cb_ref_[...]

    a = conv_part(wa_ref, cwa_ref, cba_ref, ua_ref)
    b = conv_part(wb_ref, cwb_ref, cbb_ref, ub_ref)
    act = (_gelu_tanh(a) * b).astype(BF16)
    acc_ref[...] += jnp.dot(act, wd_ref[...], preferred_element_type=F32)

    @pl.when(j == pl.num_programs(1) - 1)
    def _():
        ff = acc_ref[...]
        o_ref[...] = main_ref[...] + ff * _rms_scale(ff) * post_ref[...]


def _ffn(x2d, pre_gain, w_up_bf16, conv_w, conv_b, w_down_bf16, post_gain, seq_len,
         tm=1024, tf=512):
    T = x2d.shape[0]
    halo_per_tile = tm // FFN_HALO
    last_halo = T // FFN_HALO - 1
    n_f = D_FF // tf
    row = lambda v: v.reshape(1, -1)
    return pl.pallas_call(
        functools.partial(_ffn_kernel, tiles_per_seq=seq_len // tm, tm=tm),
        grid=(T // tm, n_f),
        in_specs=[
            pl.BlockSpec((FFN_HALO, D_MODEL),
                         lambda i, j: (jnp.maximum(i * halo_per_tile - 1, 0), 0)),
            pl.BlockSpec((tm, D_MODEL), lambda i, j: (i, 0)),
            pl.BlockSpec((FFN_HALO, D_MODEL),
                         lambda i, j: (jnp.minimum((i + 1) * halo_per_tile, last_halo), 0)),
            pl.BlockSpec((1, D_MODEL), lambda i, j: (0, 0)),
            pl.BlockSpec((D_MODEL, tf), lambda i, j: (0, j)),
            pl.BlockSpec((D_MODEL, tf), lambda i, j: (0, n_f + j)),
            pl.BlockSpec((3, tf), lambda i, j: (0, j)),
            pl.BlockSpec((3, tf), lambda i, j: (0, n_f + j)),
            pl.BlockSpec((1, tf), lambda i, j: (0, j)),
            pl.BlockSpec((1, tf), lambda i, j: (0, n_f + j)),
            pl.BlockSpec((tf, D_MODEL), lambda i, j: (j, 0)),
            pl.BlockSpec((1, D_MODEL), lambda i, j: (0, 0)),
        ],
        out_specs=pl.BlockSpec((tm, D_MODEL), lambda i, j: (i, 0)),
        out_shape=jax.ShapeDtypeStruct((T, D_MODEL), F32),
        scratch_shapes=[pltpu.VMEM((tm + 2 * FFN_HALO, D_MODEL), BF16),
                        pltpu.VMEM((tm, D_MODEL), F32)]
        + [pltpu.VMEM((tm + 2 * FFN_HALO, tf), F32)] * 2,
        compiler_params=_params("parallel", "arbitrary"),
    )(x2d, x2d, x2d, row(pre_gain), w_up_bf16, w_up_bf16, conv_w, conv_w,
      row(conv_b), row(conv_b), w_down_bf16, row(post_gain))


def _trunk(x, p):
    B, L, _ = x.shape
    x2d = x.reshape(B * L, D_MODEL)
    plan = _FftPlan(L)
    for l in range(p["w_in"].shape[0]):
        proj = _in_proj(x2d, p["norm_mix_pre"][l], p["w_in_bf16"][l])
        x0, z = _hyena_pre(proj, p["hyena_conv_w"][l], p["hyena_conv_b"][l], L)
        kern = _hyena_filter(L, p["filt_w1"][l], p["filt_b1"][l], p["filt_w2"][l], p["filt_b2"][l],
                             p["filt_w3"][l], p["filt_b3"][l], p["filt_w4"][l], p["filt_freq"][l])
        y = _fft_long_conv(z.reshape(B, L, D_HYENA), kern, plan).reshape(B * L, D_HYENA)
        o_fwd, o_bwd = _hgrn_scan(proj, p["hgrn_lower_bounds"], l, B, L)
        x2d = _mix_out(x2d, y, z, x0, o_fwd, o_bwd, proj, p["hyena_skip"][l], p["hyena_out_norm"][l],
                       p["hgrn_out_norm"][l], p["w_out_bf16"][l], p["norm_mix_post"][l])
        x2d = _ffn(x2d, p["norm_ffn_pre"][l], p["ffn_w_up_bf16"][l], p["ffn_conv_w"][l],
                   p["ffn_conv_b"][l], p["ffn_w_down_bf16"][l], p["norm_ffn_post"][l], L)
    return x2d.reshape(B, L, D_MODEL)


def kernel(x_prompt, x_sample, norm_mix_pre, norm_mix_post, norm_ffn_pre, norm_ffn_post, w_in, hyena_conv_w, hyena_conv_b, filt_w1, filt_b1, filt_w2, filt_b2, filt_w3, filt_b3, filt_w4, filt_freq, hyena_skip, hyena_out_norm, hgrn_lower_bounds, hgrn_out_norm, w_out, ffn_w_up, ffn_conv_w, ffn_conv_b, ffn_w_down):
    p = dict(
        norm_mix_pre=norm_mix_pre, norm_mix_post=norm_mix_post, norm_ffn_pre=norm_ffn_pre,
        norm_ffn_post=norm_ffn_post, w_in=w_in, hyena_conv_w=hyena_conv_w,
        hyena_conv_b=hyena_conv_b, filt_w1=filt_w1, filt_b1=filt_b1, filt_w2=filt_w2,
        filt_b2=filt_b2, filt_w3=filt_w3, filt_b3=filt_b3, filt_w4=filt_w4, filt_freq=filt_freq,
        hyena_skip=hyena_skip, hyena_out_norm=hyena_out_norm,
        hgrn_lower_bounds=hgrn_lower_bounds, hgrn_out_norm=hgrn_out_norm,
        ffn_conv_w=ffn_conv_w, ffn_conv_b=ffn_conv_b,
        w_in_bf16=w_in.astype(BF16), w_out_bf16=w_out.astype(BF16),
        ffn_w_up_bf16=ffn_w_up.astype(BF16), ffn_w_down_bf16=ffn_w_down.astype(BF16),
    )
    return (_trunk(x_prompt, p), _trunk(x_sample, p))
```

```python
import functools
import math

import numpy as np
import jax
import jax.numpy as jnp
from jax import lax
from jax.experimental import pallas as pl
from jax.experimental.pallas import tpu as pltpu

F32 = jnp.float32
BF16 = jnp.bfloat16
HIGHEST = lax.Precision.HIGHEST

D_MODEL = 1024
D_HYENA = 512
HYENA_GROUPS = 8
D_HGRN = 512
HGRN_HEADS = 4
HEAD_DIM = 128
D_IN = 3 * D_HYENA + 5 * D_HGRN
D_FF = 4 * D_MODEL
FILTER_EMB = 33
FILTER_BANDS = 16
FILTER_HIDDEN = 64
DECAY_TARGET = 1e-2
FAST_DECAY_PCT = 0.3
SLOW_DECAY_PCT = 1.5
EPS = 1e-6

SUBLANES = 8
LANES = 128
VMEM_LIMIT_BYTES = 56 * 1024 * 1024

FFT_FAST = 128
FFT_KS_GROUP_MAX = 20
FFT_F_TILE = 16
FFN_HALO = 16
SCAN_CHUNK = 128
SCAN_CHUNKS_PER_STEP = 4
SCAN_LEVELS = (64, 32, 16, 8, 4, 2, 1)


def _params(*semantics):
    return pltpu.CompilerParams(dimension_semantics=semantics,
                                vmem_limit_bytes=VMEM_LIMIT_BYTES)


def _rms_scale(x):
    return lax.rsqrt(jnp.mean(x * x, axis=-1, keepdims=True) + EPS)


def _group_sums(sq, same_group):
    return jnp.dot(sq.astype(BF16), same_group, preferred_element_type=F32)


def _in_proj_kernel(x_ref, g_ref, w_ref, o_ref, h_ref):
    @pl.when(pl.program_id(1) == 0)
    def _():
        x = x_ref[...]
        h_ref[...] = (x * _rms_scale(x) * g_ref[...]).astype(BF16)

    o_ref[...] = jnp.dot(h_ref[...], w_ref[...], preferred_element_type=F32)


def _in_proj(x2d, gain, w_bf16, tm=1024, tn=2048):
    T = x2d.shape[0]
    N = w_bf16.shape[1]
    return pl.pallas_call(
        _in_proj_kernel,
        grid=(T // tm, N // tn),
        in_specs=[
            pl.BlockSpec((tm, D_MODEL), lambda i, j: (i, 0)),
            pl.BlockSpec((1, D_MODEL), lambda i, j: (0, 0)),
            pl.BlockSpec((D_MODEL, tn), lambda i, j: (0, j)),
        ],
        out_specs=pl.BlockSpec((tm, tn), lambda i, j: (i, j)),
        out_shape=jax.ShapeDtypeStruct((T, N), F32),
        scratch_shapes=[pltpu.VMEM((tm, D_MODEL), BF16)],
        compiler_params=_params("parallel", "arbitrary"),
    )(x2d, gain.reshape(1, D_MODEL), w_bf16)


def _shift_rows(main, prev_row, next_row):
    tm = main.shape[0]
    rows = lax.broadcasted_iota(jnp.int32, (tm, 1), 0)
    down = jnp.where(rows == 0, prev_row, pltpu.roll(main, 1, 0))
    up = jnp.where(rows == tm - 1, next_row, pltpu.roll(main, tm - 1, 0))
    return down, up


def _hyena_pre_kernel(prev_ref, main_ref, next_ref, w_ref, b_ref, x0_ref, z_ref,
                      *, tiles_per_seq):
    pos = pl.program_id(0) % tiles_per_seq
    main = main_ref[...]
    prev_row = jnp.where(pos == 0, 0.0, prev_ref[SUBLANES - 1:SUBLANES, :])
    next_row = jnp.where(pos == tiles_per_seq - 1, 0.0, next_ref[0:1, :])
    down, up = _shift_rows(main, prev_row, next_row)
    w = w_ref[...]
    u = down * w[0:1, :] + main * w[1:2, :] + up * w[2:3, :] + b_ref[...]
    x0_ref[...] = u[:, :D_HYENA]
    z_ref[...] = u[:, D_HYENA:2 * D_HYENA] * u[:, 2 * D_HYENA:]


def _hyena_pre(proj, conv_w, conv_b, seq_len, tm=512):
    T = proj.shape[0]
    width = 3 * D_HYENA
    halo_per_tile = tm // SUBLANES
    last_halo = T // SUBLANES - 1
    out = jax.ShapeDtypeStruct((T, D_HYENA), F32)
    return pl.pallas_call(
        functools.partial(_hyena_pre_kernel, tiles_per_seq=seq_len // tm),
        grid=(T // tm,),
        in_specs=[
            pl.BlockSpec((SUBLANES, width),
                         lambda i: (jnp.maximum(i * halo_per_tile - 1, 0), 0)),
            pl.BlockSpec((tm, width), lambda i: (i, 0)),
            pl.BlockSpec((SUBLANES, width),
                         lambda i: (jnp.minimum((i + 1) * halo_per_tile, last_halo), 0)),
            pl.BlockSpec((3, width), lambda i: (0, 0)),
            pl.BlockSpec((1, width), lambda i: (0, 0)),
        ],
        out_specs=[pl.BlockSpec((tm, D_HYENA), lambda i: (i, 0)),
                   pl.BlockSpec((tm, D_HYENA), lambda i: (i, 0))],
        out_shape=[out, out],
        compiler_params=_params("parallel"),
    )(proj, proj, proj, conv_w, conv_b.reshape(1, width))


def _filter_kernel(band_ref, w1t_ref, w1c_ref, w1s_ref, b1_ref, w2_ref, b2_ref, w3_ref, b3_ref,
                   w4_ref, freq_ref, delta_ref, o_ref, *, seq_len, tr):
    row0 = pl.program_id(0) * tr

    def tap_index(shape, axis):
        j = row0 + lax.broadcasted_iota(jnp.int32, shape, axis)
        return j, jnp.where(j < seq_len, j, 2 * seq_len - j).astype(F32)

    _, idx = tap_index((1, tr), 1)
    t = idx * (1.0 / (seq_len - 1))
    arg = band_ref[...] * ((2.0 * math.pi / seq_len) * idx)
    fr = freq_ref[...]

    def dense(w_ref_, a):
        return jnp.dot(w_ref_[...], a, precision=HIGHEST, preferred_element_type=F32)

    h = w1t_ref[...] * t + dense(w1c_ref, jnp.cos(arg)) - dense(w1s_ref, jnp.sin(arg))
    h = jnp.sin(fr * (h + b1_ref[...]))
    h = jnp.sin(fr * (dense(w2_ref, h) + b2_ref[...]))
    h = jnp.sin(fr * (dense(w3_ref, h) + b3_ref[...]))
    out = lax.dot_general(h.astype(BF16), w4_ref[...].astype(BF16), (((0,), (0,)), ((), ())),
                          preferred_element_type=F32)
    j_col, idx_col = tap_index((tr, 1), 0)
    window = jnp.exp(-(idx_col * (1.0 / (seq_len - 1))) * delta_ref[...])
    o_ref[...] = jnp.where(j_col == seq_len, 0.0, out * window)


def _hyena_filter(seq_len, w1, b1, w2, b2, w3, b3, w4, freq, tr=1024):
    n_taps = 2 * seq_len
    bands = np.linspace(1e-4, FILTER_BANDS - 1, FILTER_BANDS, dtype=np.float32).reshape(-1, 1)
    deltas = np.abs(np.linspace(math.log(DECAY_TARGET) / SLOW_DECAY_PCT,
                                math.log(DECAY_TARGET) / FAST_DECAY_PCT, D_HYENA,
                                dtype=np.float32)).reshape(1, D_HYENA)
    tiles_fwd = seq_len // tr
    const = lambda i: (0, 0)
    col = lambda v: v.reshape(-1, 1)
    full = lambda a: pl.BlockSpec(a.shape, const)
    operands = [jnp.asarray(bands), col(w1[0]), w1[1:1 + FILTER_BANDS].T,
                w1[1 + FILTER_BANDS:].T, col(b1), w2.T, col(b2), w3.T, col(b3)]
    return pl.pallas_call(
        functools.partial(_filter_kernel, seq_len=seq_len, tr=tr),
        grid=(n_taps // tr,),
        in_specs=[full(a) for a in operands] + [
            pl.BlockSpec((FILTER_HIDDEN, D_HYENA), lambda i: (0, i // tiles_fwd)),
            pl.BlockSpec((FILTER_HIDDEN, 1), const),
            pl.BlockSpec((1, D_HYENA), const),
        ],
        out_specs=pl.BlockSpec((tr, D_HYENA), lambda i: (i, 0)),
        out_shape=jax.ShapeDtypeStruct((n_taps, D_HYENA), F32),
        compiler_params=_params("parallel"),
    )(*operands, w4, col(freq), jnp.asarray(deltas))


class _FftPlan:
    def __init__(self, seq_len):
        self.n = 2 * seq_len
        self.fast = FFT_FAST
        self.slow = self.n // FFT_FAST
        self.slow_half = self.slow // 2
        self.ks = self.slow_half + 1
        self.ks_pad = -(-self.ks // SUBLANES) * SUBLANES
        self.ks_group = max(g for g in range(1, FFT_KS_GROUP_MAX + 1) if self.ks_pad % g == 0)
        S, F, N = self.slow, self.fast, self.n
        ks = np.arange(self.ks_pad, dtype=np.float64)[:, None]
        valid = (ks < self.ks)

        def stage_a(n_s):
            s = np.arange(n_s, dtype=np.float64)[None, :]
            ang = 2.0 * np.pi * ks * s / S
            return np.concatenate([np.where(valid, np.cos(ang), 0.0),
                                   np.where(valid, -np.sin(ang), 0.0)], axis=0)

        self.a_half = stage_a(self.slow_half).astype(np.float32)
        self.a_full = stage_a(self.slow).astype(np.float32)
        s = np.arange(self.slow_half, dtype=np.float64)[:, None]
        kk = np.arange(self.ks_pad, dtype=np.float64)[None, :]
        weight = np.where((kk == 0) | (kk == self.slow_half), 1.0, 2.0) * (kk < self.ks) / N
        ang = 2.0 * np.pi * s * kk / S
        self.a_inv = np.concatenate([weight * np.cos(ang), -weight * np.sin(ang)],
                                    axis=1).astype(np.float32)
        f = np.arange(F, dtype=np.float64)
        ang = 2.0 * np.pi * np.outer(np.arange(self.ks_pad, dtype=np.float64), f) / N
        self.tw_re = np.cos(ang).reshape(-1, 1).astype(np.float32)
        self.tw_im = (-np.sin(ang)).reshape(-1, 1).astype(np.float32)
        ang = 2.0 * np.pi * np.outer(f, f) / F
        self.f_re = np.cos(ang).astype(np.float32)
        self.f_im = (-np.sin(ang)).astype(np.float32)


def _dft_dot(a_ref, x):
    return jnp.dot(a_ref[...], x.astype(BF16), preferred_element_type=F32)


def _stage_a_kernel(a_ref, x_ref, o_ref, *, batch):
    m, k = a_ref.shape
    x2d = x_ref.reshape(batch * k * FFT_F_TILE, LANES)
    o2d = o_ref.reshape(batch * m * FFT_F_TILE, LANES)
    for b in range(batch):
        for j in range(FFT_F_TILE):
            x = x2d[pl.ds(b * k * FFT_F_TILE + j, k, stride=FFT_F_TILE), :]
            o2d[pl.ds(b * m * FFT_F_TILE + j, m, stride=FFT_F_TILE), :] = _dft_dot(a_ref, x)


def _stage_a(a, x, inverse):
    F, C = x.shape[-2:]
    B = x.shape[0]
    M, K = a.shape
    x_dims, o_dims = ((2, K // 2), (M,)) if inverse else ((K,), (2, M // 2))

    def spec(dims):
        zeros = (0,) * len(dims)
        return pl.BlockSpec((B,) + dims + (FFT_F_TILE, LANES), lambda f, c: (0,) + zeros + (f, c))

    return pl.pallas_call(
        functools.partial(_stage_a_kernel, batch=B),
        grid=(F // FFT_F_TILE, C // LANES),
        in_specs=[pl.BlockSpec((M, K), lambda f, c: (0, 0)), spec(x_dims)],
        out_specs=spec(o_dims),
        out_shape=jax.ShapeDtypeStruct((B,) + o_dims + (F, C), F32),
        compiler_params=_params("parallel", "parallel"),
    )(jnp.asarray(a, BF16), x)


def _stage_b_tile(fr_ref, fi_ref, re, im, conj=False):
    both = jnp.concatenate([re, im], axis=-1)
    pr = _dft_dot(fr_ref, both)
    pi = _dft_dot(fi_ref, both)
    w = re.shape[-1]
    if conj:
        return pr[:, :w] + pi[:, w:], pr[:, w:] - pi[:, :w]
    return pr[:, :w] - pi[:, w:], pr[:, w:] + pi[:, :w]


def _fft_fwd_kernel(a_ref, twr_ref, twi_ref, fr_ref, fi_ref, o_ref):
    for g in range(a_ref.shape[2] // FFT_FAST):
        rows = slice(g * FFT_FAST, (g + 1) * FFT_FAST)
        ar, ai = a_ref[0, 0, rows, :], a_ref[0, 1, rows, :]
        tr, ti = twr_ref[rows, :], twi_ref[rows, :]
        xr, xi = _stage_b_tile(fr_ref, fi_ref, ar * tr - ai * ti, ar * ti + ai * tr)
        o_ref[0, 0, rows, :] = xr
        o_ref[0, 1, rows, :] = xi


def _fft_conv_kernel(a_ref, k_ref, twr_ref, twi_ref, fr_ref, fi_ref, o_ref):
    for g in range(a_ref.shape[2] // FFT_FAST):
        rows = slice(g * FFT_FAST, (g + 1) * FFT_FAST)
        ar, ai = a_ref[0, 0, rows, :], a_ref[0, 1, rows, :]
        tr, ti = twr_ref[rows, :], twi_ref[rows, :]
        xr, xi = _stage_b_tile(fr_ref, fi_ref, ar * tr - ai * ti, ar * ti + ai * tr)
        kr, ki = k_ref[0, 0, rows, :], k_ref[0, 1, rows, :]
        yr, yi = xr * kr - xi * ki, xr * ki + xi * kr
        br, bi = _stage_b_tile(fr_ref, fi_ref, yr, yi, conj=True)
        o_ref[0, 0, rows, :] = br * tr + bi * ti
        o_ref[0, 1, rows, :] = bi * tr - br * ti


def _fft_stage_b(plan, a, k=None):
    B, _, R, C = a.shape
    rows = plan.ks_group * FFT_FAST
    data = pl.BlockSpec((1, 2, rows, LANES), lambda r, c, b: (b, 0, r, c))
    tw = pl.BlockSpec((rows, LANES), lambda r, c, b: (r, 0))
    mat = pl.BlockSpec((FFT_FAST, FFT_FAST), lambda r, c, b: (0, 0))
    lane_bcast = lambda col: jnp.broadcast_to(jnp.asarray(col), (col.shape[0], LANES))
    consts = (lane_bcast(plan.tw_re), lane_bcast(plan.tw_im),
              jnp.asarray(plan.f_re, BF16), jnp.asarray(plan.f_im, BF16))
    common = dict(
        grid=(R // rows, C // LANES, B),
        out_specs=data,
        out_shape=jax.ShapeDtypeStruct(a.shape, F32),
        compiler_params=_params("parallel", "parallel", "arbitrary"),
    )
    if k is None:
        return pl.pallas_call(_fft_fwd_kernel, in_specs=[data, tw, tw, mat, mat],
                              **common)(a, *consts)
    kspec = pl.BlockSpec((1, 2, rows, LANES), lambda r, c, b: (0, 0, r, c))
    return pl.pallas_call(_fft_conv_kernel, in_specs=[data, kspec, tw, tw, mat, mat],
                          **common)(a, k, *consts)


def _fft_long_conv(z, kern, plan):
    B, L, C = z.shape
    S, F, P = plan.slow, plan.fast, plan.ks_pad
    ka = _stage_a(plan.a_full, kern.reshape(1, S, F, C), inverse=False)
    k_spec = _fft_stage_b(plan, ka.reshape(1, 2, P * F, C))
    za = _stage_a(plan.a_half, z.reshape(B, S // 2, F, C), inverse=False)
    ya = _fft_stage_b(plan, za.reshape(B, 2, P * F, C), k_spec)
    y = _stage_a(plan.a_inv, ya.reshape(B, 2, P, F, C), inverse=True)
    return y.reshape(B, L, C)


def _scan_constants():
    C = SCAN_CHUNK
    n_lv = len(SCAN_LEVELS)
    tri = np.tril(np.ones((C, C), np.float32))
    upper = np.zeros((n_lv, C, 1), np.float32)
    mask = np.zeros((n_lv + 1, C, C), np.float32)
    for li, h in enumerate(SCAN_LEVELS):
        for t in range(C):
            r = (t // (2 * h)) * 2 * h + h
            if t % (2 * h) >= h:
                upper[li, t, 0] = 1.0
                mask[li, t, r - h:r] = 1.0
    mask[n_lv] = np.eye(C, dtype=np.float32)
    flip = lambda m: m[:, ::-1, ::-1]
    return (np.stack([tri, tri[::-1, ::-1]]), np.stack([upper, upper[:, ::-1]]),
            np.stack([mask, flip(mask)]))


def _ref_rows(h, reverse):
    rows = []
    for g in range(SCAN_CHUNK // SUBLANES):
        pair = []
        for t in (g * SUBLANES, g * SUBLANES + SUBLANES // 2):
            start = (t // (2 * h)) * 2 * h
            pair.append(start + h if reverse else start + h - 1)
        rows.append(tuple(pair))
    return rows


def _scan_chunk(q_ref, f_ref, v_ref, lb_table, tri, up_ref, mask_ref, o_ref, b_ref, state_ref,
                *, layer, reverse):
    n_lv = len(SCAN_LEVELS)
    C = SCAN_CHUNK

    e = jnp.exp(lb_table - jnp.max(lb_table, axis=0, keepdims=True))
    prob = e / jnp.sum(e, axis=0, keepdims=True)
    lb = jnp.zeros((1, D_HGRN), F32)
    for l in range(1, layer + 1):
        lb = lb + prob[l:l + 1, :]

    q = q_ref[...]
    q = q / (1.0 + jnp.exp(-q))
    z = f_ref[...]
    w = jnp.exp(-jnp.abs(z))
    inv = 1.0 / (1.0 + w)
    log_sig = jnp.minimum(z, 0.0) + jnp.log(inv)
    key = (1.0 - lb) * (jnp.where(z >= 0.0, w, 1.0) * inv)
    log_a = jnp.log(lb)
    log_b = jnp.log1p(-lb) + log_sig
    gate = jnp.maximum(log_a, log_b) + jnp.log(1.0 + jnp.exp(-jnp.abs(log_a - log_b)))
    vb = v_ref[...].astype(BF16)

    gate2 = gate * math.log2(math.e)
    g1 = gate2.astype(BF16)
    rem = gate2 - g1.astype(F32)
    g2 = rem.astype(BF16)
    g3 = (rem - g2.astype(F32)).astype(BF16)
    b_inc = (jnp.dot(tri, g1, preferred_element_type=F32)
             + jnp.dot(tri, g2, preferred_element_type=F32)
             + jnp.dot(tri, g3, preferred_element_type=F32))
    b_ref[...] = b_inc

    def bcast_row(r):
        return jnp.broadcast_to(b_ref[pl.ds(r, 1), :], (SUBLANES, D_HGRN))

    first_half = lax.broadcasted_iota(jnp.int32, (SUBLANES, 1), 0) < SUBLANES // 2
    scores = [None] * HGRN_HEADS
    for li, h in enumerate(SCAN_LEVELS):
        is_query = up_ref[li] > 0.5
        if h == 1:
            x = jnp.where(is_query, q * (1.0 - key), key)
        else:
            groups = []
            for r0, r1 in _ref_rows(h, reverse):
                ref = bcast_row(r0)
                groups.append(ref if r1 == r0 else jnp.where(first_half, ref, bcast_row(r1)))
            diff = pltpu.bitcast(b_inc - jnp.concatenate(groups, axis=0), jnp.int32)
            neg_dist = pltpu.bitcast(diff | jnp.int32(-2 ** 31), F32)
            x = jnp.where(is_query, q, key) * jnp.exp2(neg_dist)
        xb = x.astype(BF16)
        keep = mask_ref[li] > 0.5
        for hd in range(HGRN_HEADS):
            xh = xb[:, hd * HEAD_DIM:(hd + 1) * HEAD_DIM]
            p = lax.dot_general(xh, xh, (((1,), (1,)), ((), ())), preferred_element_type=F32)
            scores[hd] = jnp.where(keep, p, 0.0 if scores[hd] is None else scores[hd])

    b_total = bcast_row(0 if reverse else C - 1)
    b_rest = jnp.concatenate([b_total] * (C // SUBLANES), axis=0) - b_inc
    q_dec = (q * jnp.exp2(b_inc)).astype(BF16)
    k_dec = (key * jnp.exp2(b_rest)).astype(BF16)
    total = jnp.exp2(b_total[0:1, :])
    qb, kb = q.astype(BF16), key.astype(BF16)
    on_diag = mask_ref[n_lv] > 0.5
    for hd in range(HGRN_HEADS):
        cols = slice(hd * HEAD_DIM, (hd + 1) * HEAD_DIM)
        diag = lax.dot_general(qb[:, cols], kb[:, cols], (((1,), (1,)), ((), ())),
                               preferred_element_type=F32)
        p = jnp.where(on_diag, diag, scores[hd]).astype(BF16)
        st = state_ref[hd]
        o = jnp.dot(p, vb[:, cols], preferred_element_type=F32)
        o = o + lax.dot_general(q_dec[:, cols], st.astype(BF16), (((1,), (1,)), ((), ())),
                                preferred_element_type=F32)
        o_ref[:, cols] = o
        upd = lax.dot_general(vb[:, cols], k_dec[:, cols], (((0,), (0,)), ((), ())),
                              preferred_element_type=F32)
        state_ref[hd] = st * total[:, cols] + upd


def _hgrn_scan_kernel(qf_ref, ff_ref, vf_ref, qb_ref, fb_ref, vb_ref, lbt_ref, tri_ref, up_ref,
                      mask_ref, of_ref, ob_ref, b_ref, state_ref, *, layer):
    @pl.when(pl.program_id(1) == 0)
    def _():
        state_ref[...] = jnp.zeros_like(state_ref)

    for s in range(SCAN_CHUNKS_PER_STEP):
        rows = pl.ds(s * SCAN_CHUNK, SCAN_CHUNK)
        _scan_chunk(qf_ref.at[rows], ff_ref.at[rows], vf_ref.at[rows], lbt_ref[0], tri_ref[0],
                    up_ref.at[0], mask_ref.at[0], of_ref.at[rows], b_ref.at[0], state_ref.at[0],
                    layer=layer, reverse=False)
        rows = pl.ds((SCAN_CHUNKS_PER_STEP - 1 - s) * SCAN_CHUNK, SCAN_CHUNK)
        _scan_chunk(qb_ref.at[rows], fb_ref.at[rows], vb_ref.at[rows], lbt_ref[1], tri_ref[1],
                    up_ref.at[1], mask_ref.at[1], ob_ref.at[rows], b_ref.at[1], state_ref.at[1],
                    layer=layer, reverse=True)


def _hgrn_scan(proj, lb_table, layer, batch, seq_len):
    T = proj.shape[0]
    C = SCAN_CHUNK
    rows_per_step = SCAN_CHUNKS_PER_STEP * C
    n_chunks = seq_len // rows_per_step
    tri, upper, mask = _scan_constants()
    col0 = (3 * D_HYENA) // D_HGRN

    fwd = lambda b, c: b * n_chunks + c
    bwd = lambda b, c: b * n_chunks + n_chunks - 1 - c
    chunk = lambda rows, col: pl.BlockSpec((rows_per_step, D_HGRN), lambda b, c: (rows(b, c), col))
    whole = lambda a: pl.BlockSpec(a.shape, lambda b, c: (0,) * a.ndim)
    out = jax.ShapeDtypeStruct((T, D_HGRN), F32)
    return pl.pallas_call(
        functools.partial(_hgrn_scan_kernel, layer=layer),
        grid=(batch, n_chunks),
        in_specs=[
            chunk(fwd, col0), chunk(fwd, col0 + 1), chunk(fwd, col0 + 3),
            chunk(bwd, col0), chunk(bwd, col0 + 2), chunk(bwd, col0 + 3),
            whole(lb_table), whole(tri), whole(upper), whole(mask),
        ],
        out_specs=[chunk(fwd, 0), chunk(bwd, 0)],
        out_shape=[out, out],
        scratch_shapes=[pltpu.VMEM((2, C, D_HGRN), F32),
                        pltpu.VMEM((2, HGRN_HEADS, HEAD_DIM, HEAD_DIM), F32)],
        compiler_params=_params("parallel", "arbitrary"),
    )(proj, proj, proj, proj, proj, proj, lb_table, jnp.asarray(tri, BF16), jnp.asarray(upper),
      jnp.asarray(mask))


def _group_mean_matrix(group):
    idx = np.arange(D_HYENA) // group
    return (idx[:, None] == idx[None, :]).astype(np.float32)


def _mix_out_kernel(x_ref, y_ref, z_ref, x0_ref, of_ref, ob_ref, g_ref, skip_ref,
                    hy_gain_ref, hg_gain_ref, grp_hy_ref, grp_hg_ref, w_ref, post_ref, o_ref):
    z = z_ref[...]
    yh = x0_ref[...] * (y_ref[...] + skip_ref[...] * z)
    ms = _group_sums(yh * yh, grp_hy_ref[...]) * (HYENA_GROUPS / D_HYENA)
    yh = yh * lax.rsqrt(ms + EPS) * hy_gain_ref[...]
    o = of_ref[...] + ob_ref[...]
    ms = _group_sums(o * o, grp_hg_ref[...]) * (1.0 / HEAD_DIM)
    g = g_ref[...]
    o = o * lax.rsqrt(ms + EPS) * hg_gain_ref[...] * (g * (1.0 / (1.0 + jnp.exp(-g))))
    mix = (jnp.dot(yh.astype(BF16), w_ref[:D_HYENA, :], preferred_element_type=F32)
           + jnp.dot(o.astype(BF16), w_ref[D_HYENA:, :], preferred_element_type=F32))
    o_ref[...] = x_ref[...] + mix * _rms_scale(mix) * post_ref[...]


def _mix_out(x2d, y, z, x0, o_fwd, o_bwd, proj, skip, hy_gain, hg_gain, w_bf16, post_gain,
             tm=1024):
    T = x2d.shape[0]
    gate_col = D_IN // D_HGRN - 1
    half = lambda: pl.BlockSpec((tm, D_HYENA), lambda i: (i, 0))
    vec = lambda n: pl.BlockSpec((1, n), lambda i: (0, 0))
    sq = lambda: pl.BlockSpec((D_HYENA, D_HYENA), lambda i: (0, 0))
    row = lambda v: v.reshape(1, -1)
    return pl.pallas_call(
        _mix_out_kernel,
        grid=(T // tm,),
        in_specs=[
            pl.BlockSpec((tm, D_MODEL), lambda i: (i, 0)),
            half(), half(), half(), half(), half(),
            pl.BlockSpec((tm, D_HGRN), lambda i: (i, gate_col)),
            vec(D_HYENA), vec(D_HYENA), vec(D_HGRN), sq(), sq(),
            pl.BlockSpec((D_MODEL, D_MODEL), lambda i: (0, 0)),
            vec(D_MODEL),
        ],
        out_specs=pl.BlockSpec((tm, D_MODEL), lambda i: (i, 0)),
        out_shape=jax.ShapeDtypeStruct((T, D_MODEL), F32),
        compiler_params=_params("parallel"),
    )(x2d, y, z, x0, o_fwd, o_bwd, proj, row(skip), row(hy_gain), row(hg_gain),
      jnp.asarray(_group_mean_matrix(D_HYENA // HYENA_GROUPS), BF16),
      jnp.asarray(_group_mean_matrix(HEAD_DIM), BF16), w_bf16, row(post_gain))


def _gelu_tanh(x):
    c = -2.0 * math.sqrt(2.0 / math.pi) * math.log2(math.e)
    return x / (1.0 + jnp.exp2(x * (c + (0.044715 * c) * (x * x))))


def _ffn_kernel(prev_ref, main_ref, next_ref, pre_ref, wa_ref, wb_ref, cwa_ref, cwb_ref,
                cba_ref, cbb_ref, wd_ref, post_ref, o_ref, h_ref, acc_ref, ua_ref, ub_ref,
                *, tiles_per_seq, tm):
    j = pl.program_id(1)
    pos = pl.program_id(0) % tiles_per_seq

    @pl.when(j == 0)
    def _():
        gain = pre_ref[...]

        def normed(x):
            return (x * _rms_scale(x) * gain).astype(BF16)

        prev = normed(prev_ref[...])
        nxt = normed(next_ref[...])
        h_ref[0:FFN_HALO, :] = jnp.where(pos == 0, jnp.zeros_like(prev), prev)
        h_ref[FFN_HALO:FFN_HALO + tm, :] = normed(main_ref[...])
        h_ref[FFN_HALO + tm:, :] = jnp.where(pos == tiles_per_seq - 1, jnp.zeros_like(nxt), nxt)
        acc_ref[...] = jnp.zeros_like(acc_ref)

    h = h_ref[...]

    def conv_part(w_ref_, cw_ref_, cb_ref_, u_ref):
        u_ref[...] = jnp.dot(h, w_ref_[...], preferred_element_type=F32)
        cw = cw_ref_[...]
        down = u_ref[pl.ds(FFN_HALO - 1, tm), :]
        mid = u_ref[pl.ds(FFN_HALO, tm), :]
        up = u_ref[pl.ds(FFN_HALO + 1, tm), :]
        return down * cw[0:1, :] + mid * cw[1:2, :] + up * cw[2:3, :] + cb_ref_[...]

    a = conv_part(wa_ref, cwa_ref, cba_ref, ua_ref)
    b = conv_part(wb_ref, cwb_ref, cbb_ref, ub_ref)
    act = (_gelu_tanh(a) * b).astype(BF16)
    acc_ref[...] += jnp.dot(act, wd_ref[...], preferred_element_type=F32)

    @pl.when(j == pl.num_programs(1) - 1)
    def _():
        ff = acc_ref[...]
        o_ref[...] = main_ref[...] + ff * _rms_scale(ff) * post_ref[...]


def _ffn(x2d, pre_gain, w_up_bf16, conv_w, conv_b, w_down_bf16, post_gain, seq_len,
         tm=1024, tf=1024):
    T = x2d.shape[0]
    halo_per_tile = tm // FFN_HALO
    last_halo = T // FFN_HALO - 1
    n_f = D_FF // tf
    row = lambda v: v.reshape(1, -1)
    return pl.pallas_call(
        functools.partial(_ffn_kernel, tiles_per_seq=seq_len // tm, tm=tm),
        grid=(T // tm, n_f),
        in_specs=[
            pl.BlockSpec((FFN_HALO, D_MODEL),
                         lambda i, j: (jnp.maximum(i * halo_per_tile - 1, 0), 0)),
            pl.BlockSpec((tm, D_MODEL), lambda i, j: (i, 0)),
            pl.BlockSpec((FFN_HALO, D_MODEL),
                         lambda i, j: (jnp.minimum((i + 1) * halo_per_tile, last_halo), 0)),
            pl.BlockSpec((1, D_MODEL), lambda i, j: (0, 0)),
            pl.BlockSpec((D_MODEL, tf), lambda i, j: (0, j)),
            pl.BlockSpec((D_MODEL, tf), lambda i, j: (0, n_f + j)),
            pl.BlockSpec((3, tf), lambda i, j: (0, j)),
            pl.BlockSpec((3, tf), lambda i, j: (0, n_f + j)),
            pl.BlockSpec((1, tf), lambda i, j: (0, j)),
            pl.BlockSpec((1, tf), lambda i, j: (0, n_f + j)),
            pl.BlockSpec((tf, D_MODEL), lambda i, j: (j, 0)),
            pl.BlockSpec((1, D_MODEL), lambda i, j: (0, 0)),
        ],
        out_specs=pl.BlockSpec((tm, D_MODEL), lambda i, j: (i, 0)),
        out_shape=jax.ShapeDtypeStruct((T, D_MODEL), F32),
        scratch_shapes=[pltpu.VMEM((tm + 2 * FFN_HALO, D_MODEL), BF16),
                        pltpu.VMEM((tm, D_MODEL), F32)]
        + [pltpu.VMEM((tm + 2 * FFN_HALO, tf), F32)] * 2,
        compiler_params=_params("parallel", "arbitrary"),
    )(x2d, x2d, x2d, row(pre_gain), w_up_bf16, w_up_bf16, conv_w, conv_w,
      row(conv_b), row(conv_b), w_down_bf16, row(post_gain))


def _trunk(x, p):
    B, L, _ = x.shape
    x2d = x.reshape(B * L, D_MODEL)
    plan = _FftPlan(L)
    for l in range(p["w_in"].shape[0]):
        proj = _in_proj(x2d, p["norm_mix_pre"][l], p["w_in_bf16"][l])
        x0, z = _hyena_pre(proj, p["hyena_conv_w"][l], p["hyena_conv_b"][l], L)
        kern = _hyena_filter(L, p["filt_w1"][l], p["filt_b1"][l], p["filt_w2"][l], p["filt_b2"][l],
                             p["filt_w3"][l], p["filt_b3"][l], p["filt_w4"][l], p["filt_freq"][l])
        y = _fft_long_conv(z.reshape(B, L, D_HYENA), kern, plan).reshape(B * L, D_HYENA)
        o_fwd, o_bwd = _hgrn_scan(proj, p["hgrn_lower_bounds"], l, B, L)
        x2d = _mix_out(x2d, y, z, x0, o_fwd, o_bwd, proj, p["hyena_skip"][l], p["hyena_out_norm"][l],
                       p["hgrn_out_norm"][l], p["w_out_bf16"][l], p["norm_mix_post"][l])
        x2d = _ffn(x2d, p["norm_ffn_pre"][l], p["ffn_w_up_bf16"][l], p["ffn_conv_w"][l],
                   p["ffn_conv_b"][l], p["ffn_w_down_bf16"][l], p["norm_ffn_post"][l], L)
    return x2d.reshape(B, L, D_MODEL)


def kernel(x_prompt, x_sample, norm_mix_pre, norm_mix_post, norm_ffn_pre, norm_ffn_post, w_in, hyena_conv_w, hyena_conv_b, filt_w1, filt_b1, filt_w2, filt_b2, filt_w3, filt_b3, filt_w4, filt_freq, hyena_skip, hyena_out_norm, hgrn_lower_bounds, hgrn_out_norm, w_out, ffn_w_up, ffn_conv_w, ffn_conv_b, ffn_w_down):
    p = dict(
        norm_mix_pre=norm_mix_pre, norm_mix_post=norm_mix_post, norm_ffn_pre=norm_ffn_pre,
        norm_ffn_post=norm_ffn_post, w_in=w_in, hyena_conv_w=hyena_conv_w,
        hyena_conv_b=hyena_conv_b, filt_w1=filt_w1, filt_b1=filt_b1, filt_w2=filt_w2,
        filt_b2=filt_b2, filt_w3=filt_w3, filt_b3=filt_b3, filt_w4=filt_w4, filt_freq=filt_freq,
        hyena_skip=hyena_skip, hyena_out_norm=hyena_out_norm,
        hgrn_lower_bounds=hgrn_lower_bounds, hgrn_out_norm=hgrn_out_norm,
        ffn_conv_w=ffn_conv_w, ffn_conv_b=ffn_conv_b,
        w_in_bf16=w_in.astype(BF16), w_out_bf16=w_out.astype(BF16),
        ffn_w_up_bf16=ffn_w_up.astype(BF16), ffn_w_down_bf16=ffn_w_down.astype(BF16),
    )
    return (_trunk(x_prompt, p), _trunk(x_sample, p))
```

```python
import functools
import math

import numpy as np
import jax
import jax.numpy as jnp
from jax import lax
from jax.experimental import pallas as pl
from jax.experimental.pallas import tpu as pltpu

F32 = jnp.float32
BF16 = jnp.bfloat16
HIGHEST = lax.Precision.HIGHEST

D_MODEL = 1024
D_HYENA = 512
HYENA_GROUPS = 8
D_HGRN = 512
HGRN_HEADS = 4
HEAD_DIM = 128
D_IN = 3 * D_HYENA + 5 * D_HGRN
D_FF = 4 * D_MODEL
FILTER_EMB = 33
FILTER_BANDS = 16
FILTER_HIDDEN = 64
DECAY_TARGET = 1e-2
FAST_DECAY_PCT = 0.3
SLOW_DECAY_PCT = 1.5
EPS = 1e-6

SUBLANES = 8
LANES = 128
BF16_ROWS = 16
VMEM_LIMIT_BYTES = 56 * 1024 * 1024

FFT_FAST = 128
FFT_KS_GROUP_MAX = 20
FFT_F_TILE = 16
FFN_HALO = 16
SCAN_CHUNK = 128
SCAN_CHUNKS_PER_STEP = 4
SCAN_LEVELS = (64, 32, 16, 8, 4, 2, 1)


def _params(*semantics):
    return pltpu.CompilerParams(dimension_semantics=semantics,
                                vmem_limit_bytes=VMEM_LIMIT_BYTES)


def _rms_scale(x):
    return lax.rsqrt(jnp.mean(x * x, axis=-1, keepdims=True) + EPS)


def _group_sums(sq, same_group):
    return jnp.dot(sq.astype(BF16), same_group, preferred_element_type=F32)


def _in_proj_kernel(x_ref, g_ref, w_ref, o_ref, h_ref):
    @pl.when(pl.program_id(1) == 0)
    def _():
        x = x_ref[...]
        h_ref[...] = (x * _rms_scale(x) * g_ref[...]).astype(BF16)

    o_ref[...] = jnp.dot(h_ref[...], w_ref[...], preferred_element_type=F32).astype(o_ref.dtype)


def _in_proj(x2d, gain, w_bf16, tm=1024, tn=2048):
    T = x2d.shape[0]
    N = w_bf16.shape[1]
    return pl.pallas_call(
        _in_proj_kernel,
        grid=(T // tm, N // tn),
        in_specs=[
            pl.BlockSpec((tm, D_MODEL), lambda i, j: (i, 0)),
            pl.BlockSpec((1, D_MODEL), lambda i, j: (0, 0)),
            pl.BlockSpec((D_MODEL, tn), lambda i, j: (0, j)),
        ],
        out_specs=pl.BlockSpec((tm, tn), lambda i, j: (i, j)),
        out_shape=jax.ShapeDtypeStruct((T, N), BF16),
        scratch_shapes=[pltpu.VMEM((tm, D_MODEL), BF16)],
        compiler_params=_params("parallel", "arbitrary"),
    )(x2d, gain.reshape(1, D_MODEL), w_bf16)


def _shift_rows(main, prev_row, next_row):
    tm = main.shape[0]
    rows = lax.broadcasted_iota(jnp.int32, (tm, 1), 0)
    down = jnp.where(rows == 0, prev_row, pltpu.roll(main, 1, 0))
    up = jnp.where(rows == tm - 1, next_row, pltpu.roll(main, tm - 1, 0))
    return down, up


def _hyena_pre_kernel(prev_ref, main_ref, next_ref, w_ref, b_ref, x0_ref, z_ref,
                      *, tiles_per_seq):
    pos = pl.program_id(0) % tiles_per_seq
    main = main_ref[...].astype(F32)
    prev_row = jnp.where(pos == 0, 0.0, prev_ref[...].astype(F32)[BF16_ROWS - 1:BF16_ROWS, :])
    next_row = jnp.where(pos == tiles_per_seq - 1, 0.0, next_ref[...].astype(F32)[0:1, :])
    down, up = _shift_rows(main, prev_row, next_row)
    w = w_ref[...]
    u = down * w[0:1, :] + main * w[1:2, :] + up * w[2:3, :] + b_ref[...]
    x0_ref[...] = u[:, :D_HYENA].astype(x0_ref.dtype)
    z_ref[...] = u[:, D_HYENA:2 * D_HYENA] * u[:, 2 * D_HYENA:]


def _hyena_pre(proj, conv_w, conv_b, seq_len, tm=512):
    T = proj.shape[0]
    width = 3 * D_HYENA
    halo_per_tile = tm // BF16_ROWS
    last_halo = T // BF16_ROWS - 1
    return pl.pallas_call(
        functools.partial(_hyena_pre_kernel, tiles_per_seq=seq_len // tm),
        grid=(T // tm,),
        in_specs=[
            pl.BlockSpec((BF16_ROWS, width),
                         lambda i: (jnp.maximum(i * halo_per_tile - 1, 0), 0)),
            pl.BlockSpec((tm, width), lambda i: (i, 0)),
            pl.BlockSpec((BF16_ROWS, width),
                         lambda i: (jnp.minimum((i + 1) * halo_per_tile, last_halo), 0)),
            pl.BlockSpec((3, width), lambda i: (0, 0)),
            pl.BlockSpec((1, width), lambda i: (0, 0)),
        ],
        out_specs=[pl.BlockSpec((tm, D_HYENA), lambda i: (i, 0)),
                   pl.BlockSpec((tm, D_HYENA), lambda i: (i, 0))],
        out_shape=[jax.ShapeDtypeStruct((T, D_HYENA), BF16),
                   jax.ShapeDtypeStruct((T, D_HYENA), F32)],
        compiler_params=_params("parallel"),
    )(proj, proj, proj, conv_w, conv_b.reshape(1, width))


def _filter_kernel(band_ref, w1t_ref, w1c_ref, w1s_ref, b1_ref, w2_ref, b2_ref, w3_ref, b3_ref,
                   w4_ref, freq_ref, delta_ref, o_ref, *, seq_len, tr):
    row0 = pl.program_id(0) * tr

    def tap_index(shape, axis):
        j = row0 + lax.broadcasted_iota(jnp.int32, shape, axis)
        return j, jnp.where(j < seq_len, j, 2 * seq_len - j).astype(F32)

    _, idx = tap_index((1, tr), 1)
    t = idx * (1.0 / (seq_len - 1))
    arg = band_ref[...] * ((2.0 * math.pi / seq_len) * idx)
    fr = freq_ref[...]

    def dense(w_ref_, a):
        return jnp.dot(w_ref_[...], a, precision=HIGHEST, preferred_element_type=F32)

    h = w1t_ref[...] * t + dense(w1c_ref, jnp.cos(arg)) - dense(w1s_ref, jnp.sin(arg))
    h = jnp.sin(fr * (h + b1_ref[...]))
    h = jnp.sin(fr * (dense(w2_ref, h) + b2_ref[...]))
    h = jnp.sin(fr * (dense(w3_ref, h) + b3_ref[...]))
    out = lax.dot_general(h.astype(BF16), w4_ref[...].astype(BF16), (((0,), (0,)), ((), ())),
                          preferred_element_type=F32)
    j_col, idx_col = tap_index((tr, 1), 0)
    window = jnp.exp(-(idx_col * (1.0 / (seq_len - 1))) * delta_ref[...])
    o_ref[...] = jnp.where(j_col == seq_len, 0.0, out * window)


def _hyena_filter(seq_len, w1, b1, w2, b2, w3, b3, w4, freq, tr=1024):
    n_taps = 2 * seq_len
    bands = np.linspace(1e-4, FILTER_BANDS - 1, FILTER_BANDS, dtype=np.float32).reshape(-1, 1)
    deltas = np.abs(np.linspace(math.log(DECAY_TARGET) / SLOW_DECAY_PCT,
                                math.log(DECAY_TARGET) / FAST_DECAY_PCT, D_HYENA,
                                dtype=np.float32)).reshape(1, D_HYENA)
    tiles_fwd = seq_len // tr
    const = lambda i: (0, 0)
    col = lambda v: v.reshape(-1, 1)
    full = lambda a: pl.BlockSpec(a.shape, const)
    operands = [jnp.asarray(bands), col(w1[0]), w1[1:1 + FILTER_BANDS].T,
                w1[1 + FILTER_BANDS:].T, col(b1), w2.T, col(b2), w3.T, col(b3)]
    return pl.pallas_call(
        functools.partial(_filter_kernel, seq_len=seq_len, tr=tr),
        grid=(n_taps // tr,),
        in_specs=[full(a) for a in operands] + [
            pl.BlockSpec((FILTER_HIDDEN, D_HYENA), lambda i: (0, i // tiles_fwd)),
            pl.BlockSpec((FILTER_HIDDEN, 1), const),
            pl.BlockSpec((1, D_HYENA), const),
        ],
        out_specs=pl.BlockSpec((tr, D_HYENA), lambda i: (i, 0)),
        out_shape=jax.ShapeDtypeStruct((n_taps, D_HYENA), F32),
        compiler_params=_params("parallel"),
    )(*operands, w4, col(freq), jnp.asarray(deltas))


class _FftPlan:
    def __init__(self, seq_len):
        self.n = 2 * seq_len
        self.fast = FFT_FAST
        self.slow = self.n // FFT_FAST
        self.slow_half = self.slow // 2
        self.ks = self.slow_half + 1
        self.ks_pad = -(-self.ks // SUBLANES) * SUBLANES
        self.ks_group = max(g for g in range(1, FFT_KS_GROUP_MAX + 1) if self.ks_pad % g == 0)
        S, F, N = self.slow, self.fast, self.n
        ks = np.arange(self.ks_pad, dtype=np.float64)[:, None]
        valid = (ks < self.ks)

        def stage_a(n_s):
            s = np.arange(n_s, dtype=np.float64)[None, :]
            ang = 2.0 * np.pi * ks * s / S
            return np.concatenate([np.where(valid, np.cos(ang), 0.0),
                                   np.where(valid, -np.sin(ang), 0.0)], axis=0)

        self.a_half = stage_a(self.slow_half).astype(np.float32)
        self.a_full = stage_a(self.slow).astype(np.float32)
        s = np.arange(self.slow_half, dtype=np.float64)[:, None]
        kk = np.arange(self.ks_pad, dtype=np.float64)[None, :]
        weight = np.where((kk == 0) | (kk == self.slow_half), 1.0, 2.0) * (kk < self.ks) / N
        ang = 2.0 * np.pi * s * kk / S
        self.a_inv = np.concatenate([weight * np.cos(ang), -weight * np.sin(ang)],
                                    axis=1).astype(np.float32)
        f = np.arange(F, dtype=np.float64)
        ang = 2.0 * np.pi * np.outer(np.arange(self.ks_pad, dtype=np.float64), f) / N
        self.tw_re = np.cos(ang).reshape(-1, 1).astype(np.float32)
        self.tw_im = (-np.sin(ang)).reshape(-1, 1).astype(np.float32)
        ang = 2.0 * np.pi * np.outer(f, f) / F
        self.f_re = np.cos(ang).astype(np.float32)
        self.f_im = (-np.sin(ang)).astype(np.float32)


def _dft_dot(a_ref, x):
    return jnp.dot(a_ref[...], x.astype(BF16), preferred_element_type=F32)


def _stage_a_kernel(a_ref, x_ref, o_ref, *, batch):
    m, k = a_ref.shape
    x2d = x_ref.reshape(batch * k * FFT_F_TILE, LANES)
    o2d = o_ref.reshape(batch * m * FFT_F_TILE, LANES)
    for b in range(batch):
        for j in range(FFT_F_TILE):
            x = x2d[pl.ds(b * k * FFT_F_TILE + j, k, stride=FFT_F_TILE), :]
            o2d[pl.ds(b * m * FFT_F_TILE + j, m, stride=FFT_F_TILE), :] = _dft_dot(a_ref, x)


def _stage_a(a, x, inverse):
    F, C = x.shape[-2:]
    B = x.shape[0]
    M, K = a.shape
    x_dims, o_dims = ((2, K // 2), (M,)) if inverse else ((K,), (2, M // 2))

    def spec(dims):
        zeros = (0,) * len(dims)
        return pl.BlockSpec((B,) + dims + (FFT_F_TILE, LANES), lambda f, c: (0,) + zeros + (f, c))

    return pl.pallas_call(
        functools.partial(_stage_a_kernel, batch=B),
        grid=(F // FFT_F_TILE, C // LANES),
        in_specs=[pl.BlockSpec((M, K), lambda f, c: (0, 0)), spec(x_dims)],
        out_specs=spec(o_dims),
        out_shape=jax.ShapeDtypeStruct((B,) + o_dims + (F, C), F32),
        compiler_params=_params("parallel", "parallel"),
    )(jnp.asarray(a, BF16), x)


def _stage_b_tile(fr_ref, fi_ref, re, im, conj=False):
    both = jnp.concatenate([re, im], axis=-1)
    pr = _dft_dot(fr_ref, both)
    pi = _dft_dot(fi_ref, both)
    w = re.shape[-1]
    if conj:
        return pr[:, :w] + pi[:, w:], pr[:, w:] - pi[:, :w]
    return pr[:, :w] - pi[:, w:], pr[:, w:] + pi[:, :w]


def _fft_fwd_kernel(a_ref, twr_ref, twi_ref, fr_ref, fi_ref, o_ref):
    for g in range(a_ref.shape[2] // FFT_FAST):
        rows = slice(g * FFT_FAST, (g + 1) * FFT_FAST)
        ar, ai = a_ref[0, 0, rows, :], a_ref[0, 1, rows, :]
        tr, ti = twr_ref[rows, :], twi_ref[rows, :]
        xr, xi = _stage_b_tile(fr_ref, fi_ref, ar * tr - ai * ti, ar * ti + ai * tr)
        o_ref[0, 0, rows, :] = xr
        o_ref[0, 1, rows, :] = xi


def _fft_conv_kernel(a_ref, k_ref, twr_ref, twi_ref, fr_ref, fi_ref, o_ref):
    for g in range(a_ref.shape[2] // FFT_FAST):
        rows = slice(g * FFT_FAST, (g + 1) * FFT_FAST)
        ar, ai = a_ref[0, 0, rows, :], a_ref[0, 1, rows, :]
        tr, ti = twr_ref[rows, :], twi_ref[rows, :]
        xr, xi = _stage_b_tile(fr_ref, fi_ref, ar * tr - ai * ti, ar * ti + ai * tr)
        kr, ki = k_ref[0, 0, rows, :], k_ref[0, 1, rows, :]
        yr, yi = xr * kr - xi * ki, xr * ki + xi * kr
        br, bi = _stage_b_tile(fr_ref, fi_ref, yr, yi, conj=True)
        o_ref[0, 0, rows, :] = br * tr + bi * ti
        o_ref[0, 1, rows, :] = bi * tr - br * ti


def _fft_stage_b(plan, a, k=None):
    B, _, R, C = a.shape
    rows = plan.ks_group * FFT_FAST
    data = pl.BlockSpec((1, 2, rows, LANES), lambda r, c, b: (b, 0, r, c))
    tw = pl.BlockSpec((rows, LANES), lambda r, c, b: (r, 0))
    mat = pl.BlockSpec((FFT_FAST, FFT_FAST), lambda r, c, b: (0, 0))
    lane_bcast = lambda col: jnp.broadcast_to(jnp.asarray(col), (col.shape[0], LANES))
    consts = (lane_bcast(plan.tw_re), lane_bcast(plan.tw_im),
              jnp.asarray(plan.f_re, BF16), jnp.asarray(plan.f_im, BF16))
    common = dict(
        grid=(R // rows, C // LANES, B),
        out_specs=data,
        out_shape=jax.ShapeDtypeStruct(a.shape, F32),
        compiler_params=_params("parallel", "parallel", "arbitrary"),
    )
    if k is None:
        return pl.pallas_call(_fft_fwd_kernel, in_specs=[data, tw, tw, mat, mat],
                              **common)(a, *consts)
    kspec = pl.BlockSpec((1, 2, rows, LANES), lambda r, c, b: (0, 0, r, c))
    return pl.pallas_call(_fft_conv_kernel, in_specs=[data, kspec, tw, tw, mat, mat],
                          **common)(a, k, *consts)


def _fft_long_conv(z, kern, plan):
    B, L, C = z.shape
    S, F, P = plan.slow, plan.fast, plan.ks_pad
    ka = _stage_a(plan.a_full, kern.reshape(1, S, F, C), inverse=False)
    k_spec = _fft_stage_b(plan, ka.reshape(1, 2, P * F, C))
    za = _stage_a(plan.a_half, z.reshape(B, S // 2, F, C), inverse=False)
    ya = _fft_stage_b(plan, za.reshape(B, 2, P * F, C), k_spec)
    y = _stage_a(plan.a_inv, ya.reshape(B, 2, P, F, C), inverse=True)
    return y.reshape(B, L, C)


def _scan_constants():
    C = SCAN_CHUNK
    n_lv = len(SCAN_LEVELS)
    tri = np.tril(np.ones((C, C), np.float32))
    upper = np.zeros((n_lv, C, 1), np.float32)
    mask = np.zeros((n_lv + 1, C, C), np.float32)
    for li, h in enumerate(SCAN_LEVELS):
        for t in range(C):
            r = (t // (2 * h)) * 2 * h + h
            if t % (2 * h) >= h:
                upper[li, t, 0] = 1.0
                mask[li, t, r - h:r] = 1.0
    mask[n_lv] = np.eye(C, dtype=np.float32)
    flip = lambda m: m[:, ::-1, ::-1]
    return (np.stack([tri, tri[::-1, ::-1]]), np.stack([upper, upper[:, ::-1]]),
            np.stack([mask, flip(mask)]))


def _ref_rows(h, reverse):
    rows = []
    for g in range(SCAN_CHUNK // SUBLANES):
        pair = []
        for t in (g * SUBLANES, g * SUBLANES + SUBLANES // 2):
            start = (t // (2 * h)) * 2 * h
            pair.append(start + h if reverse else start + h - 1)
        rows.append(tuple(pair))
    return rows


def _scan_chunk(q_ref, f_ref, v_ref, lb_table, tri, up_ref, mask_ref, o_ref, b_ref, state_ref,
                *, layer, reverse):
    n_lv = len(SCAN_LEVELS)
    C = SCAN_CHUNK

    e = jnp.exp(lb_table - jnp.max(lb_table, axis=0, keepdims=True))
    prob = e / jnp.sum(e, axis=0, keepdims=True)
    lb = jnp.zeros((1, D_HGRN), F32)
    for l in range(1, layer + 1):
        lb = lb + prob[l:l + 1, :]

    q = q_ref[...].astype(F32)
    q = q / (1.0 + jnp.exp(-q))
    z = f_ref[...].astype(F32)
    w = jnp.exp(-jnp.abs(z))
    inv = 1.0 / (1.0 + w)
    log_sig = jnp.minimum(z, 0.0) + jnp.log(inv)
    key = (1.0 - lb) * (jnp.where(z >= 0.0, w, 1.0) * inv)
    log_a = jnp.log(lb)
    log_b = jnp.log1p(-lb) + log_sig
    gate = jnp.maximum(log_a, log_b) + jnp.log(1.0 + jnp.exp(-jnp.abs(log_a - log_b)))
    vb = v_ref[...]

    gate2 = gate * math.log2(math.e)
    g1 = gate2.astype(BF16)
    rem = gate2 - g1.astype(F32)
    g2 = rem.astype(BF16)
    g3 = (rem - g2.astype(F32)).astype(BF16)
    b_inc = (jnp.dot(tri, g1, preferred_element_type=F32)
             + jnp.dot(tri, g2, preferred_element_type=F32)
             + jnp.dot(tri, g3, preferred_element_type=F32))
    b_ref[...] = b_inc

    def bcast_row(r):
        return jnp.broadcast_to(b_ref[pl.ds(r, 1), :], (SUBLANES, D_HGRN))

    first_half = lax.broadcasted_iota(jnp.int32, (SUBLANES, 1), 0) < SUBLANES // 2
    scores = [None] * HGRN_HEADS
    for li, h in enumerate(SCAN_LEVELS):
        is_query = up_ref[li] > 0.5
        if h == 1:
            x = jnp.where(is_query, q * (1.0 - key), key)
        else:
            groups = []
            for r0, r1 in _ref_rows(h, reverse):
                ref = bcast_row(r0)
                groups.append(ref if r1 == r0 else jnp.where(first_half, ref, bcast_row(r1)))
            diff = pltpu.bitcast(b_inc - jnp.concatenate(groups, axis=0), jnp.int32)
            neg_dist = pltpu.bitcast(diff | jnp.int32(-2 ** 31), F32)
            x = jnp.where(is_query, q, key) * jnp.exp2(neg_dist)
        xb = x.astype(BF16)
        keep = mask_ref[li] > 0.5
        for hd in range(HGRN_HEADS):
            xh = xb[:, hd * HEAD_DIM:(hd + 1) * HEAD_DIM]
            p = lax.dot_general(xh, xh, (((1,), (1,)), ((), ())), preferred_element_type=F32)
            scores[hd] = jnp.where(keep, p, 0.0 if scores[hd] is None else scores[hd])

    b_total = bcast_row(0 if reverse else C - 1)
    b_rest = jnp.concatenate([b_total] * (C // SUBLANES), axis=0) - b_inc
    q_dec = (q * jnp.exp2(b_inc)).astype(BF16)
    k_dec = (key * jnp.exp2(b_rest)).astype(BF16)
    total = jnp.exp2(b_total[0:1, :])
    qb, kb = q.astype(BF16), key.astype(BF16)
    on_diag = mask_ref[n_lv] > 0.5
    for hd in range(HGRN_HEADS):
        cols = slice(hd * HEAD_DIM, (hd + 1) * HEAD_DIM)
        diag = lax.dot_general(qb[:, cols], kb[:, cols], (((1,), (1,)), ((), ())),
                               preferred_element_type=F32)
        p = jnp.where(on_diag, diag, scores[hd]).astype(BF16)
        st = state_ref[hd]
        o = jnp.dot(p, vb[:, cols], preferred_element_type=F32)
        o = o + lax.dot_general(q_dec[:, cols], st.astype(BF16), (((1,), (1,)), ((), ())),
                                preferred_element_type=F32)
        o_ref[:, cols] = o.astype(o_ref.dtype)
        upd = lax.dot_general(vb[:, cols], k_dec[:, cols], (((0,), (0,)), ((), ())),
                              preferred_element_type=F32)
        state_ref[hd] = st * total[:, cols] + upd


def _hgrn_scan_kernel(qf_ref, ff_ref, vf_ref, qb_ref, fb_ref, vb_ref, lbt_ref, tri_ref, up_ref,
                      mask_ref, of_ref, ob_ref, b_ref, state_ref, *, layer):
    @pl.when(pl.program_id(1) == 0)
    def _():
        state_ref[...] = jnp.zeros_like(state_ref)

    for s in range(SCAN_CHUNKS_PER_STEP):
        rows = pl.ds(s * SCAN_CHUNK, SCAN_CHUNK)
        _scan_chunk(qf_ref.at[rows], ff_ref.at[rows], vf_ref.at[rows], lbt_ref[0], tri_ref[0],
                    up_ref.at[0], mask_ref.at[0], of_ref.at[rows], b_ref.at[0], state_ref.at[0],
                    layer=layer, reverse=False)
        rows = pl.ds((SCAN_CHUNKS_PER_STEP - 1 - s) * SCAN_CHUNK, SCAN_CHUNK)
        _scan_chunk(qb_ref.at[rows], fb_ref.at[rows], vb_ref.at[rows], lbt_ref[1], tri_ref[1],
                    up_ref.at[1], mask_ref.at[1], ob_ref.at[rows], b_ref.at[1], state_ref.at[1],
                    layer=layer, reverse=True)


def _hgrn_scan(proj, lb_table, layer, batch, seq_len):
    T = proj.shape[0]
    C = SCAN_CHUNK
    rows_per_step = SCAN_CHUNKS_PER_STEP * C
    n_chunks = seq_len // rows_per_step
    tri, upper, mask = _scan_constants()
    col0 = (3 * D_HYENA) // D_HGRN

    fwd = lambda b, c: b * n_chunks + c
    bwd = lambda b, c: b * n_chunks + n_chunks - 1 - c
    chunk = lambda rows, col: pl.BlockSpec((rows_per_step, D_HGRN), lambda b, c: (rows(b, c), col))
    whole = lambda a: pl.BlockSpec(a.shape, lambda b, c: (0,) * a.ndim)
    out = jax.ShapeDtypeStruct((T, D_HGRN), BF16)
    return pl.pallas_call(
        functools.partial(_hgrn_scan_kernel, layer=layer),
        grid=(batch, n_chunks),
        in_specs=[
            chunk(fwd, col0), chunk(fwd, col0 + 1), chunk(fwd, col0 + 3),
            chunk(bwd, col0), chunk(bwd, col0 + 2), chunk(bwd, col0 + 3),
            whole(lb_table), whole(tri), whole(upper), whole(mask),
        ],
        out_specs=[chunk(fwd, 0), chunk(bwd, 0)],
        out_shape=[out, out],
        scratch_shapes=[pltpu.VMEM((2, C, D_HGRN), F32),
                        pltpu.VMEM((2, HGRN_HEADS, HEAD_DIM, HEAD_DIM), F32)],
        compiler_params=_params("parallel", "arbitrary"),
    )(proj, proj, proj, proj, proj, proj, lb_table, jnp.asarray(tri, BF16), jnp.asarray(upper),
      jnp.asarray(mask))


def _group_mean_matrix(group):
    idx = np.arange(D_HYENA) // group
    return (idx[:, None] == idx[None, :]).astype(np.float32)


def _mix_out_kernel(x_ref, y_ref, z_ref, x0_ref, of_ref, ob_ref, g_ref, skip_ref,
                    hy_gain_ref, hg_gain_ref, grp_hy_ref, grp_hg_ref, w_ref, post_ref, o_ref):
    z = z_ref[...]
    yh = x0_ref[...].astype(F32) * (y_ref[...] + skip_ref[...] * z)
    ms = _group_sums(yh * yh, grp_hy_ref[...]) * (HYENA_GROUPS / D_HYENA)
    yh = yh * lax.rsqrt(ms + EPS) * hy_gain_ref[...]
    o = of_ref[...].astype(F32) + ob_ref[...].astype(F32)
    ms = _group_sums(o * o, grp_hg_ref[...]) * (1.0 / HEAD_DIM)
    g = g_ref[...].astype(F32)
    o = o * lax.rsqrt(ms + EPS) * hg_gain_ref[...] * (g * (1.0 / (1.0 + jnp.exp(-g))))
    mix = (jnp.dot(yh.astype(BF16), w_ref[:D_HYENA, :], preferred_element_type=F32)
           + jnp.dot(o.astype(BF16), w_ref[D_HYENA:, :], preferred_element_type=F32))
    o_ref[...] = x_ref[...] + mix * _rms_scale(mix) * post_ref[...]


def _mix_out(x2d, y, z, x0, o_fwd, o_bwd, proj, skip, hy_gain, hg_gain, w_bf16, post_gain,
             tm=1024):
    T = x2d.shape[0]
    gate_col = D_IN // D_HGRN - 1
    half = lambda: pl.BlockSpec((tm, D_HYENA), lambda i: (i, 0))
    vec = lambda n: pl.BlockSpec((1, n), lambda i: (0, 0))
    sq = lambda: pl.BlockSpec((D_HYENA, D_HYENA), lambda i: (0, 0))
    row = lambda v: v.reshape(1, -1)
    return pl.pallas_call(
        _mix_out_kernel,
        grid=(T // tm,),
        in_specs=[
            pl.BlockSpec((tm, D_MODEL), lambda i: (i, 0)),
            half(), half(), half(), half(), half(),
            pl.BlockSpec((tm, D_HGRN), lambda i: (i, gate_col)),
            vec(D_HYENA), vec(D_HYENA), vec(D_HGRN), sq(), sq(),
            pl.BlockSpec((D_MODEL, D_MODEL), lambda i: (0, 0)),
            vec(D_MODEL),
        ],
        out_specs=pl.BlockSpec((tm, D_MODEL), lambda i: (i, 0)),
        out_shape=jax.ShapeDtypeStruct((T, D_MODEL), F32),
        compiler_params=_params("parallel"),
    )(x2d, y, z, x0, o_fwd, o_bwd, proj, row(skip), row(hy_gain), row(hg_gain),
      jnp.asarray(_group_mean_matrix(D_HYENA // HYENA_GROUPS), BF16),
      jnp.asarray(_group_mean_matrix(HEAD_DIM), BF16), w_bf16, row(post_gain))


def _gelu_tanh(x):
    c = -2.0 * math.sqrt(2.0 / math.pi) * math.log2(math.e)
    return x / (1.0 + jnp.exp2(x * (c + (0.044715 * c) * (x * x))))


def _ffn_kernel(prev_ref, main_ref, next_ref, pre_ref, wa_ref, wb_ref, cwa_ref, cwb_ref,
                cba_ref, cbb_ref, wd_ref, post_ref, o_ref, h_ref, acc_ref, ua_ref, ub_ref,
                *, tiles_per_seq, tm):
    j = pl.program_id(1)
    pos = pl.program_id(0) % tiles_per_seq

    @pl.when(j == 0)
    def _():
        gain = pre_ref[...]

        def normed(x):
            return (x * _rms_scale(x) * gain).astype(BF16)

        prev = normed(prev_ref[...])
        nxt = normed(next_ref[...])
        h_ref[0:FFN_HALO, :] = jnp.where(pos == 0, jnp.zeros_like(prev), prev)
        h_ref[FFN_HALO:FFN_HALO + tm, :] = normed(main_ref[...])
        h_ref[FFN_HALO + tm:, :] = jnp.where(pos == tiles_per_seq - 1, jnp.zeros_like(nxt), nxt)
        acc_ref[...] = jnp.zeros_like(acc_ref)

    h = h_ref[...]

    def conv_part(w_ref_, cw_ref_, cb_ref_, u_ref):
        u_ref[...] = jnp.dot(h, w_ref_[...], preferred_element_type=F32)
        cw = cw_ref_[...]
        down = u_ref[pl.ds(FFN_HALO - 1, tm), :]
        mid = u_ref[pl.ds(FFN_HALO, tm), :]
        up = u_ref[pl.ds(FFN_HALO + 1, tm), :]
        return down * cw[0:1, :] + mid * cw[1:2, :] + up * cw[2:3, :] + cb_ref_[...]

    a = conv_part(wa_ref, cwa_ref, cba_ref, ua_ref)
    b = conv_part(wb_ref, cwb_ref, cbb_ref, ub_ref)
    act = (_gelu_tanh(a) * b).astype(BF16)
    acc_ref[...] += jnp.dot(act, wd_ref[...], preferred_element_type=F32)

    @pl.when(j == pl.num_programs(1) - 1)
    def _():
        ff = acc_ref[...]
        o_ref[...] = main_ref[...] + ff * _rms_scale(ff) * post_ref[...]


def _ffn(x2d, pre_gain, w_up_bf16, conv_w, conv_b, w_down_bf16, post_gain, seq_len,
         tm=1024, tf=1024):
    T = x2d.shape[0]
    halo_per_tile = tm // FFN_HALO
    last_halo = T // FFN_HALO - 1
    n_f = D_FF // tf
    row = lambda v: v.reshape(1, -1)
    return pl.pallas_call(
        functools.partial(_ffn_kernel, tiles_per_seq=seq_len // tm, tm=tm),
        grid=(T // tm, n_f),
        in_specs=[
            pl.BlockSpec((FFN_HALO, D_MODEL),
                         lambda i, j: (jnp.maximum(i * halo_per_tile - 1, 0), 0)),
            pl.BlockSpec((tm, D_MODEL), lambda i, j: (i, 0)),
            pl.BlockSpec((FFN_HALO, D_MODEL),
                         lambda i, j: (jnp.minimum((i + 1) * halo_per_tile, last_halo), 0)),
            pl.BlockSpec((1, D_MODEL), lambda i, j: (0, 0)),
            pl.BlockSpec((D_MODEL, tf), lambda i, j: (0, j)),
            pl.BlockSpec((D_MODEL, tf), lambda i, j: (0, n_f + j)),
            pl.BlockSpec((3, tf), lambda i, j: (0, j)),
            pl.BlockSpec((3, tf), lambda i, j: (0, n_f + j)),
            pl.BlockSpec((1, tf), lambda i, j: (0, j)),
            pl.BlockSpec((1, tf), lambda i, j: (0, n_f + j)),
            pl.BlockSpec((tf, D_MODEL), lambda i, j: (j, 0)),
            pl.BlockSpec((1, D_MODEL), lambda i, j: (0, 0)),
        ],
        out_specs=pl.BlockSpec((tm, D_MODEL), lambda i, j: (i, 0)),
        out_shape=jax.ShapeDtypeStruct((T, D_MODEL), F32),
        scratch_shapes=[pltpu.VMEM((tm + 2 * FFN_HALO, D_MODEL), BF16),
                        pltpu.VMEM((tm, D_MODEL), F32)]
        + [pltpu.VMEM((tm + 2 * FFN_HALO, tf), F32)] * 2,
        compiler_params=_params("parallel", "arbitrary"),
    )(x2d, x2d, x2d, row(pre_gain), w_up_bf16, w_up_bf16, conv_w, conv_w,
      row(conv_b), row(conv_b), w_down_bf16, row(post_gain))


def _trunk(x, p):
    B, L, _ = x.shape
    x2d = x.reshape(B * L, D_MODEL)
    plan = _FftPlan(L)
    for l in range(p["w_in"].shape[0]):
        proj = _in_proj(x2d, p["norm_mix_pre"][l], p["w_in_bf16"][l])
        x0, z = _hyena_pre(proj, p["hyena_conv_w"][l], p["hyena_conv_b"][l], L)
        kern = _hyena_filter(L, p["filt_w1"][l], p["filt_b1"][l], p["filt_w2"][l], p["filt_b2"][l],
                             p["filt_w3"][l], p["filt_b3"][l], p["filt_w4"][l], p["filt_freq"][l])
        y = _fft_long_conv(z.reshape(B, L, D_HYENA), kern, plan).reshape(B * L, D_HYENA)
        o_fwd, o_bwd = _hgrn_scan(proj, p["hgrn_lower_bounds"], l, B, L)
        x2d = _mix_out(x2d, y, z, x0, o_fwd, o_bwd, proj, p["hyena_skip"][l], p["hyena_out_norm"][l],
                       p["hgrn_out_norm"][l], p["w_out_bf16"][l], p["norm_mix_post"][l])
        x2d = _ffn(x2d, p["norm_ffn_pre"][l], p["ffn_w_up_bf16"][l], p["ffn_conv_w"][l],
                   p["ffn_conv_b"][l], p["ffn_w_down_bf16"][l], p["norm_ffn_post"][l], L)
    return x2d.reshape(B, L, D_MODEL)


def kernel(x_prompt, x_sample, norm_mix_pre, norm_mix_post, norm_ffn_pre, norm_ffn_post, w_in, hyena_conv_w, hyena_conv_b, filt_w1, filt_b1, filt_w2, filt_b2, filt_w3, filt_b3, filt_w4, filt_freq, hyena_skip, hyena_out_norm, hgrn_lower_bounds, hgrn_out_norm, w_out, ffn_w_up, ffn_conv_w, ffn_conv_b, ffn_w_down):
    p = dict(
        norm_mix_pre=norm_mix_pre, norm_mix_post=norm_mix_post, norm_ffn_pre=norm_ffn_pre,
        norm_ffn_post=norm_ffn_post, w_in=w_in, hyena_conv_w=hyena_conv_w,
        hyena_conv_b=hyena_conv_b, filt_w1=filt_w1, filt_b1=filt_b1, filt_w2=filt_w2,
        filt_b2=filt_b2, filt_w3=filt_w3, filt_b3=filt_b3, filt_w4=filt_w4, filt_freq=filt_freq,
        hyena_skip=hyena_skip, hyena_out_norm=hyena_out_norm,
        hgrn_lower_bounds=hgrn_lower_bounds, hgrn_out_norm=hgrn_out_norm,
        ffn_conv_w=ffn_conv_w, ffn_conv_b=ffn_conv_b,
        w_in_bf16=w_in.astype(BF16), w_out_bf16=w_out.astype(BF16),
        ffn_w_up_bf16=ffn_w_up.astype(BF16), ffn_w_down_bf16=ffn_w_down.astype(BF16),
    )
    return (_trunk(x_prompt, p), _trunk(x_sample, p))
```

```python
import functools
import math

import numpy as np
import jax
import jax.numpy as jnp
from jax import lax
from jax.experimental import pallas as pl
from jax.experimental.pallas import tpu as pltpu

F32 = jnp.float32
BF16 = jnp.bfloat16
HIGHEST = lax.Precision.HIGHEST

D_MODEL = 1024
D_HYENA = 512
HYENA_GROUPS = 8
D_HGRN = 512
HGRN_HEADS = 4
HEAD_DIM = 128
D_IN = 3 * D_HYENA + 5 * D_HGRN
D_FF = 4 * D_MODEL
FILTER_EMB = 33
FILTER_BANDS = 16
FILTER_HIDDEN = 64
DECAY_TARGET = 1e-2
FAST_DECAY_PCT = 0.3
SLOW_DECAY_PCT = 1.5
EPS = 1e-6

SUBLANES = 8
LANES = 128
BF16_ROWS = 16
VMEM_LIMIT_BYTES = 56 * 1024 * 1024

FFT_FAST = 128
FFT_KS_GROUP_MAX = 20
FFT_F_TILE = 16
FFN_HALO = 16
SCAN_CHUNK = 128
SCAN_CHUNKS_PER_STEP = 4
SCAN_LEVELS = (64, 32, 16, 8, 4, 2, 1)


def _params(*semantics):
    return pltpu.CompilerParams(dimension_semantics=semantics,
                                vmem_limit_bytes=VMEM_LIMIT_BYTES)


def _rms_scale(x):
    return lax.rsqrt(jnp.mean(x * x, axis=-1, keepdims=True) + EPS)


def _group_sums(sq, same_group):
    return jnp.dot(sq.astype(BF16), same_group, preferred_element_type=F32)


def _in_proj_kernel(x_ref, g_ref, w_ref, o_ref, h_ref):
    @pl.when(pl.program_id(1) == 0)
    def _():
        x = x_ref[...]
        h_ref[...] = (x * _rms_scale(x) * g_ref[...]).astype(BF16)

    o_ref[...] = jnp.dot(h_ref[...], w_ref[...], preferred_element_type=F32).astype(o_ref.dtype)


def _in_proj(x2d, gain, w_bf16, tm=1024, tn=2048):
    T = x2d.shape[0]
    N = w_bf16.shape[1]
    return pl.pallas_call(
        _in_proj_kernel,
        grid=(T // tm, N // tn),
        in_specs=[
            pl.BlockSpec((tm, D_MODEL), lambda i, j: (i, 0)),
            pl.BlockSpec((1, D_MODEL), lambda i, j: (0, 0)),
            pl.BlockSpec((D_MODEL, tn), lambda i, j: (0, j)),
        ],
        out_specs=pl.BlockSpec((tm, tn), lambda i, j: (i, j)),
        out_shape=jax.ShapeDtypeStruct((T, N), BF16),
        scratch_shapes=[pltpu.VMEM((tm, D_MODEL), BF16)],
        compiler_params=_params("parallel", "arbitrary"),
    )(x2d, gain.reshape(1, D_MODEL), w_bf16)


def _shift_rows(main, prev_row, next_row):
    tm = main.shape[0]
    rows = lax.broadcasted_iota(jnp.int32, (tm, 1), 0)
    down = jnp.where(rows == 0, prev_row, pltpu.roll(main, 1, 0))
    up = jnp.where(rows == tm - 1, next_row, pltpu.roll(main, tm - 1, 0))
    return down, up


def _hyena_pre_kernel(prev_ref, main_ref, next_ref, w_ref, b_ref, x0_ref, z_ref,
                      *, tiles_per_seq):
    pos = pl.program_id(0) % tiles_per_seq
    main = main_ref[...].astype(F32)
    prev_row = jnp.where(pos == 0, 0.0, prev_ref[...].astype(F32)[BF16_ROWS - 1:BF16_ROWS, :])
    next_row = jnp.where(pos == tiles_per_seq - 1, 0.0, next_ref[...].astype(F32)[0:1, :])
    down, up = _shift_rows(main, prev_row, next_row)
    w = w_ref[...]
    u = down * w[0:1, :] + main * w[1:2, :] + up * w[2:3, :] + b_ref[...]
    x0_ref[...] = u[:, :D_HYENA].astype(x0_ref.dtype)
    z_ref[...] = u[:, D_HYENA:2 * D_HYENA] * u[:, 2 * D_HYENA:]


def _hyena_pre(proj, conv_w, conv_b, seq_len, tm=512):
    T = proj.shape[0]
    width = 3 * D_HYENA
    halo_per_tile = tm // BF16_ROWS
    last_halo = T // BF16_ROWS - 1
    return pl.pallas_call(
        functools.partial(_hyena_pre_kernel, tiles_per_seq=seq_len // tm),
        grid=(T // tm,),
        in_specs=[
            pl.BlockSpec((BF16_ROWS, width),
                         lambda i: (jnp.maximum(i * halo_per_tile - 1, 0), 0)),
            pl.BlockSpec((tm, width), lambda i: (i, 0)),
            pl.BlockSpec((BF16_ROWS, width),
                         lambda i: (jnp.minimum((i + 1) * halo_per_tile, last_halo), 0)),
            pl.BlockSpec((3, width), lambda i: (0, 0)),
            pl.BlockSpec((1, width), lambda i: (0, 0)),
        ],
        out_specs=[pl.BlockSpec((tm, D_HYENA), lambda i: (i, 0)),
                   pl.BlockSpec((tm, D_HYENA), lambda i: (i, 0))],
        out_shape=[jax.ShapeDtypeStruct((T, D_HYENA), BF16),
                   jax.ShapeDtypeStruct((T, D_HYENA), F32)],
        compiler_params=_params("parallel"),
    )(proj, proj, proj, conv_w, conv_b.reshape(1, width))


def _filter_kernel(band_ref, w1t_ref, w1c_ref, w1s_ref, b1_ref, w2_ref, b2_ref, w3_ref, b3_ref,
                   w4_ref, freq_ref, delta_ref, o_ref, *, seq_len, tr):
    row0 = pl.program_id(0) * tr

    def tap_index(shape, axis):
        j = row0 + lax.broadcasted_iota(jnp.int32, shape, axis)
        return j, jnp.where(j < seq_len, j, 2 * seq_len - j).astype(F32)

    _, idx = tap_index((1, tr), 1)
    t = idx * (1.0 / (seq_len - 1))
    arg = band_ref[...] * ((2.0 * math.pi / seq_len) * idx)
    fr = freq_ref[...]

    def dense(w_ref_, a):
        return jnp.dot(w_ref_[...], a, precision=HIGHEST, preferred_element_type=F32)

    h = w1t_ref[...] * t + dense(w1c_ref, jnp.cos(arg)) - dense(w1s_ref, jnp.sin(arg))
    h = jnp.sin(fr * (h + b1_ref[...]))
    h = jnp.sin(fr * (dense(w2_ref, h) + b2_ref[...]))
    h = jnp.sin(fr * (dense(w3_ref, h) + b3_ref[...]))
    out = lax.dot_general(h.astype(BF16), w4_ref[...].astype(BF16), (((0,), (0,)), ((), ())),
                          preferred_element_type=F32)
    j_col, idx_col = tap_index((tr, 1), 0)
    window = jnp.exp(-(idx_col * (1.0 / (seq_len - 1))) * delta_ref[...])
    o_ref[...] = jnp.where(j_col == seq_len, 0.0, out * window)


def _hyena_filter(seq_len, w1, b1, w2, b2, w3, b3, w4, freq, tr=1024):
    n_taps = 2 * seq_len
    bands = np.linspace(1e-4, FILTER_BANDS - 1, FILTER_BANDS, dtype=np.float32).reshape(-1, 1)
    deltas = np.abs(np.linspace(math.log(DECAY_TARGET) / SLOW_DECAY_PCT,
                                math.log(DECAY_TARGET) / FAST_DECAY_PCT, D_HYENA,
                                dtype=np.float32)).reshape(1, D_HYENA)
    tiles_fwd = seq_len // tr
    const = lambda i: (0, 0)
    col = lambda v: v.reshape(-1, 1)
    full = lambda a: pl.BlockSpec(a.shape, const)
    operands = [jnp.asarray(bands), col(w1[0]), w1[1:1 + FILTER_BANDS].T,
                w1[1 + FILTER_BANDS:].T, col(b1), w2.T, col(b2), w3.T, col(b3)]
    return pl.pallas_call(
        functools.partial(_filter_kernel, seq_len=seq_len, tr=tr),
        grid=(n_taps // tr,),
        in_specs=[full(a) for a in operands] + [
            pl.BlockSpec((FILTER_HIDDEN, D_HYENA), lambda i: (0, i // tiles_fwd)),
            pl.BlockSpec((FILTER_HIDDEN, 1), const),
            pl.BlockSpec((1, D_HYENA), const),
        ],
        out_specs=pl.BlockSpec((tr, D_HYENA), lambda i: (i, 0)),
        out_shape=jax.ShapeDtypeStruct((n_taps, D_HYENA), F32),
        compiler_params=_params("parallel"),
    )(*operands, w4, col(freq), jnp.asarray(deltas))


class _FftPlan:
    def __init__(self, seq_len):
        self.n = 2 * seq_len
        self.fast = FFT_FAST
        self.slow = self.n // FFT_FAST
        self.slow_half = self.slow // 2
        self.ks = self.slow_half + 1
        self.ks_pad = -(-self.ks // SUBLANES) * SUBLANES
        self.ks_group = max(g for g in range(1, FFT_KS_GROUP_MAX + 1) if self.ks_pad % g == 0)
        S, F, N = self.slow, self.fast, self.n
        ks = np.arange(self.ks_pad, dtype=np.float64)[:, None]
        valid = (ks < self.ks)

        def stage_a(n_s):
            s = np.arange(n_s, dtype=np.float64)[None, :]
            ang = 2.0 * np.pi * ks * s / S
            return np.concatenate([np.where(valid, np.cos(ang), 0.0),
                                   np.where(valid, -np.sin(ang), 0.0)], axis=0)

        self.a_half = stage_a(self.slow_half).astype(np.float32)
        self.a_full = stage_a(self.slow).astype(np.float32)
        s = np.arange(self.slow_half, dtype=np.float64)[:, None]
        kk = np.arange(self.ks_pad, dtype=np.float64)[None, :]
        weight = np.where((kk == 0) | (kk == self.slow_half), 1.0, 2.0) * (kk < self.ks) / N
        ang = 2.0 * np.pi * s * kk / S
        self.a_inv = np.concatenate([weight * np.cos(ang), -weight * np.sin(ang)],
                                    axis=1).astype(np.float32)
        f = np.arange(F, dtype=np.float64)
        ang = 2.0 * np.pi * np.outer(np.arange(self.ks_pad, dtype=np.float64), f) / N
        self.tw_re = np.cos(ang).reshape(-1, 1).astype(np.float32)
        self.tw_im = (-np.sin(ang)).reshape(-1, 1).astype(np.float32)
        ang = 2.0 * np.pi * np.outer(f, f) / F
        self.f_re = np.cos(ang).astype(np.float32)
        self.f_im = (-np.sin(ang)).astype(np.float32)


def _dft_dot(a_ref, x):
    return jnp.dot(a_ref[...], x.astype(BF16), preferred_element_type=F32)


def _stage_a_kernel(a_ref, x_ref, o_ref, stage_ref, *, batch):
    m, k = a_ref.shape
    x2d = x_ref.reshape(batch * k * FFT_F_TILE, LANES)
    o2d = o_ref.reshape(batch * m * FFT_F_TILE, LANES)
    stage_in = x_ref.dtype != F32
    for b in range(batch):
        x_rows = pl.ds(b * k * FFT_F_TILE, k * FFT_F_TILE)
        o_rows = pl.ds(b * m * FFT_F_TILE, m * FFT_F_TILE)
        if stage_in:
            stage_ref[...] = x2d[x_rows, :].astype(F32)
        for j in range(FFT_F_TILE):
            if stage_in:
                r = _dft_dot(a_ref, stage_ref[pl.ds(j, k, stride=FFT_F_TILE), :])
                o2d[pl.ds(b * m * FFT_F_TILE + j, m, stride=FFT_F_TILE), :] = r
            else:
                r = _dft_dot(a_ref, x2d[pl.ds(b * k * FFT_F_TILE + j, k, stride=FFT_F_TILE), :])
                stage_ref[pl.ds(j, m, stride=FFT_F_TILE), :] = r
        if not stage_in:
            o2d[o_rows, :] = stage_ref[...].astype(o_ref.dtype)


def _stage_a(a, x, inverse):
    F, C = x.shape[-2:]
    B = x.shape[0]
    M, K = a.shape
    x_dims, o_dims = ((2, K // 2), (M,)) if inverse else ((K,), (2, M // 2))

    def spec(dims):
        zeros = (0,) * len(dims)
        return pl.BlockSpec((B,) + dims + (FFT_F_TILE, LANES), lambda f, c: (0,) + zeros + (f, c))

    return pl.pallas_call(
        functools.partial(_stage_a_kernel, batch=B),
        grid=(F // FFT_F_TILE, C // LANES),
        in_specs=[pl.BlockSpec((M, K), lambda f, c: (0, 0)), spec(x_dims)],
        out_specs=spec(o_dims),
        out_shape=jax.ShapeDtypeStruct((B,) + o_dims + (F, C), F32 if inverse else BF16),
        scratch_shapes=[pltpu.VMEM(((K if inverse else M) * FFT_F_TILE, LANES), F32)],
        compiler_params=_params("parallel", "parallel"),
    )(jnp.asarray(a, BF16), x)


def _stage_b_tile(fr_ref, fi_ref, re, im, conj=False):
    both = jnp.concatenate([re, im], axis=-1)
    pr = _dft_dot(fr_ref, both)
    pi = _dft_dot(fi_ref, both)
    w = re.shape[-1]
    if conj:
        return pr[:, :w] + pi[:, w:], pr[:, w:] - pi[:, :w]
    return pr[:, :w] - pi[:, w:], pr[:, w:] + pi[:, :w]


def _fft_fwd_kernel(a_ref, twr_ref, twi_ref, fr_ref, fi_ref, o_ref):
    for g in range(a_ref.shape[2] // FFT_FAST):
        rows = slice(g * FFT_FAST, (g + 1) * FFT_FAST)
        ar, ai = a_ref[0, 0, rows, :].astype(F32), a_ref[0, 1, rows, :].astype(F32)
        tr, ti = twr_ref[rows, :], twi_ref[rows, :]
        xr, xi = _stage_b_tile(fr_ref, fi_ref, ar * tr - ai * ti, ar * ti + ai * tr)
        o_ref[0, 0, rows, :] = xr.astype(o_ref.dtype)
        o_ref[0, 1, rows, :] = xi.astype(o_ref.dtype)


def _fft_conv_kernel(a_ref, k_ref, twr_ref, twi_ref, fr_ref, fi_ref, o_ref):
    for g in range(a_ref.shape[2] // FFT_FAST):
        rows = slice(g * FFT_FAST, (g + 1) * FFT_FAST)
        ar, ai = a_ref[0, 0, rows, :].astype(F32), a_ref[0, 1, rows, :].astype(F32)
        tr, ti = twr_ref[rows, :], twi_ref[rows, :]
        xr, xi = _stage_b_tile(fr_ref, fi_ref, ar * tr - ai * ti, ar * ti + ai * tr)
        kr, ki = k_ref[0, 0, rows, :].astype(F32), k_ref[0, 1, rows, :].astype(F32)
        yr, yi = xr * kr - xi * ki, xr * ki + xi * kr
        br, bi = _stage_b_tile(fr_ref, fi_ref, yr, yi, conj=True)
        o_ref[0, 0, rows, :] = (br * tr + bi * ti).astype(o_ref.dtype)
        o_ref[0, 1, rows, :] = (bi * tr - br * ti).astype(o_ref.dtype)


def _fft_stage_b(plan, a, k=None):
    B, _, R, C = a.shape
    rows = plan.ks_group * FFT_FAST
    data = pl.BlockSpec((1, 2, rows, LANES), lambda r, c, b: (b, 0, r, c))
    tw = pl.BlockSpec((rows, LANES), lambda r, c, b: (r, 0))
    mat = pl.BlockSpec((FFT_FAST, FFT_FAST), lambda r, c, b: (0, 0))
    lane_bcast = lambda col: jnp.broadcast_to(jnp.asarray(col), (col.shape[0], LANES))
    consts = (lane_bcast(plan.tw_re), lane_bcast(plan.tw_im),
              jnp.asarray(plan.f_re, BF16), jnp.asarray(plan.f_im, BF16))
    common = dict(
        grid=(R // rows, C // LANES, B),
        out_specs=data,
        out_shape=jax.ShapeDtypeStruct(a.shape, BF16),
        compiler_params=_params("parallel", "parallel", "arbitrary"),
    )
    if k is None:
        return pl.pallas_call(_fft_fwd_kernel, in_specs=[data, tw, tw, mat, mat],
                              **common)(a, *consts)
    kspec = pl.BlockSpec((1, 2, rows, LANES), lambda r, c, b: (0, 0, r, c))
    return pl.pallas_call(_fft_conv_kernel, in_specs=[data, kspec, tw, tw, mat, mat],
                          **common)(a, k, *consts)


def _fft_long_conv(z, kern, plan):
    B, L, C = z.shape
    S, F, P = plan.slow, plan.fast, plan.ks_pad
    ka = _stage_a(plan.a_full, kern.reshape(1, S, F, C), inverse=False)
    k_spec = _fft_stage_b(plan, ka.reshape(1, 2, P * F, C))
    za = _stage_a(plan.a_half, z.reshape(B, S // 2, F, C), inverse=False)
    ya = _fft_stage_b(plan, za.reshape(B, 2, P * F, C), k_spec)
    y = _stage_a(plan.a_inv, ya.reshape(B, 2, P, F, C), inverse=True)
    return y.reshape(B, L, C)


def _scan_constants():
    C = SCAN_CHUNK
    n_lv = len(SCAN_LEVELS)
    tri = np.tril(np.ones((C, C), np.float32))
    upper = np.zeros((n_lv, C, 1), np.float32)
    mask = np.zeros((n_lv + 1, C, C), np.float32)
    for li, h in enumerate(SCAN_LEVELS):
        for t in range(C):
            r = (t // (2 * h)) * 2 * h + h
            if t % (2 * h) >= h:
                upper[li, t, 0] = 1.0
                mask[li, t, r - h:r] = 1.0
    mask[n_lv] = np.eye(C, dtype=np.float32)
    flip = lambda m: m[:, ::-1, ::-1]
    return (np.stack([tri, tri[::-1, ::-1]]), np.stack([upper, upper[:, ::-1]]),
            np.stack([mask, flip(mask)]))


def _ref_rows(h, reverse):
    rows = []
    for g in range(SCAN_CHUNK // SUBLANES):
        pair = []
        for t in (g * SUBLANES, g * SUBLANES + SUBLANES // 2):
            start = (t // (2 * h)) * 2 * h
            pair.append(start + h if reverse else start + h - 1)
        rows.append(tuple(pair))
    return rows


def _scan_chunk(q_ref, f_ref, v_ref, lb_table, tri, up_ref, mask_ref, o_ref, b_ref, state_ref,
                *, layer, reverse):
    n_lv = len(SCAN_LEVELS)
    C = SCAN_CHUNK

    e = jnp.exp(lb_table - jnp.max(lb_table, axis=0, keepdims=True))
    prob = e / jnp.sum(e, axis=0, keepdims=True)
    lb = jnp.zeros((1, D_HGRN), F32)
    for l in range(1, layer + 1):
        lb = lb + prob[l:l + 1, :]

    q = q_ref[...].astype(F32)
    q = q / (1.0 + jnp.exp(-q))
    z = f_ref[...].astype(F32)
    w = jnp.exp(-jnp.abs(z))
    inv = 1.0 / (1.0 + w)
    log_sig = jnp.minimum(z, 0.0) + jnp.log(inv)
    key = (1.0 - lb) * (jnp.where(z >= 0.0, w, 1.0) * inv)
    log_a = jnp.log(lb)
    log_b = jnp.log1p(-lb) + log_sig
    gate = jnp.maximum(log_a, log_b) + jnp.log(1.0 + jnp.exp(-jnp.abs(log_a - log_b)))
    vb = v_ref[...]

    gate2 = gate * math.log2(math.e)
    g1 = gate2.astype(BF16)
    rem = gate2 - g1.astype(F32)
    g2 = rem.astype(BF16)
    g3 = (rem - g2.astype(F32)).astype(BF16)
    b_inc = (jnp.dot(tri, g1, preferred_element_type=F32)
             + jnp.dot(tri, g2, preferred_element_type=F32)
             + jnp.dot(tri, g3, preferred_element_type=F32))
    b_ref[...] = b_inc

    def bcast_row(r):
        return jnp.broadcast_to(b_ref[pl.ds(r, 1), :], (SUBLANES, D_HGRN))

    first_half = lax.broadcasted_iota(jnp.int32, (SUBLANES, 1), 0) < SUBLANES // 2
    scores = [None] * HGRN_HEADS
    for li, h in enumerate(SCAN_LEVELS):
        is_query = up_ref[li] > 0.5
        if h == 1:
            x = jnp.where(is_query, q * (1.0 - key), key)
        else:
            groups = []
            for r0, r1 in _ref_rows(h, reverse):
                ref = bcast_row(r0)
                groups.append(ref if r1 == r0 else jnp.where(first_half, ref, bcast_row(r1)))
            diff = pltpu.bitcast(b_inc - jnp.concatenate(groups, axis=0), jnp.int32)
            neg_dist = pltpu.bitcast(diff | jnp.int32(-2 ** 31), F32)
            x = jnp.where(is_query, q, key) * jnp.exp2(neg_dist)
        xb = x.astype(BF16)
        keep = mask_ref[li] > 0.5
        for hd in range(HGRN_HEADS):
            xh = xb[:, hd * HEAD_DIM:(hd + 1) * HEAD_DIM]
            p = lax.dot_general(xh, xh, (((1,), (1,)), ((), ())), preferred_element_type=F32)
            scores[hd] = jnp.where(keep, p, 0.0 if scores[hd] is None else scores[hd])

    b_total = bcast_row(0 if reverse else C - 1)
    b_rest = jnp.concatenate([b_total] * (C // SUBLANES), axis=0) - b_inc
    q_dec = (q * jnp.exp2(b_inc)).astype(BF16)
    k_dec = (key * jnp.exp2(b_rest)).astype(BF16)
    total = jnp.exp2(b_total[0:1, :])
    qb, kb = q.astype(BF16), key.astype(BF16)
    on_diag = mask_ref[n_lv] > 0.5
    for hd in range(HGRN_HEADS):
        cols = slice(hd * HEAD_DIM, (hd + 1) * HEAD_DIM)
        diag = lax.dot_general(qb[:, cols], kb[:, cols], (((1,), (1,)), ((), ())),
                               preferred_element_type=F32)
        p = jnp.where(on_diag, diag, scores[hd]).astype(BF16)
        st = state_ref[hd]
        o = jnp.dot(p, vb[:, cols], preferred_element_type=F32)
        o = o + lax.dot_general(q_dec[:, cols], st.astype(BF16), (((1,), (1,)), ((), ())),
                                preferred_element_type=F32)
        o_ref[:, cols] = o.astype(o_ref.dtype)
        upd = lax.dot_general(vb[:, cols], k_dec[:, cols], (((0,), (0,)), ((), ())),
                              preferred_element_type=F32)
        state_ref[hd] = st * total[:, cols] + upd


def _hgrn_scan_kernel(qf_ref, ff_ref, vf_ref, qb_ref, fb_ref, vb_ref, lbt_ref, tri_ref, up_ref,
                      mask_ref, of_ref, ob_ref, b_ref, state_ref, *, layer):
    @pl.when(pl.program_id(1) == 0)
    def _():
        state_ref[...] = jnp.zeros_like(state_ref)

    for s in range(SCAN_CHUNKS_PER_STEP):
        rows = pl.ds(s * SCAN_CHUNK, SCAN_CHUNK)
        _scan_chunk(qf_ref.at[rows], ff_ref.at[rows], vf_ref.at[rows], lbt_ref[0], tri_ref[0],
                    up_ref.at[0], mask_ref.at[0], of_ref.at[rows], b_ref.at[0], state_ref.at[0],
                    layer=layer, reverse=False)
        rows = pl.ds((SCAN_CHUNKS_PER_STEP - 1 - s) * SCAN_CHUNK, SCAN_CHUNK)
        _scan_chunk(qb_ref.at[rows], fb_ref.at[rows], vb_ref.at[rows], lbt_ref[1], tri_ref[1],
                    up_ref.at[1], mask_ref.at[1], ob_ref.at[rows], b_ref.at[1], state_ref.at[1],
                    layer=layer, reverse=True)


def _hgrn_scan(proj, lb_table, layer, batch, seq_len):
    T = proj.shape[0]
    C = SCAN_CHUNK
    rows_per_step = SCAN_CHUNKS_PER_STEP * C
    n_chunks = seq_len // rows_per_step
    tri, upper, mask = _scan_constants()
    col0 = (3 * D_HYENA) // D_HGRN

    fwd = lambda b, c: b * n_chunks + c
    bwd = lambda b, c: b * n_chunks + n_chunks - 1 - c
    chunk = lambda rows, col: pl.BlockSpec((rows_per_step, D_HGRN), lambda b, c: (rows(b, c), col))
    whole = lambda a: pl.BlockSpec(a.shape, lambda b, c: (0,) * a.ndim)
    out = jax.ShapeDtypeStruct((T, D_HGRN), BF16)
    return pl.pallas_call(
        functools.partial(_hgrn_scan_kernel, layer=layer),
        grid=(batch, n_chunks),
        in_specs=[
            chunk(fwd, col0), chunk(fwd, col0 + 1), chunk(fwd, col0 + 3),
            chunk(bwd, col0), chunk(bwd, col0 + 2), chunk(bwd, col0 + 3),
            whole(lb_table), whole(tri), whole(upper), whole(mask),
        ],
        out_specs=[chunk(fwd, 0), chunk(bwd, 0)],
        out_shape=[out, out],
        scratch_shapes=[pltpu.VMEM((2, C, D_HGRN), F32),
                        pltpu.VMEM((2, HGRN_HEADS, HEAD_DIM, HEAD_DIM), F32)],
        compiler_params=_params("parallel", "arbitrary"),
    )(proj, proj, proj, proj, proj, proj, lb_table, jnp.asarray(tri, BF16), jnp.asarray(upper),
      jnp.asarray(mask))


def _group_mean_matrix(group):
    idx = np.arange(D_HYENA) // group
    return (idx[:, None] == idx[None, :]).astype(np.float32)


def _mix_out_kernel(x_ref, y_ref, z_ref, x0_ref, of_ref, ob_ref, g_ref, skip_ref,
                    hy_gain_ref, hg_gain_ref, grp_hy_ref, grp_hg_ref, w_ref, post_ref, o_ref):
    z = z_ref[...]
    yh = x0_ref[...].astype(F32) * (y_ref[...] + skip_ref[...] * z)
    ms = _group_sums(yh * yh, grp_hy_ref[...]) * (HYENA_GROUPS / D_HYENA)
    yh = yh * lax.rsqrt(ms + EPS) * hy_gain_ref[...]
    o = of_ref[...].astype(F32) + ob_ref[...].astype(F32)
    ms = _group_sums(o * o, grp_hg_ref[...]) * (1.0 / HEAD_DIM)
    g = g_ref[...].astype(F32)
    o = o * lax.rsqrt(ms + EPS) * hg_gain_ref[...] * (g * (1.0 / (1.0 + jnp.exp(-g))))
    mix = (jnp.dot(yh.astype(BF16), w_ref[:D_HYENA, :], preferred_element_type=F32)
           + jnp.dot(o.astype(BF16), w_ref[D_HYENA:, :], preferred_element_type=F32))
    o_ref[...] = x_ref[...] + mix * _rms_scale(mix) * post_ref[...]


def _mix_out(x2d, y, z, x0, o_fwd, o_bwd, proj, skip, hy_gain, hg_gain, w_bf16, post_gain,
             tm=1024):
    T = x2d.shape[0]
    gate_col = D_IN // D_HGRN - 1
    half = lambda: pl.BlockSpec((tm, D_HYENA), lambda i: (i, 0))
    vec = lambda n: pl.BlockSpec((1, n), lambda i: (0, 0))
    sq = lambda: pl.BlockSpec((D_HYENA, D_HYENA), lambda i: (0, 0))
    row = lambda v: v.reshape(1, -1)
    return pl.pallas_call(
        _mix_out_kernel,
        grid=(T // tm,),
        in_specs=[
            pl.BlockSpec((tm, D_MODEL), lambda i: (i, 0)),
            half(), half(), half(), half(), half(),
            pl.BlockSpec((tm, D_HGRN), lambda i: (i, gate_col)),
            vec(D_HYENA), vec(D_HYENA), vec(D_HGRN), sq(), sq(),
            pl.BlockSpec((D_MODEL, D_MODEL), lambda i: (0, 0)),
            vec(D_MODEL),
        ],
        out_specs=pl.BlockSpec((tm, D_MODEL), lambda i: (i, 0)),
        out_shape=jax.ShapeDtypeStruct((T, D_MODEL), F32),
        compiler_params=_params("parallel"),
    )(x2d, y, z, x0, o_fwd, o_bwd, proj, row(skip), row(hy_gain), row(hg_gain),
      jnp.asarray(_group_mean_matrix(D_HYENA // HYENA_GROUPS), BF16),
      jnp.asarray(_group_mean_matrix(HEAD_DIM), BF16), w_bf16, row(post_gain))


def _gelu_tanh(x):
    c = -2.0 * math.sqrt(2.0 / math.pi) * math.log2(math.e)
    return x / (1.0 + jnp.exp2(x * (c + (0.044715 * c) * (x * x))))


def _ffn_kernel(prev_ref, main_ref, next_ref, pre_ref, wa_ref, wb_ref, cwa_ref, cwb_ref,
                cba_ref, cbb_ref, wd_ref, post_ref, o_ref, h_ref, acc_ref, ua_ref, ub_ref,
                *, tiles_per_seq, tm):
    j = pl.program_id(1)
    pos = pl.program_id(0) % tiles_per_seq

    @pl.when(j == 0)
    def _():
        gain = pre_ref[...]

        def normed(x):
            return (x * _rms_scale(x) * gain).astype(BF16)

        prev = normed(prev_ref[...])
        nxt = normed(next_ref[...])
        h_ref[0:FFN_HALO, :] = jnp.where(pos == 0, jnp.zeros_like(prev), prev)
        h_ref[FFN_HALO:FFN_HALO + tm, :] = normed(main_ref[...])
        h_ref[FFN_HALO + tm:, :] = jnp.where(pos == tiles_per_seq - 1, jnp.zeros_like(nxt), nxt)
        acc_ref[...] = jnp.zeros_like(acc_ref)

    h = h_ref[...]

    def conv_part(w_ref_, cw_ref_, cb_ref_, u_ref):
        u_ref[...] = jnp.dot(h, w_ref_[...], preferred_element_type=F32)
        cw = cw_ref_[...]
        down = u_ref[pl.ds(FFN_HALO - 1, tm), :]
        mid = u_ref[pl.ds(FFN_HALO, tm), :]
        up = u_ref[pl.ds(FFN_HALO + 1, tm), :]
        return down * cw[0:1, :] + mid * cw[1:2, :] + up * cw[2:3, :] + cb_ref_[...]

    a = conv_part(wa_ref, cwa_ref, cba_ref, ua_ref)
    b = conv_part(wb_ref, cwb_ref, cbb_ref, ub_ref)
    act = (_gelu_tanh(a) * b).astype(BF16)
    acc_ref[...] += jnp.dot(act, wd_ref[...], preferred_element_type=F32)

    @pl.when(j == pl.num_programs(1) - 1)
    def _():
        ff = acc_ref[...]
        o_ref[...] = main_ref[...] + ff * _rms_scale(ff) * post_ref[...]


def _ffn(x2d, pre_gain, w_up_bf16, conv_w, conv_b, w_down_bf16, post_gain, seq_len,
         tm=1024, tf=1024):
    T = x2d.shape[0]
    halo_per_tile = tm // FFN_HALO
    last_halo = T // FFN_HALO - 1
    n_f = D_FF // tf
    row = lambda v: v.reshape(1, -1)
    return pl.pallas_call(
        functools.partial(_ffn_kernel, tiles_per_seq=seq_len // tm, tm=tm),
        grid=(T // tm, n_f),
        in_specs=[
            pl.BlockSpec((FFN_HALO, D_MODEL),
                         lambda i, j: (jnp.maximum(i * halo_per_tile - 1, 0), 0)),
            pl.BlockSpec((tm, D_MODEL), lambda i, j: (i, 0)),
            pl.BlockSpec((FFN_HALO, D_MODEL),
                         lambda i, j: (jnp.minimum((i + 1) * halo_per_tile, last_halo), 0)),
            pl.BlockSpec((1, D_MODEL), lambda i, j: (0, 0)),
            pl.BlockSpec((D_MODEL, tf), lambda i, j: (0, j)),
            pl.BlockSpec((D_MODEL, tf), lambda i, j: (0, n_f + j)),
            pl.BlockSpec((3, tf), lambda i, j: (0, j)),
            pl.BlockSpec((3, tf), lambda i, j: (0, n_f + j)),
            pl.BlockSpec((1, tf), lambda i, j: (0, j)),
            pl.BlockSpec((1, tf), lambda i, j: (0, n_f + j)),
            pl.BlockSpec((tf, D_MODEL), lambda i, j: (j, 0)),
            pl.BlockSpec((1, D_MODEL), lambda i, j: (0, 0)),
        ],
        out_specs=pl.BlockSpec((tm, D_MODEL), lambda i, j: (i, 0)),
        out_shape=jax.ShapeDtypeStruct((T, D_MODEL), F32),
        scratch_shapes=[pltpu.VMEM((tm + 2 * FFN_HALO, D_MODEL), BF16),
                        pltpu.VMEM((tm, D_MODEL), F32)]
        + [pltpu.VMEM((tm + 2 * FFN_HALO, tf), F32)] * 2,
        compiler_params=_params("parallel", "arbitrary"),
    )(x2d, x2d, x2d, row(pre_gain), w_up_bf16, w_up_bf16, conv_w, conv_w,
      row(conv_b), row(conv_b), w_down_bf16, row(post_gain))


def _trunk(x, p):
    B, L, _ = x.shape
    x2d = x.reshape(B * L, D_MODEL)
    plan = _FftPlan(L)
    for l in range(p["w_in"].shape[0]):
        proj = _in_proj(x2d, p["norm_mix_pre"][l], p["w_in_bf16"][l])
        x0, z = _hyena_pre(proj, p["hyena_conv_w"][l], p["hyena_conv_b"][l], L)
        kern = _hyena_filter(L, p["filt_w1"][l], p["filt_b1"][l], p["filt_w2"][l], p["filt_b2"][l],
                             p["filt_w3"][l], p["filt_b3"][l], p["filt_w4"][l], p["filt_freq"][l])
        y = _fft_long_conv(z.reshape(B, L, D_HYENA), kern, plan).reshape(B * L, D_HYENA)
        o_fwd, o_bwd = _hgrn_scan(proj, p["hgrn_lower_bounds"], l, B, L)
        x2d = _mix_out(x2d, y, z, x0, o_fwd, o_bwd, proj, p["hyena_skip"][l], p["hyena_out_norm"][l],
                       p["hgrn_out_norm"][l], p["w_out_bf16"][l], p["norm_mix_post"][l])
        x2d = _ffn(x2d, p["norm_ffn_pre"][l], p["ffn_w_up_bf16"][l], p["ffn_conv_w"][l],
                   p["ffn_conv_b"][l], p["ffn_w_down_bf16"][l], p["norm_ffn_post"][l], L)
    return x2d.reshape(B, L, D_MODEL)


def kernel(x_prompt, x_sample, norm_mix_pre, norm_mix_post, norm_ffn_pre, norm_ffn_post, w_in, hyena_conv_w, hyena_conv_b, filt_w1, filt_b1, filt_w2, filt_b2, filt_w3, filt_b3, filt_w4, filt_freq, hyena_skip, hyena_out_norm, hgrn_lower_bounds, hgrn_out_norm, w_out, ffn_w_up, ffn_conv_w, ffn_conv_b, ffn_w_down):
    p = dict(
        norm_mix_pre=norm_mix_pre, norm_mix_post=norm_mix_post, norm_ffn_pre=norm_ffn_pre,
        norm_ffn_post=norm_ffn_post, w_in=w_in, hyena_conv_w=hyena_conv_w,
        hyena_conv_b=hyena_conv_b, filt_w1=filt_w1, filt_b1=filt_b1, filt_w2=filt_w2,
        filt_b2=filt_b2, filt_w3=filt_w3, filt_b3=filt_b3, filt_w4=filt_w4, filt_freq=filt_freq,
        hyena_skip=hyena_skip, hyena_out_norm=hyena_out_norm,
        hgrn_lower_bounds=hgrn_lower_bounds, hgrn_out_norm=hgrn_out_norm,
        ffn_conv_w=ffn_conv_w, ffn_conv_b=ffn_conv_b,
        w_in_bf16=w_in.astype(BF16), w_out_bf16=w_out.astype(BF16),
        ffn_w_up_bf16=ffn_w_up.astype(BF16), ffn_w_down_bf16=ffn_w_down.astype(BF16),
    )
    return (_trunk(x_prompt, p), _trunk(x_sample, p))
```

```python
import functools
import math

import numpy as np
import jax
import jax.numpy as jnp
from jax import lax
from jax.experimental import pallas as pl
from jax.experimental.pallas import tpu as pltpu

F32 = jnp.float32
BF16 = jnp.bfloat16
HIGHEST = lax.Precision.HIGHEST

D_MODEL = 1024
D_HYENA = 512
HYENA_GROUPS = 8
D_HGRN = 512
HGRN_HEADS = 4
HEAD_DIM = 128
D_IN = 3 * D_HYENA + 5 * D_HGRN
D_FF = 4 * D_MODEL
FILTER_EMB = 33
FILTER_BANDS = 16
FILTER_HIDDEN = 64
DECAY_TARGET = 1e-2
FAST_DECAY_PCT = 0.3
SLOW_DECAY_PCT = 1.5
EPS = 1e-6

SUBLANES = 8
LANES = 128
BF16_ROWS = 16
VMEM_LIMIT_BYTES = 56 * 1024 * 1024

FFT_FAST = 128
FFT_KS_GROUP_MAX = 20
FFT_F_TILE = 16
FFN_HALO = 16
SCAN_CHUNK = 128
SCAN_CHUNKS_PER_STEP = 4
SCAN_LEVELS = (64, 32, 16, 8, 4, 2, 1)


def _params(*semantics):
    return pltpu.CompilerParams(dimension_semantics=semantics,
                                vmem_limit_bytes=VMEM_LIMIT_BYTES)


def _rms_scale(x):
    return lax.rsqrt(jnp.mean(x * x, axis=-1, keepdims=True) + EPS)


def _group_sums(sq, same_group):
    return jnp.dot(sq.astype(BF16), same_group, preferred_element_type=F32)


def _in_proj_kernel(x_ref, g_ref, w_ref, o_ref, h_ref):
    @pl.when(pl.program_id(1) == 0)
    def _():
        x = x_ref[...]
        h_ref[...] = (x * _rms_scale(x) * g_ref[...]).astype(BF16)

    o_ref[...] = jnp.dot(h_ref[...], w_ref[...], preferred_element_type=F32).astype(o_ref.dtype)


def _in_proj(x2d, gain, w_bf16, layer, tm=1024, tn=2048):
    T = x2d.shape[0]
    N = w_bf16.shape[2]
    return pl.pallas_call(
        _in_proj_kernel,
        grid=(T // tm, N // tn),
        in_specs=[
            pl.BlockSpec((tm, D_MODEL), lambda i, j: (i, 0)),
            pl.BlockSpec((1, D_MODEL), lambda i, j: (0, 0)),
            pl.BlockSpec((None, D_MODEL, tn), lambda i, j: (layer, 0, j)),
        ],
        out_specs=pl.BlockSpec((tm, tn), lambda i, j: (i, j)),
        out_shape=jax.ShapeDtypeStruct((T, N), BF16),
        scratch_shapes=[pltpu.VMEM((tm, D_MODEL), BF16)],
        compiler_params=_params("parallel", "arbitrary"),
    )(x2d, gain.reshape(1, D_MODEL), w_bf16)


def _shift_rows(main, prev_row, next_row):
    tm = main.shape[0]
    rows = lax.broadcasted_iota(jnp.int32, (tm, 1), 0)
    down = jnp.where(rows == 0, prev_row, pltpu.roll(main, 1, 0))
    up = jnp.where(rows == tm - 1, next_row, pltpu.roll(main, tm - 1, 0))
    return down, up


def _hyena_pre_kernel(prev_ref, main_ref, next_ref, w_ref, b_ref, x0_ref, z_ref,
                      *, tiles_per_seq):
    pos = pl.program_id(0) % tiles_per_seq
    main = main_ref[...].astype(F32)
    prev_row = jnp.where(pos == 0, 0.0, prev_ref[...].astype(F32)[BF16_ROWS - 1:BF16_ROWS, :])
    next_row = jnp.where(pos == tiles_per_seq - 1, 0.0, next_ref[...].astype(F32)[0:1, :])
    down, up = _shift_rows(main, prev_row, next_row)
    w = w_ref[...]
    u = down * w[0:1, :] + main * w[1:2, :] + up * w[2:3, :] + b_ref[...]
    x0_ref[...] = u[:, :D_HYENA].astype(x0_ref.dtype)
    z_ref[...] = u[:, D_HYENA:2 * D_HYENA] * u[:, 2 * D_HYENA:]


def _hyena_pre(proj, conv_w, conv_b, seq_len, tm=512):
    T = proj.shape[0]
    width = 3 * D_HYENA
    halo_per_tile = tm // BF16_ROWS
    last_halo = T // BF16_ROWS - 1
    return pl.pallas_call(
        functools.partial(_hyena_pre_kernel, tiles_per_seq=seq_len // tm),
        grid=(T // tm,),
        in_specs=[
            pl.BlockSpec((BF16_ROWS, width),
                         lambda i: (jnp.maximum(i * halo_per_tile - 1, 0), 0)),
            pl.BlockSpec((tm, width), lambda i: (i, 0)),
            pl.BlockSpec((BF16_ROWS, width),
                         lambda i: (jnp.minimum((i + 1) * halo_per_tile, last_halo), 0)),
            pl.BlockSpec((3, width), lambda i: (0, 0)),
            pl.BlockSpec((1, width), lambda i: (0, 0)),
        ],
        out_specs=[pl.BlockSpec((tm, D_HYENA), lambda i: (i, 0)),
                   pl.BlockSpec((tm, D_HYENA), lambda i: (i, 0))],
        out_shape=[jax.ShapeDtypeStruct((T, D_HYENA), BF16),
                   jax.ShapeDtypeStruct((T, D_HYENA), F32)],
        compiler_params=_params("parallel"),
    )(proj, proj, proj, conv_w, conv_b.reshape(1, width))


def _filter_kernel(band_ref, w1t_ref, w1c_ref, w1s_ref, b1_ref, w2_ref, b2_ref, w3_ref, b3_ref,
                   w4_ref, freq_ref, delta_ref, o_ref, *, seq_len, tr):
    row0 = pl.program_id(0) * tr

    def tap_index(shape, axis):
        j = row0 + lax.broadcasted_iota(jnp.int32, shape, axis)
        return j, jnp.where(j < seq_len, j, 2 * seq_len - j).astype(F32)

    _, idx = tap_index((1, tr), 1)
    t = idx * (1.0 / (seq_len - 1))
    arg = band_ref[...] * ((2.0 * math.pi / seq_len) * idx)
    fr = freq_ref[...]

    def dense(w_ref_, a):
        return jnp.dot(w_ref_[...], a, precision=HIGHEST, preferred_element_type=F32)

    h = w1t_ref[...] * t + dense(w1c_ref, jnp.cos(arg)) - dense(w1s_ref, jnp.sin(arg))
    h = jnp.sin(fr * (h + b1_ref[...]))
    h = jnp.sin(fr * (dense(w2_ref, h) + b2_ref[...]))
    h = jnp.sin(fr * (dense(w3_ref, h) + b3_ref[...]))
    out = lax.dot_general(h.astype(BF16), w4_ref[...].astype(BF16), (((0,), (0,)), ((), ())),
                          preferred_element_type=F32)
    j_col, idx_col = tap_index((tr, 1), 0)
    window = jnp.exp(-(idx_col * (1.0 / (seq_len - 1))) * delta_ref[...])
    o_ref[...] = jnp.where(j_col == seq_len, 0.0, out * window)


def _hyena_filter(seq_len, w1, b1, w2, b2, w3, b3, w4, freq, tr=1024):
    n_taps = 2 * seq_len
    bands = np.linspace(1e-4, FILTER_BANDS - 1, FILTER_BANDS, dtype=np.float32).reshape(-1, 1)
    deltas = np.abs(np.linspace(math.log(DECAY_TARGET) / SLOW_DECAY_PCT,
                                math.log(DECAY_TARGET) / FAST_DECAY_PCT, D_HYENA,
                                dtype=np.float32)).reshape(1, D_HYENA)
    tiles_fwd = seq_len // tr
    const = lambda i: (0, 0)
    col = lambda v: v.reshape(-1, 1)
    full = lambda a: pl.BlockSpec(a.shape, const)
    operands = [jnp.asarray(bands), col(w1[0]), w1[1:1 + FILTER_BANDS].T,
                w1[1 + FILTER_BANDS:].T, col(b1), w2.T, col(b2), w3.T, col(b3)]
    return pl.pallas_call(
        functools.partial(_filter_kernel, seq_len=seq_len, tr=tr),
        grid=(n_taps // tr,),
        in_specs=[full(a) for a in operands] + [
            pl.BlockSpec((FILTER_HIDDEN, D_HYENA), lambda i: (0, i // tiles_fwd)),
            pl.BlockSpec((FILTER_HIDDEN, 1), const),
            pl.BlockSpec((1, D_HYENA), const),
        ],
        out_specs=pl.BlockSpec((tr, D_HYENA), lambda i: (i, 0)),
        out_shape=jax.ShapeDtypeStruct((n_taps, D_HYENA), F32),
        compiler_params=_params("parallel"),
    )(*operands, w4, col(freq), jnp.asarray(deltas))


class _FftPlan:
    def __init__(self, seq_len):
        self.n = 2 * seq_len
        self.fast = FFT_FAST
        self.slow = self.n // FFT_FAST
        self.slow_half = self.slow // 2
        self.ks = self.slow_half + 1
        self.ks_pad = -(-self.ks // SUBLANES) * SUBLANES
        self.ks_group = max(g for g in range(1, FFT_KS_GROUP_MAX + 1) if self.ks_pad % g == 0)
        S, F, N = self.slow, self.fast, self.n
        ks = np.arange(self.ks_pad, dtype=np.float64)[:, None]
        valid = (ks < self.ks)

        def stage_a(n_s):
            s = np.arange(n_s, dtype=np.float64)[None, :]
            ang = 2.0 * np.pi * ks * s / S
            return np.concatenate([np.where(valid, np.cos(ang), 0.0),
                                   np.where(valid, -np.sin(ang), 0.0)], axis=0)

        self.a_half = stage_a(self.slow_half).astype(np.float32)
        self.a_full = stage_a(self.slow).astype(np.float32)
        s = np.arange(self.slow_half, dtype=np.float64)[:, None]
        kk = np.arange(self.ks_pad, dtype=np.float64)[None, :]
        weight = np.where((kk == 0) | (kk == self.slow_half), 1.0, 2.0) * (kk < self.ks) / N
        ang = 2.0 * np.pi * s * kk / S
        self.a_inv = np.concatenate([weight * np.cos(ang), -weight * np.sin(ang)],
                                    axis=1).astype(np.float32)
        f = np.arange(F, dtype=np.float64)
        ang = 2.0 * np.pi * np.outer(np.arange(self.ks_pad, dtype=np.float64), f) / N
        self.tw_re = np.cos(ang).reshape(-1, 1).astype(np.float32)
        self.tw_im = (-np.sin(ang)).reshape(-1, 1).astype(np.float32)
        ang = 2.0 * np.pi * np.outer(f, f) / F
        self.f_re = np.cos(ang).astype(np.float32)
        self.f_im = (-np.sin(ang)).astype(np.float32)


def _dft_dot(a_ref, x):
    return jnp.dot(a_ref[...], x.astype(BF16), preferred_element_type=F32)


def _stage_a_kernel(a_ref, x_ref, o_ref, stage_ref, *, batch):
    m, k = a_ref.shape
    x2d = x_ref.reshape(batch * k * FFT_F_TILE, LANES)
    o2d = o_ref.reshape(batch * m * FFT_F_TILE, LANES)
    stage_in = x_ref.dtype != F32
    for b in range(batch):
        x_rows = pl.ds(b * k * FFT_F_TILE, k * FFT_F_TILE)
        o_rows = pl.ds(b * m * FFT_F_TILE, m * FFT_F_TILE)
        if stage_in:
            stage_ref[...] = x2d[x_rows, :].astype(F32)
            slabs = [stage_ref[pl.ds(j, k, stride=FFT_F_TILE), :] for j in range(FFT_F_TILE)]
        else:
            slabs = [x2d[pl.ds(b * k * FFT_F_TILE + j, k, stride=FFT_F_TILE), :]
                     for j in range(FFT_F_TILE)]
        r = _dft_dot(a_ref, jnp.concatenate(slabs, axis=-1))
        for j in range(FFT_F_TILE):
            r_j = r[:, j * LANES:(j + 1) * LANES]
            if stage_in:
                o2d[pl.ds(b * m * FFT_F_TILE + j, m, stride=FFT_F_TILE), :] = r_j
            else:
                stage_ref[pl.ds(j, m, stride=FFT_F_TILE), :] = r_j
        if not stage_in:
            o2d[o_rows, :] = stage_ref[...].astype(o_ref.dtype)


def _stage_a(a, x, inverse):
    F, C = x.shape[-2:]
    B = x.shape[0]
    M, K = a.shape
    x_dims, o_dims = ((2, K // 2), (M,)) if inverse else ((K,), (2, M // 2))

    def spec(dims):
        zeros = (0,) * len(dims)
        return pl.BlockSpec((B,) + dims + (FFT_F_TILE, LANES), lambda f, c: (0,) + zeros + (f, c))

    return pl.pallas_call(
        functools.partial(_stage_a_kernel, batch=B),
        grid=(F // FFT_F_TILE, C // LANES),
        in_specs=[pl.BlockSpec((M, K), lambda f, c: (0, 0)), spec(x_dims)],
        out_specs=spec(o_dims),
        out_shape=jax.ShapeDtypeStruct((B,) + o_dims + (F, C), F32 if inverse else BF16),
        scratch_shapes=[pltpu.VMEM(((K if inverse else M) * FFT_F_TILE, LANES), F32)],
        compiler_params=_params("parallel", "parallel"),
    )(jnp.asarray(a, BF16), x)


def _stage_b_dft(fr_ref, fi_ref, tiles, conj=False):
    w = tiles[0][0].shape[-1]
    both = jnp.concatenate([part for tile in tiles for part in tile], axis=-1)
    pr = _dft_dot(fr_ref, both)
    pi = _dft_dot(fi_ref, both)
    out = []
    for g in range(len(tiles)):
        re, im = slice(2 * g * w, (2 * g + 1) * w), slice((2 * g + 1) * w, (2 * g + 2) * w)
        if conj:
            out.append((pr[:, re] + pi[:, im], pr[:, im] - pi[:, re]))
        else:
            out.append((pr[:, re] - pi[:, im], pr[:, im] + pi[:, re]))
    return out


def _twiddled_tiles(a_ref, twr_ref, twi_ref):
    tiles, twiddles = [], []
    for g in range(a_ref.shape[2] // FFT_FAST):
        rows = slice(g * FFT_FAST, (g + 1) * FFT_FAST)
        ar, ai = a_ref[0, 0, rows, :].astype(F32), a_ref[0, 1, rows, :].astype(F32)
        tr, ti = twr_ref[rows, :], twi_ref[rows, :]
        tiles.append((ar * tr - ai * ti, ar * ti + ai * tr))
        twiddles.append((rows, tr, ti))
    return tiles, twiddles


def _fft_fwd_kernel(a_ref, twr_ref, twi_ref, fr_ref, fi_ref, o_ref):
    tiles, twiddles = _twiddled_tiles(a_ref, twr_ref, twi_ref)
    for (rows, _, _), (xr, xi) in zip(twiddles, _stage_b_dft(fr_ref, fi_ref, tiles)):
        o_ref[0, 0, rows, :] = xr.astype(o_ref.dtype)
        o_ref[0, 1, rows, :] = xi.astype(o_ref.dtype)


def _fft_conv_kernel(a_ref, k_ref, twr_ref, twi_ref, fr_ref, fi_ref, o_ref):
    tiles, twiddles = _twiddled_tiles(a_ref, twr_ref, twi_ref)
    products = []
    for (rows, _, _), (xr, xi) in zip(twiddles, _stage_b_dft(fr_ref, fi_ref, tiles)):
        kr, ki = k_ref[0, 0, rows, :].astype(F32), k_ref[0, 1, rows, :].astype(F32)
        products.append((xr * kr - xi * ki, xr * ki + xi * kr))
    inverse = _stage_b_dft(fr_ref, fi_ref, products, conj=True)
    for (rows, tr, ti), (br, bi) in zip(twiddles, inverse):
        o_ref[0, 0, rows, :] = (br * tr + bi * ti).astype(o_ref.dtype)
        o_ref[0, 1, rows, :] = (bi * tr - br * ti).astype(o_ref.dtype)


def _fft_stage_b(plan, a, k=None):
    B, _, R, C = a.shape
    rows = plan.ks_group * FFT_FAST
    data = pl.BlockSpec((1, 2, rows, LANES), lambda r, c, b: (b, 0, r, c))
    tw = pl.BlockSpec((rows, LANES), lambda r, c, b: (r, 0))
    mat = pl.BlockSpec((FFT_FAST, FFT_FAST), lambda r, c, b: (0, 0))
    lane_bcast = lambda col: jnp.broadcast_to(jnp.asarray(col), (col.shape[0], LANES))
    consts = (lane_bcast(plan.tw_re), lane_bcast(plan.tw_im),
              jnp.asarray(plan.f_re, BF16), jnp.asarray(plan.f_im, BF16))
    common = dict(
        grid=(R // rows, C // LANES, B),
        out_specs=data,
        out_shape=jax.ShapeDtypeStruct(a.shape, BF16),
        compiler_params=_params("parallel", "parallel", "arbitrary"),
    )
    if k is None:
        return pl.pallas_call(_fft_fwd_kernel, in_specs=[data, tw, tw, mat, mat],
                              **common)(a, *consts)
    kspec = pl.BlockSpec((1, 2, rows, LANES), lambda r, c, b: (0, 0, r, c))
    return pl.pallas_call(_fft_conv_kernel, in_specs=[data, kspec, tw, tw, mat, mat],
                          **common)(a, k, *consts)


def _fft_long_conv(z, kern, plan):
    B, L, C = z.shape
    S, F, P = plan.slow, plan.fast, plan.ks_pad
    ka = _stage_a(plan.a_full, kern.reshape(1, S, F, C), inverse=False)
    k_spec = _fft_stage_b(plan, ka.reshape(1, 2, P * F, C))
    za = _stage_a(plan.a_half, z.reshape(B, S // 2, F, C), inverse=False)
    ya = _fft_stage_b(plan, za.reshape(B, 2, P * F, C), k_spec)
    y = _stage_a(plan.a_inv, ya.reshape(B, 2, P, F, C), inverse=True)
    return y.reshape(B, L, C)


def _scan_constants():
    C = SCAN_CHUNK
    n_lv = len(SCAN_LEVELS)
    tri = np.tril(np.ones((C, C), np.float32))
    upper = np.zeros((n_lv, C, 1), np.float32)
    mask = np.zeros((n_lv + 1, C, C), np.float32)
    for li, h in enumerate(SCAN_LEVELS):
        for t in range(C):
            r = (t // (2 * h)) * 2 * h + h
            if t % (2 * h) >= h:
                upper[li, t, 0] = 1.0
                mask[li, t, r - h:r] = 1.0
    mask[n_lv] = np.eye(C, dtype=np.float32)
    flip = lambda m: m[:, ::-1, ::-1]
    return (np.stack([tri, tri[::-1, ::-1]]), np.stack([upper, upper[:, ::-1]]),
            np.stack([mask, flip(mask)]))


def _ref_rows(h, reverse):
    rows = []
    for g in range(SCAN_CHUNK // SUBLANES):
        pair = []
        for t in (g * SUBLANES, g * SUBLANES + SUBLANES // 2):
            start = (t // (2 * h)) * 2 * h
            pair.append(start + h if reverse else start + h - 1)
        rows.append(tuple(pair))
    return rows


def _scan_chunk(q_ref, f_ref, v_ref, lb_table, tri, up_ref, mask_ref, o_ref, b_ref, state_ref,
                *, layer, reverse):
    n_lv = len(SCAN_LEVELS)
    C = SCAN_CHUNK

    e = jnp.exp(lb_table - jnp.max(lb_table, axis=0, keepdims=True))
    prob = e / jnp.sum(e, axis=0, keepdims=True)
    lb = jnp.zeros((1, D_HGRN), F32)
    for l in range(1, layer + 1):
        lb = lb + prob[l:l + 1, :]

    q = q_ref[...].astype(F32)
    q = q / (1.0 + jnp.exp(-q))
    z = f_ref[...].astype(F32)
    w = jnp.exp(-jnp.abs(z))
    inv = 1.0 / (1.0 + w)
    log_sig = jnp.minimum(z, 0.0) + jnp.log(inv)
    key = (1.0 - lb) * (jnp.where(z >= 0.0, w, 1.0) * inv)
    log_a = jnp.log(lb)
    log_b = jnp.log1p(-lb) + log_sig
    gate = jnp.maximum(log_a, log_b) + jnp.log(1.0 + jnp.exp(-jnp.abs(log_a - log_b)))
    vb = v_ref[...]

    gate2 = gate * math.log2(math.e)
    g1 = gate2.astype(BF16)
    rem = gate2 - g1.astype(F32)
    g2 = rem.astype(BF16)
    g3 = (rem - g2.astype(F32)).astype(BF16)
    b_inc = (jnp.dot(tri, g1, preferred_element_type=F32)
             + jnp.dot(tri, g2, preferred_element_type=F32)
             + jnp.dot(tri, g3, preferred_element_type=F32))
    b_ref[...] = b_inc

    def bcast_row(r):
        return jnp.broadcast_to(b_ref[pl.ds(r, 1), :], (SUBLANES, D_HGRN))

    first_half = lax.broadcasted_iota(jnp.int32, (SUBLANES, 1), 0) < SUBLANES // 2
    scores = [None] * HGRN_HEADS
    for li, h in enumerate(SCAN_LEVELS):
        is_query = up_ref[li] > 0.5
        if h == 1:
            x = jnp.where(is_query, q * (1.0 - key), key)
        else:
            groups = []
            for r0, r1 in _ref_rows(h, reverse):
                ref = bcast_row(r0)
                groups.append(ref if r1 == r0 else jnp.where(first_half, ref, bcast_row(r1)))
            diff = pltpu.bitcast(b_inc - jnp.concatenate(groups, axis=0), jnp.int32)
            neg_dist = pltpu.bitcast(diff | jnp.int32(-2 ** 31), F32)
            x = jnp.where(is_query, q, key) * jnp.exp2(neg_dist)
        xb = x.astype(BF16)
        keep = mask_ref[li] > 0.5
        for hd in range(HGRN_HEADS):
            xh = xb[:, hd * HEAD_DIM:(hd + 1) * HEAD_DIM]
            p = lax.dot_general(xh, xh, (((1,), (1,)), ((), ())), preferred_element_type=F32)
            scores[hd] = jnp.where(keep, p, 0.0 if scores[hd] is None else scores[hd])

    b_total = bcast_row(0 if reverse else C - 1)
    b_rest = jnp.concatenate([b_total] * (C // SUBLANES), axis=0) - b_inc
    q_dec = (q * jnp.exp2(b_inc)).astype(BF16)
    k_dec = (key * jnp.exp2(b_rest)).astype(BF16)
    total = jnp.exp2(b_total[0:1, :])
    qb, kb = q.astype(BF16), key.astype(BF16)
    on_diag = mask_ref[n_lv] > 0.5
    for hd in range(HGRN_HEADS):
        cols = slice(hd * HEAD_DIM, (hd + 1) * HEAD_DIM)
        diag = lax.dot_general(qb[:, cols], kb[:, cols], (((1,), (1,)), ((), ())),
                               preferred_element_type=F32)
        p = jnp.where(on_diag, diag, scores[hd]).astype(BF16)
        st = state_ref[hd]
        o = jnp.dot(p, vb[:, cols], preferred_element_type=F32)
        o = o + lax.dot_general(q_dec[:, cols], st.astype(BF16), (((1,), (1,)), ((), ())),
                                preferred_element_type=F32)
        o_ref[:, cols] = o.astype(o_ref.dtype)
        upd = lax.dot_general(vb[:, cols], k_dec[:, cols], (((0,), (0,)), ((), ())),
                              preferred_element_type=F32)
        state_ref[hd] = st * total[:, cols] + upd


def _hgrn_scan_kernel(qf_ref, ff_ref, vf_ref, qb_ref, fb_ref, vb_ref, lbt_ref, tri_ref, up_ref,
                      mask_ref, of_ref, ob_ref, b_ref, state_ref, *, layer):
    @pl.when(pl.program_id(1) == 0)
    def _():
        state_ref[...] = jnp.zeros_like(state_ref)

    for s in range(SCAN_CHUNKS_PER_STEP):
        rows = pl.ds(s * SCAN_CHUNK, SCAN_CHUNK)
        _scan_chunk(qf_ref.at[rows], ff_ref.at[rows], vf_ref.at[rows], lbt_ref[0], tri_ref[0],
                    up_ref.at[0], mask_ref.at[0], of_ref.at[rows], b_ref.at[0], state_ref.at[0],
                    layer=layer, reverse=False)
        rows = pl.ds((SCAN_CHUNKS_PER_STEP - 1 - s) * SCAN_CHUNK, SCAN_CHUNK)
        _scan_chunk(qb_ref.at[rows], fb_ref.at[rows], vb_ref.at[rows], lbt_ref[1], tri_ref[1],
                    up_ref.at[1], mask_ref.at[1], ob_ref.at[rows], b_ref.at[1], state_ref.at[1],
                    layer=layer, reverse=True)


def _hgrn_scan(proj, lb_table, layer, batch, seq_len):
    T = proj.shape[0]
    C = SCAN_CHUNK
    rows_per_step = SCAN_CHUNKS_PER_STEP * C
    n_chunks = seq_len // rows_per_step
    tri, upper, mask = _scan_constants()
    col0 = (3 * D_HYENA) // D_HGRN

    fwd = lambda b, c: b * n_chunks + c
    bwd = lambda b, c: b * n_chunks + n_chunks - 1 - c
    chunk = lambda rows, col: pl.BlockSpec((rows_per_step, D_HGRN), lambda b, c: (rows(b, c), col))
    whole = lambda a: pl.BlockSpec(a.shape, lambda b, c: (0,) * a.ndim)
    out = jax.ShapeDtypeStruct((T, D_HGRN), BF16)
    return pl.pallas_call(
        functools.partial(_hgrn_scan_kernel, layer=layer),
        grid=(batch, n_chunks),
        in_specs=[
            chunk(fwd, col0), chunk(fwd, col0 + 1), chunk(fwd, col0 + 3),
            chunk(bwd, col0), chunk(bwd, col0 + 2), chunk(bwd, col0 + 3),
            whole(lb_table), whole(tri), whole(upper), whole(mask),
        ],
        out_specs=[chunk(fwd, 0), chunk(bwd, 0)],
        out_shape=[out, out],
        scratch_shapes=[pltpu.VMEM((2, C, D_HGRN), F32),
                        pltpu.VMEM((2, HGRN_HEADS, HEAD_DIM, HEAD_DIM), F32)],
        compiler_params=_params("parallel", "arbitrary"),
    )(proj, proj, proj, proj, proj, proj, lb_table, jnp.asarray(tri, BF16), jnp.asarray(upper),
      jnp.asarray(mask))


def _group_mean_matrix(group):
    idx = np.arange(D_HYENA) // group
    return (idx[:, None] == idx[None, :]).astype(np.float32)


def _mix_out_kernel(x_ref, y_ref, z_ref, x0_ref, of_ref, ob_ref, g_ref, skip_ref,
                    hy_gain_ref, hg_gain_ref, grp_hy_ref, grp_hg_ref, w_ref, post_ref, o_ref):
    z = z_ref[...]
    yh = x0_ref[...].astype(F32) * (y_ref[...] + skip_ref[...] * z)
    ms = _group_sums(yh * yh, grp_hy_ref[...]) * (HYENA_GROUPS / D_HYENA)
    yh = yh * lax.rsqrt(ms + EPS) * hy_gain_ref[...]
    o = of_ref[...].astype(F32) + ob_ref[...].astype(F32)
    ms = _group_sums(o * o, grp_hg_ref[...]) * (1.0 / HEAD_DIM)
    g = g_ref[...].astype(F32)
    o = o * lax.rsqrt(ms + EPS) * hg_gain_ref[...] * (g * (1.0 / (1.0 + jnp.exp(-g))))
    mix = (jnp.dot(yh.astype(BF16), w_ref[:D_HYENA, :], preferred_element_type=F32)
           + jnp.dot(o.astype(BF16), w_ref[D_HYENA:, :], preferred_element_type=F32))
    o_ref[...] = x_ref[...] + mix * _rms_scale(mix) * post_ref[...]


def _mix_out(x2d, y, z, x0, o_fwd, o_bwd, proj, skip, hy_gain, hg_gain, w_bf16, layer,
             post_gain, tm=1024):
    T = x2d.shape[0]
    gate_col = D_IN // D_HGRN - 1
    half = lambda: pl.BlockSpec((tm, D_HYENA), lambda i: (i, 0))
    vec = lambda n: pl.BlockSpec((1, n), lambda i: (0, 0))
    sq = lambda: pl.BlockSpec((D_HYENA, D_HYENA), lambda i: (0, 0))
    row = lambda v: v.reshape(1, -1)
    return pl.pallas_call(
        _mix_out_kernel,
        grid=(T // tm,),
        in_specs=[
            pl.BlockSpec((tm, D_MODEL), lambda i: (i, 0)),
            half(), half(), half(), half(), half(),
            pl.BlockSpec((tm, D_HGRN), lambda i: (i, gate_col)),
            vec(D_HYENA), vec(D_HYENA), vec(D_HGRN), sq(), sq(),
            pl.BlockSpec((None, D_MODEL, D_MODEL), lambda i: (layer, 0, 0)),
            vec(D_MODEL),
        ],
        out_specs=pl.BlockSpec((tm, D_MODEL), lambda i: (i, 0)),
        out_shape=jax.ShapeDtypeStruct((T, D_MODEL), F32),
        compiler_params=_params("parallel"),
    )(x2d, y, z, x0, o_fwd, o_bwd, proj, row(skip), row(hy_gain), row(hg_gain),
      jnp.asarray(_group_mean_matrix(D_HYENA // HYENA_GROUPS), BF16),
      jnp.asarray(_group_mean_matrix(HEAD_DIM), BF16), w_bf16, row(post_gain))


def _gelu_tanh(x):
    c = -2.0 * math.sqrt(2.0 / math.pi) * math.log2(math.e)
    return x / (1.0 + jnp.exp2(x * (c + (0.044715 * c) * (x * x))))


def _ffn_kernel(prev_ref, main_ref, next_ref, pre_ref, wa_ref, wb_ref, cwa_ref, cwb_ref,
                cba_ref, cbb_ref, wd_ref, post_ref, o_ref, h_ref, acc_ref, ua_ref, ub_ref,
                *, tiles_per_seq, tm):
    j = pl.program_id(1)
    pos = pl.program_id(0) % tiles_per_seq

    @pl.when(j == 0)
    def _():
        gain = pre_ref[...]

        def normed(x):
            return (x * _rms_scale(x) * gain).astype(BF16)

        prev = normed(prev_ref[...])
        nxt = normed(next_ref[...])
        h_ref[0:FFN_HALO, :] = jnp.where(pos == 0, jnp.zeros_like(prev), prev)
        h_ref[FFN_HALO:FFN_HALO + tm, :] = normed(main_ref[...])
        h_ref[FFN_HALO + tm:, :] = jnp.where(pos == tiles_per_seq - 1, jnp.zeros_like(nxt), nxt)
        acc_ref[...] = jnp.zeros_like(acc_ref)

    h = h_ref[...]

    def conv_part(w_ref_, cw_ref_, cb_ref_, u_ref):
        u_ref[...] = jnp.dot(h, w_ref_[...], preferred_element_type=F32)
        cw = cw_ref_[...]
        down = u_ref[pl.ds(FFN_HALO - 1, tm), :]
        mid = u_ref[pl.ds(FFN_HALO, tm), :]
        up = u_ref[pl.ds(FFN_HALO + 1, tm), :]
        return down * cw[0:1, :] + mid * cw[1:2, :] + up * cw[2:3, :] + cb_ref_[...]

    a = conv_part(wa_ref, cwa_ref, cba_ref, ua_ref)
    b = conv_part(wb_ref, cwb_ref, cbb_ref, ub_ref)
    act = (_gelu_tanh(a) * b).astype(BF16)
    acc_ref[...] += jnp.dot(act, wd_ref[...], preferred_element_type=F32)

    @pl.when(j == pl.num_programs(1) - 1)
    def _():
        ff = acc_ref[...]
        o_ref[...] = main_ref[...] + ff * _rms_scale(ff) * post_ref[...]


def _ffn(x2d, pre_gain, w_up_bf16, conv_w, conv_b, w_down_bf16, layer, post_gain, seq_len,
         tm=1024, tf=1024):
    T = x2d.shape[0]
    conv_b = conv_b.reshape(conv_b.shape[0], 1, -1)
    halo_per_tile = tm // FFN_HALO
    last_halo = T // FFN_HALO - 1
    n_f = D_FF // tf
    row = lambda v: v.reshape(1, -1)
    return pl.pallas_call(
        functools.partial(_ffn_kernel, tiles_per_seq=seq_len // tm, tm=tm),
        grid=(T // tm, n_f),
        in_specs=[
            pl.BlockSpec((FFN_HALO, D_MODEL),
                         lambda i, j: (jnp.maximum(i * halo_per_tile - 1, 0), 0)),
            pl.BlockSpec((tm, D_MODEL), lambda i, j: (i, 0)),
            pl.BlockSpec((FFN_HALO, D_MODEL),
                         lambda i, j: (jnp.minimum((i + 1) * halo_per_tile, last_halo), 0)),
            pl.BlockSpec((1, D_MODEL), lambda i, j: (0, 0)),
            pl.BlockSpec((None, D_MODEL, tf), lambda i, j: (layer, 0, j)),
            pl.BlockSpec((None, D_MODEL, tf), lambda i, j: (layer, 0, n_f + j)),
            pl.BlockSpec((None, 3, tf), lambda i, j: (layer, 0, j)),
            pl.BlockSpec((None, 3, tf), lambda i, j: (layer, 0, n_f + j)),
            pl.BlockSpec((None, 1, tf), lambda i, j: (layer, 0, j)),
            pl.BlockSpec((None, 1, tf), lambda i, j: (layer, 0, n_f + j)),
            pl.BlockSpec((None, tf, D_MODEL), lambda i, j: (layer, j, 0)),
            pl.BlockSpec((1, D_MODEL), lambda i, j: (0, 0)),
        ],
        out_specs=pl.BlockSpec((tm, D_MODEL), lambda i, j: (i, 0)),
        out_shape=jax.ShapeDtypeStruct((T, D_MODEL), F32),
        scratch_shapes=[pltpu.VMEM((tm + 2 * FFN_HALO, D_MODEL), BF16),
                        pltpu.VMEM((tm, D_MODEL), F32)]
        + [pltpu.VMEM((tm + 2 * FFN_HALO, tf), F32)] * 2,
        compiler_params=_params("parallel", "arbitrary"),
    )(x2d, x2d, x2d, row(pre_gain), w_up_bf16, w_up_bf16, conv_w, conv_w,
      conv_b, conv_b, w_down_bf16, row(post_gain))


def _trunk(x, p):
    B, L, _ = x.shape
    x2d = x.reshape(B * L, D_MODEL)
    plan = _FftPlan(L)
    for l in range(p["w_in"].shape[0]):
        proj = _in_proj(x2d, p["norm_mix_pre"][l], p["w_in_bf16"], l)
        x0, z = _hyena_pre(proj, p["hyena_conv_w"][l], p["hyena_conv_b"][l], L)
        kern = _hyena_filter(L, p["filt_w1"][l], p["filt_b1"][l], p["filt_w2"][l], p["filt_b2"][l],
                             p["filt_w3"][l], p["filt_b3"][l], p["filt_w4"][l], p["filt_freq"][l])
        y = _fft_long_conv(z.reshape(B, L, D_HYENA), kern, plan).reshape(B * L, D_HYENA)
        o_fwd, o_bwd = _hgrn_scan(proj, p["hgrn_lower_bounds"], l, B, L)
        x2d = _mix_out(x2d, y, z, x0, o_fwd, o_bwd, proj, p["hyena_skip"][l], p["hyena_out_norm"][l],
                       p["hgrn_out_norm"][l], p["w_out_bf16"], l, p["norm_mix_post"][l])
        x2d = _ffn(x2d, p["norm_ffn_pre"][l], p["ffn_w_up_bf16"], p["ffn_conv_w"],
                   p["ffn_conv_b"], p["ffn_w_down_bf16"], l, p["norm_ffn_post"][l], L)
    return x2d.reshape(B, L, D_MODEL)


def kernel(x_prompt, x_sample, norm_mix_pre, norm_mix_post, norm_ffn_pre, norm_ffn_post, w_in, hyena_conv_w, hyena_conv_b, filt_w1, filt_b1, filt_w2, filt_b2, filt_w3, filt_b3, filt_w4, filt_freq, hyena_skip, hyena_out_norm, hgrn_lower_bounds, hgrn_out_norm, w_out, ffn_w_up, ffn_conv_w, ffn_conv_b, ffn_w_down):
    p = dict(
        norm_mix_pre=norm_mix_pre, norm_mix_post=norm_mix_post, norm_ffn_pre=norm_ffn_pre,
        norm_ffn_post=norm_ffn_post, w_in=w_in, hyena_conv_w=hyena_conv_w,
        hyena_conv_b=hyena_conv_b, filt_w1=filt_w1, filt_b1=filt_b1, filt_w2=filt_w2,
        filt_b2=filt_b2, filt_w3=filt_w3, filt_b3=filt_b3, filt_w4=filt_w4, filt_freq=filt_freq,
        hyena_skip=hyena_skip, hyena_out_norm=hyena_out_norm,
        hgrn_lower_bounds=hgrn_lower_bounds, hgrn_out_norm=hgrn_out_norm,
        ffn_conv_w=ffn_conv_w, ffn_conv_b=ffn_conv_b,
        w_in_bf16=w_in.astype(BF16), w_out_bf16=w_out.astype(BF16),
        ffn_w_up_bf16=ffn_w_up.astype(BF16), ffn_w_down_bf16=ffn_w_down.astype(BF16),
    )
    return (_trunk(x_prompt, p), _trunk(x_sample, p))
```

```python
import functools
import math

import numpy as np
import jax
import jax.numpy as jnp
from jax import lax
from jax.experimental import pallas as pl
from jax.experimental.pallas import tpu as pltpu

F32 = jnp.float32
BF16 = jnp.bfloat16
HIGHEST = lax.Precision.HIGHEST

D_MODEL = 1024
D_HYENA = 512
HYENA_GROUPS = 8
D_HGRN = 512
HGRN_HEADS = 4
HEAD_DIM = 128
D_IN = 3 * D_HYENA + 5 * D_HGRN
D_FF = 4 * D_MODEL
FILTER_EMB = 33
FILTER_BANDS = 16
FILTER_HIDDEN = 64
DECAY_TARGET = 1e-2
FAST_DECAY_PCT = 0.3
SLOW_DECAY_PCT = 1.5
EPS = 1e-6

SUBLANES = 8
LANES = 128
BF16_ROWS = 16
VMEM_LIMIT_BYTES = 56 * 1024 * 1024

FFT_FAST = 128
FFT_KS_GROUP_MAX = 20
FFT_F_TILE = 16
FFN_HALO = 16
SCAN_CHUNK = 128
SCAN_CHUNKS_PER_STEP = 4
SCAN_LEVELS = (64, 32, 16, 8, 4, 2, 1)


def _params(*semantics):
    return pltpu.CompilerParams(dimension_semantics=semantics,
                                vmem_limit_bytes=VMEM_LIMIT_BYTES)


def _rms_scale(x):
    return lax.rsqrt(jnp.mean(x * x, axis=-1, keepdims=True) + EPS)


def _group_sums(sq, same_group):
    return jnp.dot(sq.astype(BF16), same_group, preferred_element_type=F32)


def _in_proj_kernel(x_ref, g_ref, w_ref, o_ref, h_ref):
    @pl.when(pl.program_id(1) == 0)
    def _():
        x = x_ref[...]
        h_ref[...] = (x * _rms_scale(x) * g_ref[...]).astype(BF16)

    o_ref[...] = jnp.dot(h_ref[...], w_ref[...], preferred_element_type=F32).astype(o_ref.dtype)


def _in_proj(x2d, gain, w_bf16, layer, tm=1024, tn=4096):
    T = x2d.shape[0]
    N = w_bf16.shape[2]
    return pl.pallas_call(
        _in_proj_kernel,
        grid=(T // tm, N // tn),
        in_specs=[
            pl.BlockSpec((tm, D_MODEL), lambda i, j: (i, 0)),
            pl.BlockSpec((1, D_MODEL), lambda i, j: (0, 0)),
            pl.BlockSpec((None, D_MODEL, tn), lambda i, j: (layer, 0, j)),
        ],
        out_specs=pl.BlockSpec((tm, tn), lambda i, j: (i, j)),
        out_shape=jax.ShapeDtypeStruct((T, N), BF16),
        scratch_shapes=[pltpu.VMEM((tm, D_MODEL), BF16)],
        compiler_params=_params("parallel", "arbitrary"),
    )(x2d, gain.reshape(1, D_MODEL), w_bf16)


def _shift_rows(main, prev_row, next_row):
    tm = main.shape[0]
    rows = lax.broadcasted_iota(jnp.int32, (tm, 1), 0)
    down = jnp.where(rows == 0, prev_row, pltpu.roll(main, 1, 0))
    up = jnp.where(rows == tm - 1, next_row, pltpu.roll(main, tm - 1, 0))
    return down, up


def _hyena_pre_kernel(prev_ref, main_ref, next_ref, w_ref, b_ref, x0_ref, z_ref,
                      *, tiles_per_seq):
    pos = pl.program_id(0) % tiles_per_seq
    main = main_ref[...].astype(F32)
    prev_row = jnp.where(pos == 0, 0.0, prev_ref[...].astype(F32)[BF16_ROWS - 1:BF16_ROWS, :])
    next_row = jnp.where(pos == tiles_per_seq - 1, 0.0, next_ref[...].astype(F32)[0:1, :])
    down, up = _shift_rows(main, prev_row, next_row)
    w = w_ref[...]
    u = down * w[0:1, :] + main * w[1:2, :] + up * w[2:3, :] + b_ref[...]
    x0_ref[...] = u[:, :D_HYENA].astype(x0_ref.dtype)
    z_ref[...] = u[:, D_HYENA:2 * D_HYENA] * u[:, 2 * D_HYENA:]


def _hyena_pre(proj, conv_w, conv_b, seq_len, tm=512):
    T = proj.shape[0]
    width = 3 * D_HYENA
    halo_per_tile = tm // BF16_ROWS
    last_halo = T // BF16_ROWS - 1
    return pl.pallas_call(
        functools.partial(_hyena_pre_kernel, tiles_per_seq=seq_len // tm),
        grid=(T // tm,),
        in_specs=[
            pl.BlockSpec((BF16_ROWS, width),
                         lambda i: (jnp.maximum(i * halo_per_tile - 1, 0), 0)),
            pl.BlockSpec((tm, width), lambda i: (i, 0)),
            pl.BlockSpec((BF16_ROWS, width),
                         lambda i: (jnp.minimum((i + 1) * halo_per_tile, last_halo), 0)),
            pl.BlockSpec((3, width), lambda i: (0, 0)),
            pl.BlockSpec((1, width), lambda i: (0, 0)),
        ],
        out_specs=[pl.BlockSpec((tm, D_HYENA), lambda i: (i, 0)),
                   pl.BlockSpec((tm, D_HYENA), lambda i: (i, 0))],
        out_shape=[jax.ShapeDtypeStruct((T, D_HYENA), BF16),
                   jax.ShapeDtypeStruct((T, D_HYENA), F32)],
        compiler_params=_params("parallel"),
    )(proj, proj, proj, conv_w, conv_b.reshape(1, width))


def _filter_kernel(band_ref, w1t_ref, w1c_ref, w1s_ref, b1_ref, w2_ref, b2_ref, w3_ref, b3_ref,
                   w4_ref, freq_ref, delta_ref, o_ref, *, seq_len, tr):
    row0 = pl.program_id(0) * tr

    def tap_index(shape, axis):
        j = row0 + lax.broadcasted_iota(jnp.int32, shape, axis)
        return j, jnp.where(j < seq_len, j, 2 * seq_len - j).astype(F32)

    _, idx = tap_index((1, tr), 1)
    t = idx * (1.0 / (seq_len - 1))
    arg = band_ref[...] * ((2.0 * math.pi / seq_len) * idx)
    fr = freq_ref[...]

    def dense(w_ref_, a):
        return jnp.dot(w_ref_[...], a, precision=HIGHEST, preferred_element_type=F32)

    h = w1t_ref[...] * t + dense(w1c_ref, jnp.cos(arg)) - dense(w1s_ref, jnp.sin(arg))
    h = jnp.sin(fr * (h + b1_ref[...]))
    h = jnp.sin(fr * (dense(w2_ref, h) + b2_ref[...]))
    h = jnp.sin(fr * (dense(w3_ref, h) + b3_ref[...]))
    out = lax.dot_general(h.astype(BF16), w4_ref[...].astype(BF16), (((0,), (0,)), ((), ())),
                          preferred_element_type=F32)
    j_col, idx_col = tap_index((tr, 1), 0)
    window = jnp.exp(-(idx_col * (1.0 / (seq_len - 1))) * delta_ref[...])
    o_ref[...] = jnp.where(j_col == seq_len, 0.0, out * window)


def _hyena_filter(seq_len, w1, b1, w2, b2, w3, b3, w4, freq, tr=1024):
    n_taps = 2 * seq_len
    bands = np.linspace(1e-4, FILTER_BANDS - 1, FILTER_BANDS, dtype=np.float32).reshape(-1, 1)
    deltas = np.abs(np.linspace(math.log(DECAY_TARGET) / SLOW_DECAY_PCT,
                                math.log(DECAY_TARGET) / FAST_DECAY_PCT, D_HYENA,
                                dtype=np.float32)).reshape(1, D_HYENA)
    tiles_fwd = seq_len // tr
    const = lambda i: (0, 0)
    col = lambda v: v.reshape(-1, 1)
    full = lambda a: pl.BlockSpec(a.shape, const)
    operands = [jnp.asarray(bands), col(w1[0]), w1[1:1 + FILTER_BANDS].T,
                w1[1 + FILTER_BANDS:].T, col(b1), w2.T, col(b2), w3.T, col(b3)]
    return pl.pallas_call(
        functools.partial(_filter_kernel, seq_len=seq_len, tr=tr),
        grid=(n_taps // tr,),
        in_specs=[full(a) for a in operands] + [
            pl.BlockSpec((FILTER_HIDDEN, D_HYENA), lambda i: (0, i // tiles_fwd)),
            pl.BlockSpec((FILTER_HIDDEN, 1), const),
            pl.BlockSpec((1, D_HYENA), const),
        ],
        out_specs=pl.BlockSpec((tr, D_HYENA), lambda i: (i, 0)),
        out_shape=jax.ShapeDtypeStruct((n_taps, D_HYENA), F32),
        compiler_params=_params("parallel"),
    )(*operands, w4, col(freq), jnp.asarray(deltas))


class _FftPlan:
    def __init__(self, seq_len):
        self.n = 2 * seq_len
        self.fast = FFT_FAST
        self.slow = self.n // FFT_FAST
        self.slow_half = self.slow // 2
        self.ks = self.slow_half + 1
        self.ks_pad = -(-self.ks // SUBLANES) * SUBLANES
        self.ks_group = max(g for g in range(1, FFT_KS_GROUP_MAX + 1) if self.ks_pad % g == 0)
        S, F, N = self.slow, self.fast, self.n
        ks = np.arange(self.ks_pad, dtype=np.float64)[:, None]
        valid = (ks < self.ks)

        def stage_a(n_s):
            s = np.arange(n_s, dtype=np.float64)[None, :]
            ang = 2.0 * np.pi * ks * s / S
            return np.concatenate([np.where(valid, np.cos(ang), 0.0),
                                   np.where(valid, -np.sin(ang), 0.0)], axis=0)

        self.a_half = stage_a(self.slow_half).astype(np.float32)
        self.a_full = stage_a(self.slow).astype(np.float32)
        s = np.arange(self.slow_half, dtype=np.float64)[:, None]
        kk = np.arange(self.ks_pad, dtype=np.float64)[None, :]
        weight = np.where((kk == 0) | (kk == self.slow_half), 1.0, 2.0) * (kk < self.ks) / N
        ang = 2.0 * np.pi * s * kk / S
        self.a_inv = np.concatenate([weight * np.cos(ang), -weight * np.sin(ang)],
                                    axis=1).astype(np.float32)
        f = np.arange(F, dtype=np.float64)
        ang = 2.0 * np.pi * np.outer(np.arange(self.ks_pad, dtype=np.float64), f) / N
        self.tw_re = np.cos(ang).reshape(-1, 1).astype(np.float32)
        self.tw_im = (-np.sin(ang)).reshape(-1, 1).astype(np.float32)
        ang = 2.0 * np.pi * np.outer(f, f) / F
        self.f_re = np.cos(ang).astype(np.float32)
        self.f_im = (-np.sin(ang)).astype(np.float32)


def _dft_dot(a_ref, x):
    return jnp.dot(a_ref[...], x.astype(BF16), preferred_element_type=F32)


def _stage_a_kernel(a_ref, x_ref, o_ref, stage_ref, *, batch):
    m, k = a_ref.shape
    x2d = x_ref.reshape(batch * k * FFT_F_TILE, LANES)
    o2d = o_ref.reshape(batch * m * FFT_F_TILE, LANES)
    stage_in = x_ref.dtype != F32
    for b in range(batch):
        x_rows = pl.ds(b * k * FFT_F_TILE, k * FFT_F_TILE)
        o_rows = pl.ds(b * m * FFT_F_TILE, m * FFT_F_TILE)
        if stage_in:
            stage_ref[...] = x2d[x_rows, :].astype(F32)
            slabs = [stage_ref[pl.ds(j, k, stride=FFT_F_TILE), :] for j in range(FFT_F_TILE)]
        else:
            slabs = [x2d[pl.ds(b * k * FFT_F_TILE + j, k, stride=FFT_F_TILE), :]
                     for j in range(FFT_F_TILE)]
        r = _dft_dot(a_ref, jnp.concatenate(slabs, axis=-1))
        for j in range(FFT_F_TILE):
            r_j = r[:, j * LANES:(j + 1) * LANES]
            if stage_in:
                o2d[pl.ds(b * m * FFT_F_TILE + j, m, stride=FFT_F_TILE), :] = r_j
            else:
                stage_ref[pl.ds(j, m, stride=FFT_F_TILE), :] = r_j
        if not stage_in:
            o2d[o_rows, :] = stage_ref[...].astype(o_ref.dtype)


def _stage_a(a, x, inverse):
    F, C = x.shape[-2:]
    B = x.shape[0]
    M, K = a.shape
    x_dims, o_dims = ((2, K // 2), (M,)) if inverse else ((K,), (2, M // 2))

    def spec(dims):
        zeros = (0,) * len(dims)
        return pl.BlockSpec((B,) + dims + (FFT_F_TILE, LANES), lambda f, c: (0,) + zeros + (f, c))

    return pl.pallas_call(
        functools.partial(_stage_a_kernel, batch=B),
        grid=(F // FFT_F_TILE, C // LANES),
        in_specs=[pl.BlockSpec((M, K), lambda f, c: (0, 0)), spec(x_dims)],
        out_specs=spec(o_dims),
        out_shape=jax.ShapeDtypeStruct((B,) + o_dims + (F, C), F32 if inverse else BF16),
        scratch_shapes=[pltpu.VMEM(((K if inverse else M) * FFT_F_TILE, LANES), F32)],
        compiler_params=_params("parallel", "parallel"),
    )(jnp.asarray(a, BF16), x)


def _stage_b_dft(fr_ref, fi_ref, tiles, conj=False):
    w = tiles[0][0].shape[-1]
    both = jnp.concatenate([part for tile in tiles for part in tile], axis=-1)
    pr = _dft_dot(fr_ref, both)
    pi = _dft_dot(fi_ref, both)
    out = []
    for g in range(len(tiles)):
        re, im = slice(2 * g * w, (2 * g + 1) * w), slice((2 * g + 1) * w, (2 * g + 2) * w)
        if conj:
            out.append((pr[:, re] + pi[:, im], pr[:, im] - pi[:, re]))
        else:
            out.append((pr[:, re] - pi[:, im], pr[:, im] + pi[:, re]))
    return out


def _twiddled_tiles(a_ref, twr_ref, twi_ref):
    tiles, twiddles = [], []
    for g in range(a_ref.shape[2] // FFT_FAST):
        rows = slice(g * FFT_FAST, (g + 1) * FFT_FAST)
        ar, ai = a_ref[0, 0, rows, :].astype(F32), a_ref[0, 1, rows, :].astype(F32)
        tr, ti = twr_ref[rows, :], twi_ref[rows, :]
        tiles.append((ar * tr - ai * ti, ar * ti + ai * tr))
        twiddles.append((rows, tr, ti))
    return tiles, twiddles


def _fft_fwd_kernel(a_ref, twr_ref, twi_ref, fr_ref, fi_ref, o_ref):
    tiles, twiddles = _twiddled_tiles(a_ref, twr_ref, twi_ref)
    for (rows, _, _), (xr, xi) in zip(twiddles, _stage_b_dft(fr_ref, fi_ref, tiles)):
        o_ref[0, 0, rows, :] = xr.astype(o_ref.dtype)
        o_ref[0, 1, rows, :] = xi.astype(o_ref.dtype)


def _fft_conv_kernel(a_ref, k_ref, twr_ref, twi_ref, fr_ref, fi_ref, o_ref):
    tiles, twiddles = _twiddled_tiles(a_ref, twr_ref, twi_ref)
    products = []
    for (rows, _, _), (xr, xi) in zip(twiddles, _stage_b_dft(fr_ref, fi_ref, tiles)):
        kr, ki = k_ref[0, 0, rows, :].astype(F32), k_ref[0, 1, rows, :].astype(F32)
        products.append((xr * kr - xi * ki, xr * ki + xi * kr))
    inverse = _stage_b_dft(fr_ref, fi_ref, products, conj=True)
    for (rows, tr, ti), (br, bi) in zip(twiddles, inverse):
        o_ref[0, 0, rows, :] = (br * tr + bi * ti).astype(o_ref.dtype)
        o_ref[0, 1, rows, :] = (bi * tr - br * ti).astype(o_ref.dtype)


def _fft_stage_b(plan, a, k=None):
    B, _, R, C = a.shape
    rows = plan.ks_group * FFT_FAST
    data = pl.BlockSpec((1, 2, rows, LANES), lambda r, c, b: (b, 0, r, c))
    tw = pl.BlockSpec((rows, LANES), lambda r, c, b: (r, 0))
    mat = pl.BlockSpec((FFT_FAST, FFT_FAST), lambda r, c, b: (0, 0))
    lane_bcast = lambda col: jnp.broadcast_to(jnp.asarray(col), (col.shape[0], LANES))
    consts = (lane_bcast(plan.tw_re), lane_bcast(plan.tw_im),
              jnp.asarray(plan.f_re, BF16), jnp.asarray(plan.f_im, BF16))
    common = dict(
        grid=(R // rows, C // LANES, B),
        out_specs=data,
        out_shape=jax.ShapeDtypeStruct(a.shape, BF16),
        compiler_params=_params("parallel", "parallel", "arbitrary"),
    )
    if k is None:
        return pl.pallas_call(_fft_fwd_kernel, in_specs=[data, tw, tw, mat, mat],
                              **common)(a, *consts)
    kspec = pl.BlockSpec((1, 2, rows, LANES), lambda r, c, b: (0, 0, r, c))
    return pl.pallas_call(_fft_conv_kernel, in_specs=[data, kspec, tw, tw, mat, mat],
                          **common)(a, k, *consts)


def _fft_long_conv(z, kern, plan):
    B, L, C = z.shape
    S, F, P = plan.slow, plan.fast, plan.ks_pad
    ka = _stage_a(plan.a_full, kern.reshape(1, S, F, C), inverse=False)
    k_spec = _fft_stage_b(plan, ka.reshape(1, 2, P * F, C))
    za = _stage_a(plan.a_half, z.reshape(B, S // 2, F, C), inverse=False)
    ya = _fft_stage_b(plan, za.reshape(B, 2, P * F, C), k_spec)
    y = _stage_a(plan.a_inv, ya.reshape(B, 2, P, F, C), inverse=True)
    return y.reshape(B, L, C)


def _scan_constants():
    C = SCAN_CHUNK
    n_lv = len(SCAN_LEVELS)
    tri = np.tril(np.ones((C, C), np.float32))
    upper = np.zeros((n_lv, C, 1), np.float32)
    mask = np.zeros((n_lv + 1, C, C), np.float32)
    for li, h in enumerate(SCAN_LEVELS):
        for t in range(C):
            r = (t // (2 * h)) * 2 * h + h
            if t % (2 * h) >= h:
                upper[li, t, 0] = 1.0
                mask[li, t, r - h:r] = 1.0
    mask[n_lv] = np.eye(C, dtype=np.float32)
    flip = lambda m: m[:, ::-1, ::-1]
    return (np.stack([tri, tri[::-1, ::-1]]), np.stack([upper, upper[:, ::-1]]),
            np.stack([mask, flip(mask)]))


def _ref_rows(h, reverse):
    rows = []
    for g in range(SCAN_CHUNK // SUBLANES):
        pair = []
        for t in (g * SUBLANES, g * SUBLANES + SUBLANES // 2):
            start = (t // (2 * h)) * 2 * h
            pair.append(start + h if reverse else start + h - 1)
        rows.append(tuple(pair))
    return rows


def _scan_chunk(q_ref, f_ref, v_ref, lb_table, tri, up_ref, mask_ref, o_ref, b_ref, state_ref,
                *, layer, reverse):
    n_lv = len(SCAN_LEVELS)
    C = SCAN_CHUNK

    e = jnp.exp(lb_table - jnp.max(lb_table, axis=0, keepdims=True))
    prob = e / jnp.sum(e, axis=0, keepdims=True)
    lb = jnp.zeros((1, D_HGRN), F32)
    for l in range(1, layer + 1):
        lb = lb + prob[l:l + 1, :]

    q = q_ref[...].astype(F32)
    q = q / (1.0 + jnp.exp(-q))
    z = f_ref[...].astype(F32)
    w = jnp.exp(-jnp.abs(z))
    inv = 1.0 / (1.0 + w)
    log_sig = jnp.minimum(z, 0.0) + jnp.log(inv)
    key = (1.0 - lb) * (jnp.where(z >= 0.0, w, 1.0) * inv)
    log_a = jnp.log(lb)
    log_b = jnp.log1p(-lb) + log_sig
    gate = jnp.maximum(log_a, log_b) + jnp.log(1.0 + jnp.exp(-jnp.abs(log_a - log_b)))
    vb = v_ref[...]

    gate2 = gate * math.log2(math.e)
    g1 = gate2.astype(BF16)
    rem = gate2 - g1.astype(F32)
    g2 = rem.astype(BF16)
    g3 = (rem - g2.astype(F32)).astype(BF16)
    b_inc = (jnp.dot(tri, g1, preferred_element_type=F32)
             + jnp.dot(tri, g2, preferred_element_type=F32)
             + jnp.dot(tri, g3, preferred_element_type=F32))
    for hd in range(HGRN_HEADS):
        b_ref[hd] = b_inc[:, hd * HEAD_DIM:(hd + 1) * HEAD_DIM]

    def bcast_row(r):
        return jnp.concatenate([b_ref.at[hd][pl.ds(r, SUBLANES, stride=0), :]
                                for hd in range(HGRN_HEADS)], axis=-1)

    def by_role(h):
        blocks = []
        for start in range(0, C, h):
            upper = (start // h) % 2 == 1
            src = q if upper != reverse else key
            blocks.append(src[start:start + h])
        return jnp.concatenate(blocks, axis=0)

    first_half = lax.broadcasted_iota(jnp.int32, (SUBLANES, 1), 0) < SUBLANES // 2
    scores = [None] * HGRN_HEADS
    for li, h in enumerate(SCAN_LEVELS):
        if h == 1:
            x = jnp.where(up_ref[li] > 0.5, q * (1.0 - key), key)
        else:
            groups = []
            for r0, r1 in _ref_rows(h, reverse):
                ref = bcast_row(r0)
                groups.append(ref if r1 == r0 else jnp.where(first_half, ref, bcast_row(r1)))
            diff = pltpu.bitcast(b_inc - jnp.concatenate(groups, axis=0), jnp.int32)
            neg_dist = pltpu.bitcast(diff | jnp.int32(-2 ** 31), F32)
            roles = by_role(h) if h >= SUBLANES else jnp.where(up_ref[li] > 0.5, q, key)
            x = roles * jnp.exp2(neg_dist)
        xb = x.astype(BF16)
        keep = mask_ref[li] > 0.5
        for hd in range(HGRN_HEADS):
            xh = xb[:, hd * HEAD_DIM:(hd + 1) * HEAD_DIM]
            p = lax.dot_general(xh, xh, (((1,), (1,)), ((), ())), preferred_element_type=F32)
            scores[hd] = jnp.where(keep, p, 0.0 if scores[hd] is None else scores[hd])

    b_total = bcast_row(0 if reverse else C - 1)
    b_rest = jnp.concatenate([b_total] * (C // SUBLANES), axis=0) - b_inc
    q_dec = (q * jnp.exp2(b_inc)).astype(BF16)
    k_dec = (key * jnp.exp2(b_rest)).astype(BF16)
    total = jnp.exp2(b_total[0:1, :])
    qb, kb = q.astype(BF16), key.astype(BF16)
    on_diag = mask_ref[n_lv] > 0.5
    for hd in range(HGRN_HEADS):
        cols = slice(hd * HEAD_DIM, (hd + 1) * HEAD_DIM)
        diag = lax.dot_general(qb[:, cols], kb[:, cols], (((1,), (1,)), ((), ())),
                               preferred_element_type=F32)
        p = jnp.where(on_diag, diag, scores[hd]).astype(BF16)
        st = state_ref[hd]
        o = jnp.dot(p, vb[:, cols], preferred_element_type=F32)
        o = o + lax.dot_general(q_dec[:, cols], st.astype(BF16), (((1,), (1,)), ((), ())),
                                preferred_element_type=F32)
        o_ref[:, cols] = o.astype(o_ref.dtype)
        upd = lax.dot_general(vb[:, cols], k_dec[:, cols], (((0,), (0,)), ((), ())),
                              preferred_element_type=F32)
        state_ref[hd] = st * total[:, cols] + upd


def _hgrn_scan_kernel(qf_ref, ff_ref, vf_ref, qb_ref, fb_ref, vb_ref, lbt_ref, tri_ref, up_ref,
                      mask_ref, of_ref, ob_ref, b_ref, state_ref, *, layer):
    @pl.when(pl.program_id(1) == 0)
    def _():
        state_ref[...] = jnp.zeros_like(state_ref)

    for s in range(SCAN_CHUNKS_PER_STEP):
        rows = pl.ds(s * SCAN_CHUNK, SCAN_CHUNK)
        _scan_chunk(qf_ref.at[rows], ff_ref.at[rows], vf_ref.at[rows], lbt_ref[0], tri_ref[0],
                    up_ref.at[0], mask_ref.at[0], of_ref.at[rows], b_ref.at[0], state_ref.at[0],
                    layer=layer, reverse=False)
        rows = pl.ds((SCAN_CHUNKS_PER_STEP - 1 - s) * SCAN_CHUNK, SCAN_CHUNK)
        _scan_chunk(qb_ref.at[rows], fb_ref.at[rows], vb_ref.at[rows], lbt_ref[1], tri_ref[1],
                    up_ref.at[1], mask_ref.at[1], ob_ref.at[rows], b_ref.at[1], state_ref.at[1],
                    layer=layer, reverse=True)


def _hgrn_scan(proj, lb_table, layer, batch, seq_len):
    T = proj.shape[0]
    C = SCAN_CHUNK
    rows_per_step = SCAN_CHUNKS_PER_STEP * C
    n_chunks = seq_len // rows_per_step
    tri, upper, mask = _scan_constants()
    col0 = (3 * D_HYENA) // D_HGRN

    fwd = lambda b, c: b * n_chunks + c
    bwd = lambda b, c: b * n_chunks + n_chunks - 1 - c
    chunk = lambda rows, col: pl.BlockSpec((rows_per_step, D_HGRN), lambda b, c: (rows(b, c), col))
    whole = lambda a: pl.BlockSpec(a.shape, lambda b, c: (0,) * a.ndim)
    out = jax.ShapeDtypeStruct((T, D_HGRN), BF16)
    return pl.pallas_call(
        functools.partial(_hgrn_scan_kernel, layer=layer),
        grid=(batch, n_chunks),
        in_specs=[
            chunk(fwd, col0), chunk(fwd, col0 + 1), chunk(fwd, col0 + 3),
            chunk(bwd, col0), chunk(bwd, col0 + 2), chunk(bwd, col0 + 3),
            whole(lb_table), whole(tri), whole(upper), whole(mask),
        ],
        out_specs=[chunk(fwd, 0), chunk(bwd, 0)],
        out_shape=[out, out],
        scratch_shapes=[pltpu.VMEM((2, HGRN_HEADS, C, HEAD_DIM), F32),
                        pltpu.VMEM((2, HGRN_HEADS, HEAD_DIM, HEAD_DIM), F32)],
        compiler_params=_params("parallel", "arbitrary"),
    )(proj, proj, proj, proj, proj, proj, lb_table, jnp.asarray(tri, BF16), jnp.asarray(upper),
      jnp.asarray(mask))


def _group_mean_matrix(group):
    idx = np.arange(D_HYENA) // group
    return (idx[:, None] == idx[None, :]).astype(np.float32)


def _mix_out_kernel(x_ref, y_ref, z_ref, x0_ref, of_ref, ob_ref, g_ref, skip_ref,
                    hy_gain_ref, hg_gain_ref, grp_hy_ref, grp_hg_ref, w_ref, post_ref, o_ref):
    z = z_ref[...]
    yh = x0_ref[...].astype(F32) * (y_ref[...] + skip_ref[...] * z)
    ms = _group_sums(yh * yh, grp_hy_ref[...]) * (HYENA_GROUPS / D_HYENA)
    yh = yh * lax.rsqrt(ms + EPS) * hy_gain_ref[...]
    o = of_ref[...].astype(F32) + ob_ref[...].astype(F32)
    ms = _group_sums(o * o, grp_hg_ref[...]) * (1.0 / HEAD_DIM)
    g = g_ref[...].astype(F32)
    o = o * lax.rsqrt(ms + EPS) * hg_gain_ref[...] * (g * (1.0 / (1.0 + jnp.exp(-g))))
    mix = (jnp.dot(yh.astype(BF16), w_ref[:D_HYENA, :], preferred_element_type=F32)
           + jnp.dot(o.astype(BF16), w_ref[D_HYENA:, :], preferred_element_type=F32))
    o_ref[...] = x_ref[...] + mix * _rms_scale(mix) * post_ref[...]


def _mix_out(x2d, y, z, x0, o_fwd, o_bwd, proj, skip, hy_gain, hg_gain, w_bf16, layer,
             post_gain, tm=1024):
    T = x2d.shape[0]
    gate_col = D_IN // D_HGRN - 1
    half = lambda: pl.BlockSpec((tm, D_HYENA), lambda i: (i, 0))
    vec = lambda n: pl.BlockSpec((1, n), lambda i: (0, 0))
    sq = lambda: pl.BlockSpec((D_HYENA, D_HYENA), lambda i: (0, 0))
    row = lambda v: v.reshape(1, -1)
    return pl.pallas_call(
        _mix_out_kernel,
        grid=(T // tm,),
        in_specs=[
            pl.BlockSpec((tm, D_MODEL), lambda i: (i, 0)),
            half(), half(), half(), half(), half(),
            pl.BlockSpec((tm, D_HGRN), lambda i: (i, gate_col)),
            vec(D_HYENA), vec(D_HYENA), vec(D_HGRN), sq(), sq(),
            pl.BlockSpec((None, D_MODEL, D_MODEL), lambda i: (layer, 0, 0)),
            vec(D_MODEL),
        ],
        out_specs=pl.BlockSpec((tm, D_MODEL), lambda i: (i, 0)),
        out_shape=jax.ShapeDtypeStruct((T, D_MODEL), F32),
        compiler_params=_params("parallel"),
    )(x2d, y, z, x0, o_fwd, o_bwd, proj, row(skip), row(hy_gain), row(hg_gain),
      jnp.asarray(_group_mean_matrix(D_HYENA // HYENA_GROUPS), BF16),
      jnp.asarray(_group_mean_matrix(HEAD_DIM), BF16), w_bf16, row(post_gain))


def _gelu_tanh(x):
    c = -2.0 * math.sqrt(2.0 / math.pi) * math.log2(math.e)
    return x / (1.0 + jnp.exp2(x * (c + (0.044715 * c) * (x * x))))


def _ffn_kernel(prev_ref, main_ref, next_ref, pre_ref, wa_ref, wb_ref, cwa_ref, cwb_ref,
                cba_ref, cbb_ref, wd_ref, post_ref, o_ref, h_ref, acc_ref, ua_ref, ub_ref,
                *, tiles_per_seq, tm):
    j = pl.program_id(1)
    pos = pl.program_id(0) % tiles_per_seq

    @pl.when(j == 0)
    def _():
        gain = pre_ref[...]

        def normed(x):
            return (x * _rms_scale(x) * gain).astype(BF16)

        prev = normed(prev_ref[...])
        nxt = normed(next_ref[...])
        h_ref[0:FFN_HALO, :] = jnp.where(pos == 0, jnp.zeros_like(prev), prev)
        h_ref[FFN_HALO:FFN_HALO + tm, :] = normed(main_ref[...])
        h_ref[FFN_HALO + tm:, :] = jnp.where(pos == tiles_per_seq - 1, jnp.zeros_like(nxt), nxt)
        acc_ref[...] = jnp.zeros_like(acc_ref)

    h = h_ref[...]

    def conv_part(w_ref_, cw_ref_, cb_ref_, u_ref):
        u_ref[...] = jnp.dot(h, w_ref_[...], preferred_element_type=F32)
        cw = cw_ref_[...]
        down = u_ref[pl.ds(FFN_HALO - 1, tm), :]
        mid = u_ref[pl.ds(FFN_HALO, tm), :]
        up = u_ref[pl.ds(FFN_HALO + 1, tm), :]
        return down * cw[0:1, :] + mid * cw[1:2, :] + up * cw[2:3, :] + cb_ref_[...]

    a = conv_part(wa_ref, cwa_ref, cba_ref, ua_ref)
    b = conv_part(wb_ref, cwb_ref, cbb_ref, ub_ref)
    act = (_gelu_tanh(a) * b).astype(BF16)
    acc_ref[...] += jnp.dot(act, wd_ref[...], preferred_element_type=F32)

    @pl.when(j == pl.num_programs(1) - 1)
    def _():
        ff = acc_ref[...]
        o_ref[...] = main_ref[...] + ff * _rms_scale(ff) * post_ref[...]


def _ffn(x2d, pre_gain, w_up_bf16, conv_w, conv_b, w_down_bf16, layer, post_gain, seq_len,
         tm=1024, tf=1024):
    T = x2d.shape[0]
    conv_b = conv_b.reshape(conv_b.shape[0], 1, -1)
    halo_per_tile = tm // FFN_HALO
    last_halo = T // FFN_HALO - 1
    n_f = D_FF // tf
    row = lambda v: v.reshape(1, -1)
    return pl.pallas_call(
        functools.partial(_ffn_kernel, tiles_per_seq=seq_len // tm, tm=tm),
        grid=(T // tm, n_f),
        in_specs=[
            pl.BlockSpec((FFN_HALO, D_MODEL),
                         lambda i, j: (jnp.maximum(i * halo_per_tile - 1, 0), 0)),
            pl.BlockSpec((tm, D_MODEL), lambda i, j: (i, 0)),
            pl.BlockSpec((FFN_HALO, D_MODEL),
                         lambda i, j: (jnp.minimum((i + 1) * halo_per_tile, last_halo), 0)),
            pl.BlockSpec((1, D_MODEL), lambda i, j: (0, 0)),
            pl.BlockSpec((None, D_MODEL, tf), lambda i, j: (layer, 0, j)),
            pl.BlockSpec((None, D_MODEL, tf), lambda i, j: (layer, 0, n_f + j)),
            pl.BlockSpec((None, 3, tf), lambda i, j: (layer, 0, j)),
            pl.BlockSpec((None, 3, tf), lambda i, j: (layer, 0, n_f + j)),
            pl.BlockSpec((None, 1, tf), lambda i, j: (layer, 0, j)),
            pl.BlockSpec((None, 1, tf), lambda i, j: (layer, 0, n_f + j)),
            pl.BlockSpec((None, tf, D_MODEL), lambda i, j: (layer, j, 0)),
            pl.BlockSpec((1, D_MODEL), lambda i, j: (0, 0)),
        ],
        out_specs=pl.BlockSpec((tm, D_MODEL), lambda i, j: (i, 0)),
        out_shape=jax.ShapeDtypeStruct((T, D_MODEL), F32),
        scratch_shapes=[pltpu.VMEM((tm + 2 * FFN_HALO, D_MODEL), BF16),
                        pltpu.VMEM((tm, D_MODEL), F32)]
        + [pltpu.VMEM((tm + 2 * FFN_HALO, tf), F32)] * 2,
        compiler_params=_params("parallel", "arbitrary"),
    )(x2d, x2d, x2d, row(pre_gain), w_up_bf16, w_up_bf16, conv_w, conv_w,
      conv_b, conv_b, w_down_bf16, row(post_gain))


def _trunk(x, p):
    B, L, _ = x.shape
    x2d = x.reshape(B * L, D_MODEL)
    plan = _FftPlan(L)
    for l in range(p["w_in"].shape[0]):
        proj = _in_proj(x2d, p["norm_mix_pre"][l], p["w_in_bf16"], l)
        x0, z = _hyena_pre(proj, p["hyena_conv_w"][l], p["hyena_conv_b"][l], L)
        kern = _hyena_filter(L, p["filt_w1"][l], p["filt_b1"][l], p["filt_w2"][l], p["filt_b2"][l],
                             p["filt_w3"][l], p["filt_b3"][l], p["filt_w4"][l], p["filt_freq"][l])
        y = _fft_long_conv(z.reshape(B, L, D_HYENA), kern, plan).reshape(B * L, D_HYENA)
        o_fwd, o_bwd = _hgrn_scan(proj, p["hgrn_lower_bounds"], l, B, L)
        x2d = _mix_out(x2d, y, z, x0, o_fwd, o_bwd, proj, p["hyena_skip"][l], p["hyena_out_norm"][l],
                       p["hgrn_out_norm"][l], p["w_out_bf16"], l, p["norm_mix_post"][l])
        x2d = _ffn(x2d, p["norm_ffn_pre"][l], p["ffn_w_up_bf16"], p["ffn_conv_w"],
                   p["ffn_conv_b"], p["ffn_w_down_bf16"], l, p["norm_ffn_post"][l], L)
    return x2d.reshape(B, L, D_MODEL)


def kernel(x_prompt, x_sample, norm_mix_pre, norm_mix_post, norm_ffn_pre, norm_ffn_post, w_in, hyena_conv_w, hyena_conv_b, filt_w1, filt_b1, filt_w2, filt_b2, filt_w3, filt_b3, filt_w4, filt_freq, hyena_skip, hyena_out_norm, hgrn_lower_bounds, hgrn_out_norm, w_out, ffn_w_up, ffn_conv_w, ffn_conv_b, ffn_w_down):
    p = dict(
        norm_mix_pre=norm_mix_pre, norm_mix_post=norm_mix_post, norm_ffn_pre=norm_ffn_pre,
        norm_ffn_post=norm_ffn_post, w_in=w_in, hyena_conv_w=hyena_conv_w,
        hyena_conv_b=hyena_conv_b, filt_w1=filt_w1, filt_b1=filt_b1, filt_w2=filt_w2,
        filt_b2=filt_b2, filt_w3=filt_w3, filt_b3=filt_b3, filt_w4=filt_w4, filt_freq=filt_freq,
        hyena_skip=hyena_skip, hyena_out_norm=hyena_out_norm,
        hgrn_lower_bounds=hgrn_lower_bounds, hgrn_out_norm=hgrn_out_norm,
        ffn_conv_w=ffn_conv_w, ffn_conv_b=ffn_conv_b,
        w_in_bf16=w_in.astype(BF16), w_out_bf16=w_out.astype(BF16),
        ffn_w_up_bf16=ffn_w_up.astype(BF16), ffn_w_down_bf16=ffn_w_down.astype(BF16),
    )
    return (_trunk(x_prompt, p), _trunk(x_sample, p))
```

```python
import functools
import math

import numpy as np
import jax
import jax.numpy as jnp
from jax import lax
from jax.experimental import pallas as pl
from jax.experimental.pallas import tpu as pltpu

F32 = jnp.float32
BF16 = jnp.bfloat16
HIGHEST = lax.Precision.HIGHEST

D_MODEL = 1024
D_HYENA = 512
HYENA_GROUPS = 8
D_HGRN = 512
HGRN_HEADS = 4
HEAD_DIM = 128
D_IN = 3 * D_HYENA + 5 * D_HGRN
D_FF = 4 * D_MODEL
FILTER_EMB = 33
FILTER_BANDS = 16
FILTER_HIDDEN = 64
DECAY_TARGET = 1e-2
FAST_DECAY_PCT = 0.3
SLOW_DECAY_PCT = 1.5
EPS = 1e-6

SUBLANES = 8
LANES = 128
BF16_ROWS = 16
VMEM_LIMIT_BYTES = 56 * 1024 * 1024

FFT_FAST = 128
FFT_KS_GROUP_MAX = 20
FFT_F_TILE = 16
FFN_HALO = 16
SCAN_CHUNK = 128
SCAN_CHUNKS_PER_STEP = 4
SCAN_LEVELS = (64, 32, 16, 8, 4, 2, 1)


def _params(*semantics):
    return pltpu.CompilerParams(dimension_semantics=semantics,
                                vmem_limit_bytes=VMEM_LIMIT_BYTES)


def _rms_scale(x):
    return lax.rsqrt(jnp.mean(x * x, axis=-1, keepdims=True) + EPS)


def _group_sums(sq, same_group):
    return jnp.dot(sq.astype(BF16), same_group, preferred_element_type=F32)


def _in_proj_kernel(x_ref, g_ref, w_ref, o_ref, h_ref):
    @pl.when(pl.program_id(1) == 0)
    def _():
        x = x_ref[...]
        h_ref[...] = (x * _rms_scale(x) * g_ref[...]).astype(BF16)

    o_ref[...] = jnp.dot(h_ref[...], w_ref[...], preferred_element_type=F32).astype(o_ref.dtype)


def _in_proj(x2d, gain, w_bf16, layer, tm=1024, tn=4096):
    T = x2d.shape[0]
    N = w_bf16.shape[2]
    return pl.pallas_call(
        _in_proj_kernel,
        grid=(T // tm, N // tn),
        in_specs=[
            pl.BlockSpec((tm, D_MODEL), lambda i, j: (i, 0)),
            pl.BlockSpec((1, D_MODEL), lambda i, j: (0, 0)),
            pl.BlockSpec((None, D_MODEL, tn), lambda i, j: (layer, 0, j)),
        ],
        out_specs=pl.BlockSpec((tm, tn), lambda i, j: (i, j)),
        out_shape=jax.ShapeDtypeStruct((T, N), BF16),
        scratch_shapes=[pltpu.VMEM((tm, D_MODEL), BF16)],
        compiler_params=_params("parallel", "arbitrary"),
    )(x2d, gain.reshape(1, D_MODEL), w_bf16)


def _shift_rows(main, prev_row, next_row):
    tm = main.shape[0]
    rows = lax.broadcasted_iota(jnp.int32, (tm, 1), 0)
    down = jnp.where(rows == 0, prev_row, pltpu.roll(main, 1, 0))
    up = jnp.where(rows == tm - 1, next_row, pltpu.roll(main, tm - 1, 0))
    return down, up


def _hyena_pre_kernel(prev_ref, main_ref, next_ref, w_ref, b_ref, x0_ref, z_ref,
                      *, tiles_per_seq):
    pos = pl.program_id(0) % tiles_per_seq
    main = main_ref[...].astype(F32)
    prev_row = jnp.where(pos == 0, 0.0, prev_ref[...].astype(F32)[BF16_ROWS - 1:BF16_ROWS, :])
    next_row = jnp.where(pos == tiles_per_seq - 1, 0.0, next_ref[...].astype(F32)[0:1, :])
    down, up = _shift_rows(main, prev_row, next_row)
    w = w_ref[...]
    u = down * w[0:1, :] + main * w[1:2, :] + up * w[2:3, :] + b_ref[...]
    x0_ref[...] = u[:, :D_HYENA].astype(x0_ref.dtype)
    z_ref[...] = u[:, D_HYENA:2 * D_HYENA] * u[:, 2 * D_HYENA:]


def _hyena_pre(proj, conv_w, conv_b, seq_len, tm=512):
    T = proj.shape[0]
    width = 3 * D_HYENA
    halo_per_tile = tm // BF16_ROWS
    last_halo = T // BF16_ROWS - 1
    return pl.pallas_call(
        functools.partial(_hyena_pre_kernel, tiles_per_seq=seq_len // tm),
        grid=(T // tm,),
        in_specs=[
            pl.BlockSpec((BF16_ROWS, width),
                         lambda i: (jnp.maximum(i * halo_per_tile - 1, 0), 0)),
            pl.BlockSpec((tm, width), lambda i: (i, 0)),
            pl.BlockSpec((BF16_ROWS, width),
                         lambda i: (jnp.minimum((i + 1) * halo_per_tile, last_halo), 0)),
            pl.BlockSpec((3, width), lambda i: (0, 0)),
            pl.BlockSpec((1, width), lambda i: (0, 0)),
        ],
        out_specs=[pl.BlockSpec((tm, D_HYENA), lambda i: (i, 0)),
                   pl.BlockSpec((tm, D_HYENA), lambda i: (i, 0))],
        out_shape=[jax.ShapeDtypeStruct((T, D_HYENA), BF16),
                   jax.ShapeDtypeStruct((T, D_HYENA), F32)],
        compiler_params=_params("parallel"),
    )(proj, proj, proj, conv_w, conv_b.reshape(1, width))


def _filter_kernel(band_ref, w1t_ref, w1c_ref, w1s_ref, b1_ref, w2_ref, b2_ref, w3_ref, b3_ref,
                   w4_ref, freq_ref, delta_ref, o_ref, *, seq_len, tr):
    row0 = pl.program_id(0) * tr

    def tap_index(shape, axis):
        j = row0 + lax.broadcasted_iota(jnp.int32, shape, axis)
        return j, jnp.where(j < seq_len, j, 2 * seq_len - j).astype(F32)

    _, idx = tap_index((1, tr), 1)
    t = idx * (1.0 / (seq_len - 1))
    arg = band_ref[...] * ((2.0 * math.pi / seq_len) * idx)
    fr = freq_ref[...]

    def dense(w_ref_, a):
        return jnp.dot(w_ref_[...], a, precision=HIGHEST, preferred_element_type=F32)

    h = w1t_ref[...] * t + dense(w1c_ref, jnp.cos(arg)) - dense(w1s_ref, jnp.sin(arg))
    h = jnp.sin(fr * (h + b1_ref[...]))
    h = jnp.sin(fr * (dense(w2_ref, h) + b2_ref[...]))
    h = jnp.sin(fr * (dense(w3_ref, h) + b3_ref[...]))
    out = lax.dot_general(h.astype(BF16), w4_ref[...].astype(BF16), (((0,), (0,)), ((), ())),
                          preferred_element_type=F32)
    j_col, idx_col = tap_index((tr, 1), 0)
    window = jnp.exp(-(idx_col * (1.0 / (seq_len - 1))) * delta_ref[...])
    o_ref[...] = jnp.where(j_col == seq_len, 0.0, out * window)


def _hyena_filter(seq_len, w1, b1, w2, b2, w3, b3, w4, freq, tr=1024):
    n_taps = 2 * seq_len
    bands = np.linspace(1e-4, FILTER_BANDS - 1, FILTER_BANDS, dtype=np.float32).reshape(-1, 1)
    deltas = np.abs(np.linspace(math.log(DECAY_TARGET) / SLOW_DECAY_PCT,
                                math.log(DECAY_TARGET) / FAST_DECAY_PCT, D_HYENA,
                                dtype=np.float32)).reshape(1, D_HYENA)
    tiles_fwd = seq_len // tr
    const = lambda i: (0, 0)
    col = lambda v: v.reshape(-1, 1)
    full = lambda a: pl.BlockSpec(a.shape, const)
    operands = [jnp.asarray(bands), col(w1[0]), w1[1:1 + FILTER_BANDS].T,
                w1[1 + FILTER_BANDS:].T, col(b1), w2.T, col(b2), w3.T, col(b3)]
    return pl.pallas_call(
        functools.partial(_filter_kernel, seq_len=seq_len, tr=tr),
        grid=(n_taps // tr,),
        in_specs=[full(a) for a in operands] + [
            pl.BlockSpec((FILTER_HIDDEN, D_HYENA), lambda i: (0, i // tiles_fwd)),
            pl.BlockSpec((FILTER_HIDDEN, 1), const),
            pl.BlockSpec((1, D_HYENA), const),
        ],
        out_specs=pl.BlockSpec((tr, D_HYENA), lambda i: (i, 0)),
        out_shape=jax.ShapeDtypeStruct((n_taps, D_HYENA), F32),
        compiler_params=_params("parallel"),
    )(*operands, w4, col(freq), jnp.asarray(deltas))


class _FftPlan:
    def __init__(self, seq_len):
        self.n = 2 * seq_len
        self.fast = FFT_FAST
        self.slow = self.n // FFT_FAST
        self.slow_half = self.slow // 2
        self.ks = self.slow_half + 1
        self.ks_pad = -(-self.ks // SUBLANES) * SUBLANES
        self.ks_group = max(g for g in range(1, FFT_KS_GROUP_MAX + 1) if self.ks_pad % g == 0)
        S, F, N = self.slow, self.fast, self.n
        ks = np.arange(self.ks_pad, dtype=np.float64)[:, None]
        valid = (ks < self.ks)

        def stage_a(n_s):
            s = np.arange(n_s, dtype=np.float64)[None, :]
            ang = 2.0 * np.pi * ks * s / S
            return np.concatenate([np.where(valid, np.cos(ang), 0.0),
                                   np.where(valid, -np.sin(ang), 0.0)], axis=0)

        self.a_half = stage_a(self.slow_half).astype(np.float32)
        self.a_full = stage_a(self.slow).astype(np.float32)
        s = np.arange(self.slow_half, dtype=np.float64)[:, None]
        kk = np.arange(self.ks_pad, dtype=np.float64)[None, :]
        weight = np.where((kk == 0) | (kk == self.slow_half), 1.0, 2.0) * (kk < self.ks) / N
        ang = 2.0 * np.pi * s * kk / S
        self.a_inv = np.concatenate([weight * np.cos(ang), -weight * np.sin(ang)],
                                    axis=1).astype(np.float32)
        f = np.arange(F, dtype=np.float64)
        ang = 2.0 * np.pi * np.outer(np.arange(self.ks_pad, dtype=np.float64), f) / N
        self.tw_re = np.cos(ang).reshape(-1, 1).astype(np.float32)
        self.tw_im = (-np.sin(ang)).reshape(-1, 1).astype(np.float32)
        ang = 2.0 * np.pi * np.outer(f, f) / F
        self.f_re = np.cos(ang).astype(np.float32)
        self.f_im = (-np.sin(ang)).astype(np.float32)


def _dft_dot(a_ref, x):
    return jnp.dot(a_ref[...], x.astype(BF16), preferred_element_type=F32)


def _stage_a_kernel(a_ref, x_ref, o_ref, stage_ref, *, batch):
    m, k = a_ref.shape
    x2d = x_ref.reshape(batch * k * FFT_F_TILE, LANES)
    o2d = o_ref.reshape(batch * m * FFT_F_TILE, LANES)
    stage_in = x_ref.dtype != F32
    for b in range(batch):
        x_rows = pl.ds(b * k * FFT_F_TILE, k * FFT_F_TILE)
        o_rows = pl.ds(b * m * FFT_F_TILE, m * FFT_F_TILE)
        if stage_in:
            stage_ref[...] = x2d[x_rows, :].astype(F32)
            slabs = [stage_ref[pl.ds(j, k, stride=FFT_F_TILE), :] for j in range(FFT_F_TILE)]
        else:
            slabs = [x2d[pl.ds(b * k * FFT_F_TILE + j, k, stride=FFT_F_TILE), :]
                     for j in range(FFT_F_TILE)]
        r = _dft_dot(a_ref, jnp.concatenate(slabs, axis=-1))
        for j in range(FFT_F_TILE):
            r_j = r[:, j * LANES:(j + 1) * LANES]
            if stage_in:
                o2d[pl.ds(b * m * FFT_F_TILE + j, m, stride=FFT_F_TILE), :] = r_j
            else:
                stage_ref[pl.ds(j, m, stride=FFT_F_TILE), :] = r_j
        if not stage_in:
            o2d[o_rows, :] = stage_ref[...].astype(o_ref.dtype)


def _stage_a(a, x, inverse):
    F, C = x.shape[-2:]
    B = x.shape[0]
    M, K = a.shape
    x_dims, o_dims = ((2, K // 2), (M,)) if inverse else ((K,), (2, M // 2))

    def spec(dims):
        zeros = (0,) * len(dims)
        return pl.BlockSpec((B,) + dims + (FFT_F_TILE, LANES), lambda f, c: (0,) + zeros + (f, c))

    return pl.pallas_call(
        functools.partial(_stage_a_kernel, batch=B),
        grid=(F // FFT_F_TILE, C // LANES),
        in_specs=[pl.BlockSpec((M, K), lambda f, c: (0, 0)), spec(x_dims)],
        out_specs=spec(o_dims),
        out_shape=jax.ShapeDtypeStruct((B,) + o_dims + (F, C), F32 if inverse else BF16),
        scratch_shapes=[pltpu.VMEM(((K if inverse else M) * FFT_F_TILE, LANES), F32)],
        compiler_params=_params("parallel", "parallel"),
    )(jnp.asarray(a, BF16), x)


def _stage_b_dft(fr_ref, fi_ref, tiles, conj=False):
    w = tiles[0][0].shape[-1]
    both = jnp.concatenate([part for tile in tiles for part in tile], axis=-1)
    pr = _dft_dot(fr_ref, both)
    pi = _dft_dot(fi_ref, both)
    out = []
    for g in range(len(tiles)):
        re, im = slice(2 * g * w, (2 * g + 1) * w), slice((2 * g + 1) * w, (2 * g + 2) * w)
        if conj:
            out.append((pr[:, re] + pi[:, im], pr[:, im] - pi[:, re]))
        else:
            out.append((pr[:, re] - pi[:, im], pr[:, im] + pi[:, re]))
    return out


def _twiddled_tiles(a_ref, twr_ref, twi_ref):
    tiles, twiddles = [], []
    for g in range(a_ref.shape[2] // FFT_FAST):
        rows = slice(g * FFT_FAST, (g + 1) * FFT_FAST)
        ar, ai = a_ref[0, 0, rows, :].astype(F32), a_ref[0, 1, rows, :].astype(F32)
        tr, ti = twr_ref[rows, :], twi_ref[rows, :]
        tiles.append((ar * tr - ai * ti, ar * ti + ai * tr))
        twiddles.append((rows, tr, ti))
    return tiles, twiddles


def _fft_fwd_kernel(a_ref, twr_ref, twi_ref, fr_ref, fi_ref, o_ref):
    tiles, twiddles = _twiddled_tiles(a_ref, twr_ref, twi_ref)
    for (rows, _, _), (xr, xi) in zip(twiddles, _stage_b_dft(fr_ref, fi_ref, tiles)):
        o_ref[0, 0, rows, :] = xr.astype(o_ref.dtype)
        o_ref[0, 1, rows, :] = xi.astype(o_ref.dtype)


def _fft_conv_kernel(a_ref, k_ref, twr_ref, twi_ref, fr_ref, fi_ref, o_ref):
    tiles, twiddles = _twiddled_tiles(a_ref, twr_ref, twi_ref)
    products = []
    for (rows, _, _), (xr, xi) in zip(twiddles, _stage_b_dft(fr_ref, fi_ref, tiles)):
        kr, ki = k_ref[0, 0, rows, :].astype(F32), k_ref[0, 1, rows, :].astype(F32)
        products.append((xr * kr - xi * ki, xr * ki + xi * kr))
    inverse = _stage_b_dft(fr_ref, fi_ref, products, conj=True)
    for (rows, tr, ti), (br, bi) in zip(twiddles, inverse):
        o_ref[0, 0, rows, :] = (br * tr + bi * ti).astype(o_ref.dtype)
        o_ref[0, 1, rows, :] = (bi * tr - br * ti).astype(o_ref.dtype)


def _fft_stage_b(plan, a, k=None):
    B, _, R, C = a.shape
    rows = plan.ks_group * FFT_FAST
    data = pl.BlockSpec((1, 2, rows, LANES), lambda r, c, b: (b, 0, r, c))
    tw = pl.BlockSpec((rows, LANES), lambda r, c, b: (r, 0))
    mat = pl.BlockSpec((FFT_FAST, FFT_FAST), lambda r, c, b: (0, 0))
    lane_bcast = lambda col: jnp.broadcast_to(jnp.asarray(col), (col.shape[0], LANES))
    consts = (lane_bcast(plan.tw_re), lane_bcast(plan.tw_im),
              jnp.asarray(plan.f_re, BF16), jnp.asarray(plan.f_im, BF16))
    common = dict(
        grid=(R // rows, C // LANES, B),
        out_specs=data,
        out_shape=jax.ShapeDtypeStruct(a.shape, BF16),
        compiler_params=_params("parallel", "parallel", "arbitrary"),
    )
    if k is None:
        return pl.pallas_call(_fft_fwd_kernel, in_specs=[data, tw, tw, mat, mat],
                              **common)(a, *consts)
    kspec = pl.BlockSpec((1, 2, rows, LANES), lambda r, c, b: (0, 0, r, c))
    return pl.pallas_call(_fft_conv_kernel, in_specs=[data, kspec, tw, tw, mat, mat],
                          **common)(a, k, *consts)


def _fft_long_conv(z, kern, plan):
    B, L, C = z.shape
    S, F, P = plan.slow, plan.fast, plan.ks_pad
    ka = _stage_a(plan.a_full, kern.reshape(1, S, F, C), inverse=False)
    k_spec = _fft_stage_b(plan, ka.reshape(1, 2, P * F, C))
    za = _stage_a(plan.a_half, z.reshape(B, S // 2, F, C), inverse=False)
    ya = _fft_stage_b(plan, za.reshape(B, 2, P * F, C), k_spec)
    y = _stage_a(plan.a_inv, ya.reshape(B, 2, P, F, C), inverse=True)
    return y.reshape(B, L, C)


def _scan_constants():
    C = SCAN_CHUNK
    n_lv = len(SCAN_LEVELS)
    tri = np.tril(np.ones((C, C), np.float32))
    upper = np.zeros((n_lv, C, 1), np.float32)
    mask = np.zeros((n_lv + 1, C, C), np.float32)
    for li, h in enumerate(SCAN_LEVELS):
        for t in range(C):
            r = (t // (2 * h)) * 2 * h + h
            if t % (2 * h) >= h:
                upper[li, t, 0] = 1.0
                mask[li, t, r - h:r] = 1.0
    mask[n_lv] = np.eye(C, dtype=np.float32)
    flip = lambda m: m[:, ::-1, ::-1]
    return (np.stack([tri, tri[::-1, ::-1]]), np.stack([upper, upper[:, ::-1]]),
            np.stack([mask, flip(mask)]))


def _ref_rows(h, reverse):
    rows = []
    for g in range(SCAN_CHUNK // SUBLANES):
        pair = []
        for t in (g * SUBLANES, g * SUBLANES + SUBLANES // 2):
            start = (t // (2 * h)) * 2 * h
            pair.append(start + h if reverse else start + h - 1)
        rows.append(tuple(pair))
    return rows


def _scan_chunk(q_ref, f_ref, v_ref, lb_table, tri, up_ref, mask_ref, o_ref, b_ref, state_ref,
                *, layer, reverse):
    n_lv = len(SCAN_LEVELS)
    C = SCAN_CHUNK

    e = jnp.exp(lb_table - jnp.max(lb_table, axis=0, keepdims=True))
    prob = e / jnp.sum(e, axis=0, keepdims=True)
    lb = jnp.zeros((1, D_HGRN), F32)
    for l in range(1, layer + 1):
        lb = lb + prob[l:l + 1, :]

    q = q_ref[...].astype(F32)
    q = q / (1.0 + jnp.exp(-q))
    z = f_ref[...]
    one = jnp.ones((), BF16)
    w = jnp.exp(-jnp.abs(z))
    inv = one / (one + w)
    log_sig = jnp.minimum(z, 0) + jnp.log(inv)
    key = ((one - lb.astype(BF16)) * (jnp.where(z >= 0, w, one) * inv)).astype(F32)
    log_a = jnp.log(lb).astype(BF16)
    log_b = jnp.log1p(-lb).astype(BF16) + log_sig
    gate = jnp.maximum(log_a, log_b) + jnp.log(one + jnp.exp(-jnp.abs(log_a - log_b)))
    vb = v_ref[...]

    gate2 = gate * jnp.asarray(math.log2(math.e), BF16)
    b_inc = jnp.dot(tri, gate2, preferred_element_type=F32)
    for hd in range(HGRN_HEADS):
        b_ref[hd] = b_inc[:, hd * HEAD_DIM:(hd + 1) * HEAD_DIM]

    def bcast_row(r):
        return jnp.concatenate([b_ref.at[hd][pl.ds(r, SUBLANES, stride=0), :]
                                for hd in range(HGRN_HEADS)], axis=-1)

    def by_role(h):
        blocks = []
        for start in range(0, C, h):
            upper = (start // h) % 2 == 1
            src = q if upper != reverse else key
            blocks.append(src[start:start + h])
        return jnp.concatenate(blocks, axis=0)

    first_half = lax.broadcasted_iota(jnp.int32, (SUBLANES, 1), 0) < SUBLANES // 2
    scores = [None] * HGRN_HEADS
    for li, h in enumerate(SCAN_LEVELS):
        if h == 1:
            x = jnp.where(up_ref[li] > 0.5, q * (1.0 - key), key)
        else:
            groups = []
            for r0, r1 in _ref_rows(h, reverse):
                ref = bcast_row(r0)
                groups.append(ref if r1 == r0 else jnp.where(first_half, ref, bcast_row(r1)))
            diff = pltpu.bitcast(b_inc - jnp.concatenate(groups, axis=0), jnp.int32)
            neg_dist = pltpu.bitcast(diff | jnp.int32(-2 ** 31), F32)
            roles = by_role(h) if h >= SUBLANES else jnp.where(up_ref[li] > 0.5, q, key)
            x = roles * jnp.exp2(neg_dist)
        xb = x.astype(BF16)
        keep = mask_ref[li] > 0.5
        for hd in range(HGRN_HEADS):
            xh = xb[:, hd * HEAD_DIM:(hd + 1) * HEAD_DIM]
            p = lax.dot_general(xh, xh, (((1,), (1,)), ((), ())), preferred_element_type=F32)
            scores[hd] = jnp.where(keep, p, 0.0 if scores[hd] is None else scores[hd])

    b_total = bcast_row(0 if reverse else C - 1)
    b_rest = jnp.concatenate([b_total] * (C // SUBLANES), axis=0) - b_inc
    q_dec = (q * jnp.exp2(b_inc)).astype(BF16)
    k_dec = (key * jnp.exp2(b_rest)).astype(BF16)
    total = jnp.exp2(b_total[0:1, :])
    qb, kb = q.astype(BF16), key.astype(BF16)
    on_diag = mask_ref[n_lv] > 0.5
    for hd in range(HGRN_HEADS):
        cols = slice(hd * HEAD_DIM, (hd + 1) * HEAD_DIM)
        diag = lax.dot_general(qb[:, cols], kb[:, cols], (((1,), (1,)), ((), ())),
                               preferred_element_type=F32)
        p = jnp.where(on_diag, diag, scores[hd]).astype(BF16)
        st = state_ref[hd]
        o = jnp.dot(p, vb[:, cols], preferred_element_type=F32)
        o = o + lax.dot_general(q_dec[:, cols], st.astype(BF16), (((1,), (1,)), ((), ())),
                                preferred_element_type=F32)
        o_ref[:, cols] = o.astype(o_ref.dtype)
        upd = lax.dot_general(vb[:, cols], k_dec[:, cols], (((0,), (0,)), ((), ())),
                              preferred_element_type=F32)
        state_ref[hd] = st * total[:, cols] + upd


def _hgrn_scan_kernel(qf_ref, ff_ref, vf_ref, qb_ref, fb_ref, vb_ref, lbt_ref, tri_ref, up_ref,
                      mask_ref, of_ref, ob_ref, b_ref, state_ref, *, layer):
    @pl.when(pl.program_id(1) == 0)
    def _():
        state_ref[...] = jnp.zeros_like(state_ref)

    for s in range(SCAN_CHUNKS_PER_STEP):
        rows = pl.ds(s * SCAN_CHUNK, SCAN_CHUNK)
        _scan_chunk(qf_ref.at[rows], ff_ref.at[rows], vf_ref.at[rows], lbt_ref[0], tri_ref[0],
                    up_ref.at[0], mask_ref.at[0], of_ref.at[rows], b_ref.at[0], state_ref.at[0],
                    layer=layer, reverse=False)
        rows = pl.ds((SCAN_CHUNKS_PER_STEP - 1 - s) * SCAN_CHUNK, SCAN_CHUNK)
        _scan_chunk(qb_ref.at[rows], fb_ref.at[rows], vb_ref.at[rows], lbt_ref[1], tri_ref[1],
                    up_ref.at[1], mask_ref.at[1], ob_ref.at[rows], b_ref.at[1], state_ref.at[1],
                    layer=layer, reverse=True)


def _hgrn_scan(proj, lb_table, layer, batch, seq_len):
    T = proj.shape[0]
    C = SCAN_CHUNK
    rows_per_step = SCAN_CHUNKS_PER_STEP * C
    n_chunks = seq_len // rows_per_step
    tri, upper, mask = _scan_constants()
    col0 = (3 * D_HYENA) // D_HGRN

    fwd = lambda b, c: b * n_chunks + c
    bwd = lambda b, c: b * n_chunks + n_chunks - 1 - c
    chunk = lambda rows, col: pl.BlockSpec((rows_per_step, D_HGRN), lambda b, c: (rows(b, c), col))
    whole = lambda a: pl.BlockSpec(a.shape, lambda b, c: (0,) * a.ndim)
    out = jax.ShapeDtypeStruct((T, D_HGRN), BF16)
    return pl.pallas_call(
        functools.partial(_hgrn_scan_kernel, layer=layer),
        grid=(batch, n_chunks),
        in_specs=[
            chunk(fwd, col0), chunk(fwd, col0 + 1), chunk(fwd, col0 + 3),
            chunk(bwd, col0), chunk(bwd, col0 + 2), chunk(bwd, col0 + 3),
            whole(lb_table), whole(tri), whole(upper), whole(mask),
        ],
        out_specs=[chunk(fwd, 0), chunk(bwd, 0)],
        out_shape=[out, out],
        scratch_shapes=[pltpu.VMEM((2, HGRN_HEADS, C, HEAD_DIM), F32),
                        pltpu.VMEM((2, HGRN_HEADS, HEAD_DIM, HEAD_DIM), F32)],
        compiler_params=_params("parallel", "arbitrary"),
    )(proj, proj, proj, proj, proj, proj, lb_table, jnp.asarray(tri, BF16), jnp.asarray(upper),
      jnp.asarray(mask))


def _group_mean_matrix(group):
    idx = np.arange(D_HYENA) // group
    return (idx[:, None] == idx[None, :]).astype(np.float32)


def _mix_out_kernel(x_ref, y_ref, z_ref, x0_ref, of_ref, ob_ref, g_ref, skip_ref,
                    hy_gain_ref, hg_gain_ref, grp_hy_ref, grp_hg_ref, w_ref, post_ref, o_ref):
    z = z_ref[...]
    yh = x0_ref[...].astype(F32) * (y_ref[...] + skip_ref[...] * z)
    ms = _group_sums(yh * yh, grp_hy_ref[...]) * (HYENA_GROUPS / D_HYENA)
    yh = yh * lax.rsqrt(ms + EPS) * hy_gain_ref[...]
    o = of_ref[...].astype(F32) + ob_ref[...].astype(F32)
    ms = _group_sums(o * o, grp_hg_ref[...]) * (1.0 / HEAD_DIM)
    g = g_ref[...].astype(F32)
    o = o * lax.rsqrt(ms + EPS) * hg_gain_ref[...] * (g * (1.0 / (1.0 + jnp.exp(-g))))
    mix = (jnp.dot(yh.astype(BF16), w_ref[:D_HYENA, :], preferred_element_type=F32)
           + jnp.dot(o.astype(BF16), w_ref[D_HYENA:, :], preferred_element_type=F32))
    o_ref[...] = x_ref[...] + mix * _rms_scale(mix) * post_ref[...]


def _mix_out(x2d, y, z, x0, o_fwd, o_bwd, proj, skip, hy_gain, hg_gain, w_bf16, layer,
             post_gain, tm=1024):
    T = x2d.shape[0]
    gate_col = D_IN // D_HGRN - 1
    half = lambda: pl.BlockSpec((tm, D_HYENA), lambda i: (i, 0))
    vec = lambda n: pl.BlockSpec((1, n), lambda i: (0, 0))
    sq = lambda: pl.BlockSpec((D_HYENA, D_HYENA), lambda i: (0, 0))
    row = lambda v: v.reshape(1, -1)
    return pl.pallas_call(
        _mix_out_kernel,
        grid=(T // tm,),
        in_specs=[
            pl.BlockSpec((tm, D_MODEL), lambda i: (i, 0)),
            half(), half(), half(), half(), half(),
            pl.BlockSpec((tm, D_HGRN), lambda i: (i, gate_col)),
            vec(D_HYENA), vec(D_HYENA), vec(D_HGRN), sq(), sq(),
            pl.BlockSpec((None, D_MODEL, D_MODEL), lambda i: (layer, 0, 0)),
            vec(D_MODEL),
        ],
        out_specs=pl.BlockSpec((tm, D_MODEL), lambda i: (i, 0)),
        out_shape=jax.ShapeDtypeStruct((T, D_MODEL), F32),
        compiler_params=_params("parallel"),
    )(x2d, y, z, x0, o_fwd, o_bwd, proj, row(skip), row(hy_gain), row(hg_gain),
      jnp.asarray(_group_mean_matrix(D_HYENA // HYENA_GROUPS), BF16),
      jnp.asarray(_group_mean_matrix(HEAD_DIM), BF16), w_bf16, row(post_gain))


def _gelu_tanh(x):
    c = -2.0 * math.sqrt(2.0 / math.pi) * math.log2(math.e)
    return x / (1.0 + jnp.exp2(x * (c + (0.044715 * c) * (x * x))))


def _ffn_kernel(prev_ref, main_ref, next_ref, pre_ref, wa_ref, wb_ref, cwa_ref, cwb_ref,
                cba_ref, cbb_ref, wd_ref, post_ref, o_ref, h_ref, acc_ref, ua_ref, ub_ref,
                *, tiles_per_seq, tm):
    j = pl.program_id(1)
    pos = pl.program_id(0) % tiles_per_seq

    @pl.when(j == 0)
    def _():
        gain = pre_ref[...]

        def normed(x):
            return (x * _rms_scale(x) * gain).astype(BF16)

        prev = normed(prev_ref[...])
        nxt = normed(next_ref[...])
        h_ref[0:FFN_HALO, :] = jnp.where(pos == 0, jnp.zeros_like(prev), prev)
        h_ref[FFN_HALO:FFN_HALO + tm, :] = normed(main_ref[...])
        h_ref[FFN_HALO + tm:, :] = jnp.where(pos == tiles_per_seq - 1, jnp.zeros_like(nxt), nxt)
        acc_ref[...] = jnp.zeros_like(acc_ref)

    h = h_ref[...]

    def conv_part(w_ref_, cw_ref_, cb_ref_, u_ref):
        u_ref[...] = jnp.dot(h, w_ref_[...], preferred_element_type=F32)
        cw = cw_ref_[...]
        down = u_ref[pl.ds(FFN_HALO - 1, tm), :]
        mid = u_ref[pl.ds(FFN_HALO, tm), :]
        up = u_ref[pl.ds(FFN_HALO + 1, tm), :]
        return down * cw[0:1, :] + mid * cw[1:2, :] + up * cw[2:3, :] + cb_ref_[...]

    a = conv_part(wa_ref, cwa_ref, cba_ref, ua_ref)
    b = conv_part(wb_ref, cwb_ref, cbb_ref, ub_ref)
    act = (_gelu_tanh(a) * b).astype(BF16)
    acc_ref[...] += jnp.dot(act, wd_ref[...], preferred_element_type=F32)

    @pl.when(j == pl.num_programs(1) - 1)
    def _():
        ff = acc_ref[...]
        o_ref[...] = main_ref[...] + ff * _rms_scale(ff) * post_ref[...]


def _ffn(x2d, pre_gain, w_up_bf16, conv_w, conv_b, w_down_bf16, layer, post_gain, seq_len,
         tm=1024, tf=1024):
    T = x2d.shape[0]
    conv_b = conv_b.reshape(conv_b.shape[0], 1, -1)
    halo_per_tile = tm // FFN_HALO
    last_halo = T // FFN_HALO - 1
    n_f = D_FF // tf
    row = lambda v: v.reshape(1, -1)
    return pl.pallas_call(
        functools.partial(_ffn_kernel, tiles_per_seq=seq_len // tm, tm=tm),
        grid=(T // tm, n_f),
        in_specs=[
            pl.BlockSpec((FFN_HALO, D_MODEL),
                         lambda i, j: (jnp.maximum(i * halo_per_tile - 1, 0), 0)),
            pl.BlockSpec((tm, D_MODEL), lambda i, j: (i, 0)),
            pl.BlockSpec((FFN_HALO, D_MODEL),
                         lambda i, j: (jnp.minimum((i + 1) * halo_per_tile, last_halo), 0)),
            pl.BlockSpec((1, D_MODEL), lambda i, j: (0, 0)),
            pl.BlockSpec((None, D_MODEL, tf), lambda i, j: (layer, 0, j)),
            pl.BlockSpec((None, D_MODEL, tf), lambda i, j: (layer, 0, n_f + j)),
            pl.BlockSpec((None, 3, tf), lambda i, j: (layer, 0, j)),
            pl.BlockSpec((None, 3, tf), lambda i, j: (layer, 0, n_f + j)),
            pl.BlockSpec((None, 1, tf), lambda i, j: (layer, 0, j)),
            pl.BlockSpec((None, 1, tf), lambda i, j: (layer, 0, n_f + j)),
            pl.BlockSpec((None, tf, D_MODEL), lambda i, j: (layer, j, 0)),
            pl.BlockSpec((1, D_MODEL), lambda i, j: (0, 0)),
        ],
        out_specs=pl.BlockSpec((tm, D_MODEL), lambda i, j: (i, 0)),
        out_shape=jax.ShapeDtypeStruct((T, D_MODEL), F32),
        scratch_shapes=[pltpu.VMEM((tm + 2 * FFN_HALO, D_MODEL), BF16),
                        pltpu.VMEM((tm, D_MODEL), F32)]
        + [pltpu.VMEM((tm + 2 * FFN_HALO, tf), F32)] * 2,
        compiler_params=_params("parallel", "arbitrary"),
    )(x2d, x2d, x2d, row(pre_gain), w_up_bf16, w_up_bf16, conv_w, conv_w,
      conv_b, conv_b, w_down_bf16, row(post_gain))


def _trunk(x, p):
    B, L, _ = x.shape
    x2d = x.reshape(B * L, D_MODEL)
    plan = _FftPlan(L)
    for l in range(p["w_in"].shape[0]):
        proj = _in_proj(x2d, p["norm_mix_pre"][l], p["w_in_bf16"], l)
        x0, z = _hyena_pre(proj, p["hyena_conv_w"][l], p["hyena_conv_b"][l], L)
        kern = _hyena_filter(L, p["filt_w1"][l], p["filt_b1"][l], p["filt_w2"][l], p["filt_b2"][l],
                             p["filt_w3"][l], p["filt_b3"][l], p["filt_w4"][l], p["filt_freq"][l])
        y = _fft_long_conv(z.reshape(B, L, D_HYENA), kern, plan).reshape(B * L, D_HYENA)
        o_fwd, o_bwd = _hgrn_scan(proj, p["hgrn_lower_bounds"], l, B, L)
        x2d = _mix_out(x2d, y, z, x0, o_fwd, o_bwd, proj, p["hyena_skip"][l], p["hyena_out_norm"][l],
                       p["hgrn_out_norm"][l], p["w_out_bf16"], l, p["norm_mix_post"][l])
        x2d = _ffn(x2d, p["norm_ffn_pre"][l], p["ffn_w_up_bf16"], p["ffn_conv_w"],
                   p["ffn_conv_b"], p["ffn_w_down_bf16"], l, p["norm_ffn_post"][l], L)
    return x2d.reshape(B, L, D_MODEL)


def kernel(x_prompt, x_sample, norm_mix_pre, norm_mix_post, norm_ffn_pre, norm_ffn_post, w_in, hyena_conv_w, hyena_conv_b, filt_w1, filt_b1, filt_w2, filt_b2, filt_w3, filt_b3, filt_w4, filt_freq, hyena_skip, hyena_out_norm, hgrn_lower_bounds, hgrn_out_norm, w_out, ffn_w_up, ffn_conv_w, ffn_conv_b, ffn_w_down):
    p = dict(
        norm_mix_pre=norm_mix_pre, norm_mix_post=norm_mix_post, norm_ffn_pre=norm_ffn_pre,
        norm_ffn_post=norm_ffn_post, w_in=w_in, hyena_conv_w=hyena_conv_w,
        hyena_conv_b=hyena_conv_b, filt_w1=filt_w1, filt_b1=filt_b1, filt_w2=filt_w2,
        filt_b2=filt_b2, filt_w3=filt_w3, filt_b3=filt_b3, filt_w4=filt_w4, filt_freq=filt_freq,
        hyena_skip=hyena_skip, hyena_out_norm=hyena_out_norm,
        hgrn_lower_bounds=hgrn_lower_bounds, hgrn_out_norm=hgrn_out_norm,
        ffn_conv_w=ffn_conv_w, ffn_conv_b=ffn_conv_b,
        w_in_bf16=w_in.astype(BF16), w_out_bf16=w_out.astype(BF16),
        ffn_w_up_bf16=ffn_w_up.astype(BF16), ffn_w_down_bf16=ffn_w_down.astype(BF16),
    )
    return (_trunk(x_prompt, p), _trunk(x_sample, p))
```

```python
import functools
import math

import numpy as np
import jax
import jax.numpy as jnp
from jax import lax
from jax.experimental import pallas as pl
from jax.experimental.pallas import tpu as pltpu

F32 = jnp.float32
BF16 = jnp.bfloat16
HIGHEST = lax.Precision.HIGHEST

D_MODEL = 1024
D_HYENA = 512
HYENA_GROUPS = 8
D_HGRN = 512
HGRN_HEADS = 4
HEAD_DIM = 128
D_IN = 3 * D_HYENA + 5 * D_HGRN
D_FF = 4 * D_MODEL
FILTER_EMB = 33
FILTER_BANDS = 16
FILTER_HIDDEN = 64
DECAY_TARGET = 1e-2
FAST_DECAY_PCT = 0.3
SLOW_DECAY_PCT = 1.5
EPS = 1e-6

SUBLANES = 8
LANES = 128
BF16_ROWS = 16
VMEM_LIMIT_BYTES = 56 * 1024 * 1024

FFT_FAST = 128
FFT_KS_GROUP_MAX = 20
FFT_F_TILE = 16
FFT_B_LANES = 256
FFN_HALO = 16
SCAN_CHUNK = 128
SCAN_CHUNKS_PER_STEP = 4
SCAN_LEVELS = (64, 32, 16, 8, 4, 2, 1)


def _params(*semantics):
    return pltpu.CompilerParams(dimension_semantics=semantics,
                                vmem_limit_bytes=VMEM_LIMIT_BYTES)


def _rms_scale(x):
    return lax.rsqrt(jnp.mean(x * x, axis=-1, keepdims=True) + EPS)


def _group_sums(sq, same_group):
    return jnp.dot(sq.astype(BF16), same_group, preferred_element_type=F32)


def _in_proj_kernel(x_ref, g_ref, w_ref, o_ref, h_ref):
    @pl.when(pl.program_id(1) == 0)
    def _():
        x = x_ref[...]
        h_ref[...] = (x * _rms_scale(x) * g_ref[...]).astype(BF16)

    o_ref[...] = jnp.dot(h_ref[...], w_ref[...], preferred_element_type=F32).astype(o_ref.dtype)


def _in_proj(x2d, gain, w_bf16, layer, tm=1024, tn=4096):
    T = x2d.shape[0]
    N = w_bf16.shape[2]
    return pl.pallas_call(
        _in_proj_kernel,
        grid=(T // tm, N // tn),
        in_specs=[
            pl.BlockSpec((tm, D_MODEL), lambda i, j: (i, 0)),
            pl.BlockSpec((1, D_MODEL), lambda i, j: (0, 0)),
            pl.BlockSpec((None, D_MODEL, tn), lambda i, j: (layer, 0, j)),
        ],
        out_specs=pl.BlockSpec((tm, tn), lambda i, j: (i, j)),
        out_shape=jax.ShapeDtypeStruct((T, N), BF16),
        scratch_shapes=[pltpu.VMEM((tm, D_MODEL), BF16)],
        compiler_params=_params("parallel", "arbitrary"),
    )(x2d, gain.reshape(1, D_MODEL), w_bf16)


def _shift_rows(main, prev_row, next_row):
    tm = main.shape[0]
    rows = lax.broadcasted_iota(jnp.int32, (tm, 1), 0)
    down = jnp.where(rows == 0, prev_row, pltpu.roll(main, 1, 0))
    up = jnp.where(rows == tm - 1, next_row, pltpu.roll(main, tm - 1, 0))
    return down, up


def _hyena_pre_kernel(prev_ref, main_ref, next_ref, w_ref, b_ref, x0_ref, z_ref,
                      *, tiles_per_seq):
    pos = pl.program_id(0) % tiles_per_seq
    main = main_ref[...].astype(F32)
    prev_row = jnp.where(pos == 0, 0.0, prev_ref[...].astype(F32)[BF16_ROWS - 1:BF16_ROWS, :])
    next_row = jnp.where(pos == tiles_per_seq - 1, 0.0, next_ref[...].astype(F32)[0:1, :])
    down, up = _shift_rows(main, prev_row, next_row)
    w = w_ref[...]
    u = down * w[0:1, :] + main * w[1:2, :] + up * w[2:3, :] + b_ref[...]
    x0_ref[...] = u[:, :D_HYENA].astype(x0_ref.dtype)
    z_ref[...] = u[:, D_HYENA:2 * D_HYENA] * u[:, 2 * D_HYENA:]


def _hyena_pre(proj, conv_w, conv_b, seq_len, tm=512):
    T = proj.shape[0]
    width = 3 * D_HYENA
    halo_per_tile = tm // BF16_ROWS
    last_halo = T // BF16_ROWS - 1
    return pl.pallas_call(
        functools.partial(_hyena_pre_kernel, tiles_per_seq=seq_len // tm),
        grid=(T // tm,),
        in_specs=[
            pl.BlockSpec((BF16_ROWS, width),
                         lambda i: (jnp.maximum(i * halo_per_tile - 1, 0), 0)),
            pl.BlockSpec((tm, width), lambda i: (i, 0)),
            pl.BlockSpec((BF16_ROWS, width),
                         lambda i: (jnp.minimum((i + 1) * halo_per_tile, last_halo), 0)),
            pl.BlockSpec((3, width), lambda i: (0, 0)),
            pl.BlockSpec((1, width), lambda i: (0, 0)),
        ],
        out_specs=[pl.BlockSpec((tm, D_HYENA), lambda i: (i, 0)),
                   pl.BlockSpec((tm, D_HYENA), lambda i: (i, 0))],
        out_shape=[jax.ShapeDtypeStruct((T, D_HYENA), BF16),
                   jax.ShapeDtypeStruct((T, D_HYENA), F32)],
        compiler_params=_params("parallel"),
    )(proj, proj, proj, conv_w, conv_b.reshape(1, width))


def _filter_kernel(band_ref, w1t_ref, w1c_ref, w1s_ref, b1_ref, w2_ref, b2_ref, w3_ref, b3_ref,
                   w4_ref, freq_ref, delta_ref, rev_ref, of_ref, ob_ref, *, seq_len, tr):
    i = pl.program_id(0)
    ext = tr + LANES

    def tap_index(shape, axis):
        return (i * tr + lax.broadcasted_iota(jnp.int32, shape, axis)).astype(F32)

    idx = tap_index((1, ext), 1)
    t = idx * (1.0 / (seq_len - 1))
    arg = band_ref[...] * ((2.0 * math.pi / seq_len) * idx)
    fr = freq_ref[...]

    def dense(w_ref_, a):
        return jnp.dot(w_ref_[...], a, precision=HIGHEST, preferred_element_type=F32)

    h = w1t_ref[...] * t + dense(w1c_ref, jnp.cos(arg)) - dense(w1s_ref, jnp.sin(arg))
    h = jnp.sin(fr * (h + b1_ref[...]))
    h = jnp.sin(fr * (dense(w2_ref, h) + b2_ref[...]))
    h = jnp.sin(fr * (dense(w3_ref, h) + b3_ref[...])).astype(BF16)
    taps = lax.dot_general(h, w4_ref[...].astype(BF16), (((0,), (0,)), ((), ())),
                           preferred_element_type=F32)
    window = jnp.exp(-(tap_index((ext, 1), 0) * (1.0 / (seq_len - 1))) * delta_ref[...])
    of_ref[...] = taps[:tr, :D_HYENA] * window[:tr]
    bwd = (taps[:, D_HYENA:] * window).astype(BF16)
    rows = lax.broadcasted_iota(jnp.int32, (tr, 1), 0)
    is_tap_l = (rows == 0) & (i == pl.num_programs(0) - 1)
    ob_ref[...] = jnp.where(is_tap_l, 0.0,
                            jnp.dot(rev_ref[...], bwd, preferred_element_type=F32))


def _hyena_filter(seq_len, w1, b1, w2, b2, w3, b3, w4, freq, tr=1024):
    n_tiles = seq_len // tr
    bands = np.linspace(1e-4, FILTER_BANDS - 1, FILTER_BANDS, dtype=np.float32).reshape(-1, 1)
    deltas = np.abs(np.linspace(math.log(DECAY_TARGET) / SLOW_DECAY_PCT,
                                math.log(DECAY_TARGET) / FAST_DECAY_PCT, D_HYENA,
                                dtype=np.float32)).reshape(1, D_HYENA)
    reverse = (np.arange(tr)[:, None] + np.arange(tr + LANES)[None, :] == tr).astype(np.float32)
    const = lambda i: (0, 0)
    col = lambda v: v.reshape(-1, 1)
    full = lambda a: pl.BlockSpec(a.shape, const)
    operands = [jnp.asarray(bands), col(w1[0]), w1[1:1 + FILTER_BANDS].T,
                w1[1 + FILTER_BANDS:].T, col(b1), w2.T, col(b2), w3.T, col(b3),
                w4, col(freq), jnp.asarray(deltas), jnp.asarray(reverse, BF16)]
    out = jax.ShapeDtypeStruct((seq_len, D_HYENA), F32)
    return pl.pallas_call(
        functools.partial(_filter_kernel, seq_len=seq_len, tr=tr),
        grid=(n_tiles,),
        in_specs=[full(a) for a in operands],
        out_specs=[pl.BlockSpec((tr, D_HYENA), lambda i: (i, 0)),
                   pl.BlockSpec((tr, D_HYENA), lambda i: (n_tiles - 1 - i, 0))],
        out_shape=[out, out],
        compiler_params=_params("parallel"),
    )(*operands)


class _FftPlan:
    def __init__(self, seq_len):
        self.n = 2 * seq_len
        self.fast = FFT_FAST
        self.slow = self.n // FFT_FAST
        self.slow_half = self.slow // 2
        self.ks = self.slow_half + 1
        self.ks_pad = -(-self.ks // SUBLANES) * SUBLANES
        self.ks_group = max(g for g in range(1, FFT_KS_GROUP_MAX + 1) if self.ks_pad % g == 0)
        S, F, N = self.slow, self.fast, self.n
        ks = np.arange(self.ks_pad, dtype=np.float64)[:, None]
        valid = (ks < self.ks)

        def stage_a(n_s):
            s = np.arange(n_s, dtype=np.float64)[None, :]
            ang = 2.0 * np.pi * ks * s / S
            return np.concatenate([np.where(valid, np.cos(ang), 0.0),
                                   np.where(valid, -np.sin(ang), 0.0)], axis=0)

        self.a_half = stage_a(self.slow_half).astype(np.float32)
        self.a_full = stage_a(self.slow).astype(np.float32)
        s = np.arange(self.slow_half, dtype=np.float64)[:, None]
        kk = np.arange(self.ks_pad, dtype=np.float64)[None, :]
        weight = np.where((kk == 0) | (kk == self.slow_half), 1.0, 2.0) * (kk < self.ks) / N
        ang = 2.0 * np.pi * s * kk / S
        self.a_inv = np.concatenate([weight * np.cos(ang), -weight * np.sin(ang)],
                                    axis=1).astype(np.float32)
        f = np.arange(F, dtype=np.float64)
        ang = 2.0 * np.pi * np.outer(np.arange(self.ks_pad, dtype=np.float64), f) / N
        self.tw_re = np.cos(ang).reshape(-1, 1).astype(np.float32)
        self.tw_im = (-np.sin(ang)).reshape(-1, 1).astype(np.float32)
        ang = 2.0 * np.pi * np.outer(f, f) / F
        self.f_re = np.cos(ang).astype(np.float32)
        self.f_im = (-np.sin(ang)).astype(np.float32)


def _dft_dot(a_ref, x):
    return jnp.dot(a_ref[...], x.astype(BF16), preferred_element_type=F32)


def _stage_a_kernel(a_ref, *refs, batch):
    x_refs, o_ref, stage_ref = refs[:-2], refs[-2], refs[-1]
    m, k = a_ref.shape
    kp = k // len(x_refs)
    x2ds = [x_ref.reshape(batch * kp * FFT_F_TILE, LANES) for x_ref in x_refs]
    x2d = x2ds[0]
    o2d = o_ref.reshape(batch * m * FFT_F_TILE, LANES)
    stage_in = x_refs[0].dtype != F32
    for b in range(batch):
        x_rows = pl.ds(b * k * FFT_F_TILE, k * FFT_F_TILE)
        o_rows = pl.ds(b * m * FFT_F_TILE, m * FFT_F_TILE)
        if stage_in:
            stage_ref[...] = x2d[x_rows, :].astype(F32)
            slabs = [stage_ref[pl.ds(j, k, stride=FFT_F_TILE), :] for j in range(FFT_F_TILE)]
        else:
            slabs = [jnp.concatenate(
                [part[pl.ds(b * kp * FFT_F_TILE + j, kp, stride=FFT_F_TILE), :] for part in x2ds],
                axis=0) for j in range(FFT_F_TILE)]
        r = _dft_dot(a_ref, jnp.concatenate(slabs, axis=-1))
        for j in range(FFT_F_TILE):
            r_j = r[:, j * LANES:(j + 1) * LANES]
            if stage_in:
                o2d[pl.ds(b * m * FFT_F_TILE + j, m, stride=FFT_F_TILE), :] = r_j
            else:
                stage_ref[pl.ds(j, m, stride=FFT_F_TILE), :] = r_j
        if not stage_in:
            o2d[o_rows, :] = stage_ref[...].astype(o_ref.dtype)


def _stage_a(a, xs, inverse):
    F, C = xs[0].shape[-2:]
    B = xs[0].shape[0]
    M, K = a.shape
    x_dims, o_dims = ((2, K // 2), (M,)) if inverse else ((K // len(xs),), (2, M // 2))

    def spec(dims):
        zeros = (0,) * len(dims)
        return pl.BlockSpec((B,) + dims + (FFT_F_TILE, LANES), lambda f, c: (0,) + zeros + (f, c))

    return pl.pallas_call(
        functools.partial(_stage_a_kernel, batch=B),
        grid=(F // FFT_F_TILE, C // LANES),
        in_specs=[pl.BlockSpec((M, K), lambda f, c: (0, 0))] + [spec(x_dims)] * len(xs),
        out_specs=spec(o_dims),
        out_shape=jax.ShapeDtypeStruct((B,) + o_dims + (F, C), F32 if inverse else BF16),
        scratch_shapes=[pltpu.VMEM(((K if inverse else M) * FFT_F_TILE, LANES), F32)],
        compiler_params=_params("parallel", "parallel"),
    )(jnp.asarray(a, BF16), *xs)


def _stage_b_dft(fr_ref, fi_ref, tiles, conj=False):
    w = tiles[0][0].shape[-1]
    both = jnp.concatenate([part for tile in tiles for part in tile], axis=-1)
    pr = _dft_dot(fr_ref, both)
    pi = _dft_dot(fi_ref, both)
    out = []
    for g in range(len(tiles)):
        re, im = slice(2 * g * w, (2 * g + 1) * w), slice((2 * g + 1) * w, (2 * g + 2) * w)
        if conj:
            out.append((pr[:, re] + pi[:, im], pr[:, im] - pi[:, re]))
        else:
            out.append((pr[:, re] - pi[:, im], pr[:, im] + pi[:, re]))
    return out


def _twiddled_tiles(a_ref, twr_ref, twi_ref):
    tiles, twiddles = [], []
    for g in range(a_ref.shape[2] // FFT_FAST):
        rows = slice(g * FFT_FAST, (g + 1) * FFT_FAST)
        tr, ti = twr_ref[rows, :], twi_ref[rows, :]
        for c in range(a_ref.shape[3] // LANES):
            at = rows, slice(c * LANES, (c + 1) * LANES)
            ar, ai = a_ref[(0, 0) + at].astype(F32), a_ref[(0, 1) + at].astype(F32)
            tiles.append((ar * tr - ai * ti, ar * ti + ai * tr))
            twiddles.append((at, tr, ti))
    return tiles, twiddles


def _fft_fwd_kernel(a_ref, twr_ref, twi_ref, fr_ref, fi_ref, o_ref):
    tiles, twiddles = _twiddled_tiles(a_ref, twr_ref, twi_ref)
    for (at, _, _), (xr, xi) in zip(twiddles, _stage_b_dft(fr_ref, fi_ref, tiles)):
        o_ref[(0, 0) + at] = xr.astype(o_ref.dtype)
        o_ref[(0, 1) + at] = xi.astype(o_ref.dtype)


def _fft_conv_kernel(a_ref, k_ref, twr_ref, twi_ref, fr_ref, fi_ref, o_ref):
    tiles, twiddles = _twiddled_tiles(a_ref, twr_ref, twi_ref)
    products = []
    for (at, _, _), (xr, xi) in zip(twiddles, _stage_b_dft(fr_ref, fi_ref, tiles)):
        kr, ki = k_ref[(0, 0) + at].astype(F32), k_ref[(0, 1) + at].astype(F32)
        products.append((xr * kr - xi * ki, xr * ki + xi * kr))
    inverse = _stage_b_dft(fr_ref, fi_ref, products, conj=True)
    for (at, tr, ti), (br, bi) in zip(twiddles, inverse):
        o_ref[(0, 0) + at] = (br * tr + bi * ti).astype(o_ref.dtype)
        o_ref[(0, 1) + at] = (bi * tr - br * ti).astype(o_ref.dtype)


def _fft_stage_b(plan, a, k=None):
    B, _, R, C = a.shape
    rows = plan.ks_group * FFT_FAST
    data = pl.BlockSpec((1, 2, rows, FFT_B_LANES), lambda r, c, b: (b, 0, r, c))
    tw = pl.BlockSpec((rows, LANES), lambda r, c, b: (r, 0))
    mat = pl.BlockSpec((FFT_FAST, FFT_FAST), lambda r, c, b: (0, 0))
    lane_bcast = lambda col: jnp.broadcast_to(jnp.asarray(col), (col.shape[0], LANES))
    consts = (lane_bcast(plan.tw_re), lane_bcast(plan.tw_im),
              jnp.asarray(plan.f_re, BF16), jnp.asarray(plan.f_im, BF16))
    common = dict(
        grid=(R // rows, C // FFT_B_LANES, B),
        out_specs=data,
        out_shape=jax.ShapeDtypeStruct(a.shape, BF16),
        compiler_params=_params("parallel", "parallel", "arbitrary"),
    )
    if k is None:
        return pl.pallas_call(_fft_fwd_kernel, in_specs=[data, tw, tw, mat, mat],
                              **common)(a, *consts)
    kspec = pl.BlockSpec((1, 2, rows, FFT_B_LANES), lambda r, c, b: (0, 0, r, c))
    return pl.pallas_call(_fft_conv_kernel, in_specs=[data, kspec, tw, tw, mat, mat],
                          **common)(a, k, *consts)


def _fft_long_conv(z, kern_halves, plan):
    B, L, C = z.shape
    S, F, P = plan.slow, plan.fast, plan.ks_pad
    ka = _stage_a(plan.a_full, [h.reshape(1, S // 2, F, C) for h in kern_halves], inverse=False)
    k_spec = _fft_stage_b(plan, ka.reshape(1, 2, P * F, C))
    za = _stage_a(plan.a_half, [z.reshape(B, S // 2, F, C)], inverse=False)
    ya = _fft_stage_b(plan, za.reshape(B, 2, P * F, C), k_spec)
    y = _stage_a(plan.a_inv, [ya.reshape(B, 2, P, F, C)], inverse=True)
    return y.reshape(B, L, C)


def _scan_constants():
    C = SCAN_CHUNK
    n_lv = len(SCAN_LEVELS)
    tri = np.tril(np.ones((C, C), np.float32))
    upper = np.zeros((n_lv, C, 1), np.float32)
    mask = np.zeros((n_lv + 1, C, C), np.float32)
    for li, h in enumerate(SCAN_LEVELS):
        for t in range(C):
            r = (t // (2 * h)) * 2 * h + h
            if t % (2 * h) >= h:
                upper[li, t, 0] = 1.0
                mask[li, t, r - h:r] = 1.0
    mask[n_lv] = np.eye(C, dtype=np.float32)
    flip = lambda m: m[:, ::-1, ::-1]
    return (np.stack([tri, tri[::-1, ::-1]]), np.stack([upper, upper[:, ::-1]]),
            np.stack([mask, flip(mask)]))


def _ref_rows(h, reverse):
    rows = []
    for g in range(SCAN_CHUNK // SUBLANES):
        pair = []
        for t in (g * SUBLANES, g * SUBLANES + SUBLANES // 2):
            start = (t // (2 * h)) * 2 * h
            pair.append(start + h if reverse else start + h - 1)
        rows.append(tuple(pair))
    return rows


def _scan_chunk(q_ref, f_ref, v_ref, lb_table, tri, up_ref, mask_ref, o_ref, b_ref, state_ref,
                *, layer, reverse):
    n_lv = len(SCAN_LEVELS)
    C = SCAN_CHUNK

    e = jnp.exp(lb_table - jnp.max(lb_table, axis=0, keepdims=True))
    prob = e / jnp.sum(e, axis=0, keepdims=True)
    lb = jnp.zeros((1, D_HGRN), F32)
    for l in range(1, layer + 1):
        lb = lb + prob[l:l + 1, :]

    one = jnp.ones((), BF16)
    q = q_ref[...]
    q = (q / (one + jnp.exp(-q))).astype(F32)
    z = f_ref[...]
    w = jnp.exp(-jnp.abs(z))
    inv = one / (one + w)
    log_sig = jnp.minimum(z, 0) + jnp.log(inv)
    key = ((one - lb.astype(BF16)) * (jnp.where(z >= 0, w, one) * inv)).astype(F32)
    log_a = jnp.log(lb).astype(BF16)
    log_b = jnp.log1p(-lb).astype(BF16) + log_sig
    gate = jnp.maximum(log_a, log_b) + jnp.log(one + jnp.exp(-jnp.abs(log_a - log_b)))
    vb = v_ref[...]

    gate2 = gate * jnp.asarray(math.log2(math.e), BF16)
    b_inc = jnp.dot(tri, gate2, preferred_element_type=F32)
    for hd in range(HGRN_HEADS):
        b_ref[hd] = b_inc[:, hd * HEAD_DIM:(hd + 1) * HEAD_DIM]

    def bcast_row(r):
        return jnp.concatenate([b_ref.at[hd][pl.ds(r, SUBLANES, stride=0), :]
                                for hd in range(HGRN_HEADS)], axis=-1)

    def by_role(h):
        blocks = []
        for start in range(0, C, h):
            upper = (start // h) % 2 == 1
            src = q if upper != reverse else key
            blocks.append(src[start:start + h])
        return jnp.concatenate(blocks, axis=0)

    first_half = lax.broadcasted_iota(jnp.int32, (SUBLANES, 1), 0) < SUBLANES // 2
    scores = [None] * HGRN_HEADS
    for li, h in enumerate(SCAN_LEVELS):
        if h == 1:
            x = jnp.where(up_ref[li] > 0.5, q * (1.0 - key), key)
        else:
            groups = []
            for r0, r1 in _ref_rows(h, reverse):
                ref = bcast_row(r0)
                groups.append(ref if r1 == r0 else jnp.where(first_half, ref, bcast_row(r1)))
            diff = pltpu.bitcast(b_inc - jnp.concatenate(groups, axis=0), jnp.int32)
            neg_dist = pltpu.bitcast(diff | jnp.int32(-2 ** 31), F32)
            roles = by_role(h) if h >= SUBLANES else jnp.where(up_ref[li] > 0.5, q, key)
            x = roles * jnp.exp2(neg_dist)
        xb = x.astype(BF16)
        keep = mask_ref[li] > 0.5
        for hd in range(HGRN_HEADS):
            xh = xb[:, hd * HEAD_DIM:(hd + 1) * HEAD_DIM]
            p = lax.dot_general(xh, xh, (((1,), (1,)), ((), ())), preferred_element_type=F32)
            scores[hd] = jnp.where(keep, p, 0.0 if scores[hd] is None else scores[hd])

    b_total = bcast_row(0 if reverse else C - 1)
    b_rest = jnp.concatenate([b_total] * (C // SUBLANES), axis=0) - b_inc
    q_dec = (q * jnp.exp2(b_inc)).astype(BF16)
    k_dec = (key * jnp.exp2(b_rest)).astype(BF16)
    total = jnp.exp2(b_total[0:1, :])
    qb, kb = q.astype(BF16), key.astype(BF16)
    on_diag = mask_ref[n_lv] > 0.5
    for hd in range(HGRN_HEADS):
        cols = slice(hd * HEAD_DIM, (hd + 1) * HEAD_DIM)
        diag = lax.dot_general(qb[:, cols], kb[:, cols], (((1,), (1,)), ((), ())),
                               preferred_element_type=F32)
        p = jnp.where(on_diag, diag, scores[hd]).astype(BF16)
        st = state_ref[hd]
        o = jnp.dot(p, vb[:, cols], preferred_element_type=F32)
        o = o + lax.dot_general(q_dec[:, cols], st.astype(BF16), (((1,), (1,)), ((), ())),
                                preferred_element_type=F32)
        o_ref[:, cols] = o.astype(o_ref.dtype)
        upd = lax.dot_general(vb[:, cols], k_dec[:, cols], (((0,), (0,)), ((), ())),
                              preferred_element_type=F32)
        state_ref[hd] = st * total[:, cols] + upd


def _hgrn_scan_kernel(qf_ref, ff_ref, vf_ref, qb_ref, fb_ref, vb_ref, lbt_ref, tri_ref, up_ref,
                      mask_ref, of_ref, ob_ref, b_ref, state_ref, *, layer):
    @pl.when(pl.program_id(1) == 0)
    def _():
        state_ref[...] = jnp.zeros_like(state_ref)

    for s in range(SCAN_CHUNKS_PER_STEP):
        rows = pl.ds(s * SCAN_CHUNK, SCAN_CHUNK)
        _scan_chunk(qf_ref.at[rows], ff_ref.at[rows], vf_ref.at[rows], lbt_ref[0], tri_ref[0],
                    up_ref.at[0], mask_ref.at[0], of_ref.at[rows], b_ref.at[0], state_ref.at[0],
                    layer=layer, reverse=False)
        rows = pl.ds((SCAN_CHUNKS_PER_STEP - 1 - s) * SCAN_CHUNK, SCAN_CHUNK)
        _scan_chunk(qb_ref.at[rows], fb_ref.at[rows], vb_ref.at[rows], lbt_ref[1], tri_ref[1],
                    up_ref.at[1], mask_ref.at[1], ob_ref.at[rows], b_ref.at[1], state_ref.at[1],
                    layer=layer, reverse=True)


def _hgrn_scan(proj, lb_table, layer, batch, seq_len):
    T = proj.shape[0]
    C = SCAN_CHUNK
    rows_per_step = SCAN_CHUNKS_PER_STEP * C
    n_chunks = seq_len // rows_per_step
    tri, upper, mask = _scan_constants()
    col0 = (3 * D_HYENA) // D_HGRN

    fwd = lambda b, c: b * n_chunks + c
    bwd = lambda b, c: b * n_chunks + n_chunks - 1 - c
    chunk = lambda rows, col: pl.BlockSpec((rows_per_step, D_HGRN), lambda b, c: (rows(b, c), col))
    whole = lambda a: pl.BlockSpec(a.shape, lambda b, c: (0,) * a.ndim)
    out = jax.ShapeDtypeStruct((T, D_HGRN), BF16)
    return pl.pallas_call(
        functools.partial(_hgrn_scan_kernel, layer=layer),
        grid=(batch, n_chunks),
        in_specs=[
            chunk(fwd, col0), chunk(fwd, col0 + 1), chunk(fwd, col0 + 3),
            chunk(bwd, col0), chunk(bwd, col0 + 2), chunk(bwd, col0 + 3),
            whole(lb_table), whole(tri), whole(upper), whole(mask),
        ],
        out_specs=[chunk(fwd, 0), chunk(bwd, 0)],
        out_shape=[out, out],
        scratch_shapes=[pltpu.VMEM((2, HGRN_HEADS, C, HEAD_DIM), F32),
                        pltpu.VMEM((2, HGRN_HEADS, HEAD_DIM, HEAD_DIM), F32)],
        compiler_params=_params("parallel", "arbitrary"),
    )(proj, proj, proj, proj, proj, proj, lb_table, jnp.asarray(tri, BF16), jnp.asarray(upper),
      jnp.asarray(mask))


def _group_mean_matrix(group):
    idx = np.arange(D_HYENA) // group
    return (idx[:, None] == idx[None, :]).astype(np.float32)


def _mix_out_kernel(x_ref, y_ref, z_ref, x0_ref, of_ref, ob_ref, g_ref, skip_ref,
                    hy_gain_ref, hg_gain_ref, grp_hy_ref, grp_hg_ref, w_ref, post_ref, o_ref):
    z = z_ref[...]
    yh = x0_ref[...].astype(F32) * (y_ref[...] + skip_ref[...] * z)
    ms = _group_sums(yh * yh, grp_hy_ref[...]) * (HYENA_GROUPS / D_HYENA)
    yh = yh * lax.rsqrt(ms + EPS) * hy_gain_ref[...]
    o = of_ref[...].astype(F32) + ob_ref[...].astype(F32)
    ms = _group_sums(o * o, grp_hg_ref[...]) * (1.0 / HEAD_DIM)
    g = g_ref[...].astype(F32)
    o = o * lax.rsqrt(ms + EPS) * hg_gain_ref[...] * (g * (1.0 / (1.0 + jnp.exp(-g))))
    mix = (jnp.dot(yh.astype(BF16), w_ref[:D_HYENA, :], preferred_element_type=F32)
           + jnp.dot(o.astype(BF16), w_ref[D_HYENA:, :], preferred_element_type=F32))
    o_ref[...] = x_ref[...] + mix * _rms_scale(mix) * post_ref[...]


def _mix_out(x2d, y, z, x0, o_fwd, o_bwd, proj, skip, hy_gain, hg_gain, w_bf16, layer,
             post_gain, tm=1024):
    T = x2d.shape[0]
    gate_col = D_IN // D_HGRN - 1
    half = lambda: pl.BlockSpec((tm, D_HYENA), lambda i: (i, 0))
    vec = lambda n: pl.BlockSpec((1, n), lambda i: (0, 0))
    sq = lambda: pl.BlockSpec((D_HYENA, D_HYENA), lambda i: (0, 0))
    row = lambda v: v.reshape(1, -1)
    return pl.pallas_call(
        _mix_out_kernel,
        grid=(T // tm,),
        in_specs=[
            pl.BlockSpec((tm, D_MODEL), lambda i: (i, 0)),
            half(), half(), half(), half(), half(),
            pl.BlockSpec((tm, D_HGRN), lambda i: (i, gate_col)),
            vec(D_HYENA), vec(D_HYENA), vec(D_HGRN), sq(), sq(),
            pl.BlockSpec((None, D_MODEL, D_MODEL), lambda i: (layer, 0, 0)),
            vec(D_MODEL),
        ],
        out_specs=pl.BlockSpec((tm, D_MODEL), lambda i: (i, 0)),
        out_shape=jax.ShapeDtypeStruct((T, D_MODEL), F32),
        compiler_params=_params("parallel"),
    )(x2d, y, z, x0, o_fwd, o_bwd, proj, row(skip), row(hy_gain), row(hg_gain),
      jnp.asarray(_group_mean_matrix(D_HYENA // HYENA_GROUPS), BF16),
      jnp.asarray(_group_mean_matrix(HEAD_DIM), BF16), w_bf16, row(post_gain))


def _gelu_tanh(x):
    c = -2.0 * math.sqrt(2.0 / math.pi) * math.log2(math.e)
    return x / (1.0 + jnp.exp2(x * (c + (0.044715 * c) * (x * x))))


def _ffn_kernel(prev_ref, main_ref, next_ref, pre_ref, wa_ref, wb_ref, cwa_ref, cwb_ref,
                cba_ref, cbb_ref, wd_ref, post_ref, o_ref, h_ref, acc_ref, ua_ref, ub_ref,
                *, tiles_per_seq, tm):
    j = pl.program_id(1)
    pos = pl.program_id(0) % tiles_per_seq

    @pl.when(j == 0)
    def _():
        gain = pre_ref[...]

        def normed(x):
            return (x * _rms_scale(x) * gain).astype(BF16)

        prev = normed(prev_ref[...])
        nxt = normed(next_ref[...])
        h_ref[0:FFN_HALO, :] = jnp.where(pos == 0, jnp.zeros_like(prev), prev)
        h_ref[FFN_HALO:FFN_HALO + tm, :] = normed(main_ref[...])
        h_ref[FFN_HALO + tm:, :] = jnp.where(pos == tiles_per_seq - 1, jnp.zeros_like(nxt), nxt)
        acc_ref[...] = jnp.zeros_like(acc_ref)

    h = h_ref[...]

    def conv_part(w_ref_, cw_ref_, cb_ref_, u_ref):
        u_ref[...] = jnp.dot(h, w_ref_[...], preferred_element_type=F32)
        cw = cw_ref_[...]
        down = u_ref[pl.ds(FFN_HALO - 1, tm), :]
        mid = u_ref[pl.ds(FFN_HALO, tm), :]
        up = u_ref[pl.ds(FFN_HALO + 1, tm), :]
        return down * cw[0:1, :] + mid * cw[1:2, :] + up * cw[2:3, :] + cb_ref_[...]

    a = conv_part(wa_ref, cwa_ref, cba_ref, ua_ref)
    b = conv_part(wb_ref, cwb_ref, cbb_ref, ub_ref)
    act = (_gelu_tanh(a) * b).astype(BF16)
    acc_ref[...] += jnp.dot(act, wd_ref[...], preferred_element_type=F32)

    @pl.when(j == pl.num_programs(1) - 1)
    def _():
        ff = acc_ref[...]
        o_ref[...] = main_ref[...] + ff * _rms_scale(ff) * post_ref[...]


def _ffn(x2d, pre_gain, w_up_bf16, conv_w, conv_b, w_down_bf16, layer, post_gain, seq_len,
         tm=1024, tf=1024):
    T = x2d.shape[0]
    conv_b = conv_b.reshape(conv_b.shape[0], 1, -1)
    halo_per_tile = tm // FFN_HALO
    last_halo = T // FFN_HALO - 1
    n_f = D_FF // tf
    row = lambda v: v.reshape(1, -1)
    return pl.pallas_call(
        functools.partial(_ffn_kernel, tiles_per_seq=seq_len // tm, tm=tm),
        grid=(T // tm, n_f),
        in_specs=[
            pl.BlockSpec((FFN_HALO, D_MODEL),
                         lambda i, j: (jnp.maximum(i * halo_per_tile - 1, 0), 0)),
            pl.BlockSpec((tm, D_MODEL), lambda i, j: (i, 0)),
            pl.BlockSpec((FFN_HALO, D_MODEL),
                         lambda i, j: (jnp.minimum((i + 1) * halo_per_tile, last_halo), 0)),
            pl.BlockSpec((1, D_MODEL), lambda i, j: (0, 0)),
            pl.BlockSpec((None, D_MODEL, tf), lambda i, j: (layer, 0, j)),
            pl.BlockSpec((None, D_MODEL, tf), lambda i, j: (layer, 0, n_f + j)),
            pl.BlockSpec((None, 3, tf), lambda i, j: (layer, 0, j)),
            pl.BlockSpec((None, 3, tf), lambda i, j: (layer, 0, n_f + j)),
            pl.BlockSpec((None, 1, tf), lambda i, j: (layer, 0, j)),
            pl.BlockSpec((None, 1, tf), lambda i, j: (layer, 0, n_f + j)),
            pl.BlockSpec((None, tf, D_MODEL), lambda i, j: (layer, j, 0)),
            pl.BlockSpec((1, D_MODEL), lambda i, j: (0, 0)),
        ],
        out_specs=pl.BlockSpec((tm, D_MODEL), lambda i, j: (i, 0)),
        out_shape=jax.ShapeDtypeStruct((T, D_MODEL), F32),
        scratch_shapes=[pltpu.VMEM((tm + 2 * FFN_HALO, D_MODEL), BF16),
                        pltpu.VMEM((tm, D_MODEL), F32)]
        + [pltpu.VMEM((tm + 2 * FFN_HALO, tf), F32)] * 2,
        compiler_params=_params("parallel", "arbitrary"),
    )(x2d, x2d, x2d, row(pre_gain), w_up_bf16, w_up_bf16, conv_w, conv_w,
      conv_b, conv_b, w_down_bf16, row(post_gain))


def _trunk(x, p):
    B, L, _ = x.shape
    x2d = x.reshape(B * L, D_MODEL)
    plan = _FftPlan(L)
    for l in range(p["w_in"].shape[0]):
        proj = _in_proj(x2d, p["norm_mix_pre"][l], p["w_in_bf16"], l)
        x0, z = _hyena_pre(proj, p["hyena_conv_w"][l], p["hyena_conv_b"][l], L)
        kern = _hyena_filter(L, p["filt_w1"][l], p["filt_b1"][l], p["filt_w2"][l], p["filt_b2"][l],
                             p["filt_w3"][l], p["filt_b3"][l], p["filt_w4"][l], p["filt_freq"][l])
        y = _fft_long_conv(z.reshape(B, L, D_HYENA), kern, plan).reshape(B * L, D_HYENA)
        o_fwd, o_bwd = _hgrn_scan(proj, p["hgrn_lower_bounds"], l, B, L)
        x2d = _mix_out(x2d, y, z, x0, o_fwd, o_bwd, proj, p["hyena_skip"][l], p["hyena_out_norm"][l],
                       p["hgrn_out_norm"][l], p["w_out_bf16"], l, p["norm_mix_post"][l])
        x2d = _ffn(x2d, p["norm_ffn_pre"][l], p["ffn_w_up_bf16"], p["ffn_conv_w"],
                   p["ffn_conv_b"], p["ffn_w_down_bf16"], l, p["norm_ffn_post"][l], L)
    return x2d.reshape(B, L, D_MODEL)


def kernel(x_prompt, x_sample, norm_mix_pre, norm_mix_post, norm_ffn_pre, norm_ffn_post, w_in, hyena_conv_w, hyena_conv_b, filt_w1, filt_b1, filt_w2, filt_b2, filt_w3, filt_b3, filt_w4, filt_freq, hyena_skip, hyena_out_norm, hgrn_lower_bounds, hgrn_out_norm, w_out, ffn_w_up, ffn_conv_w, ffn_conv_b, ffn_w_down):
    p = dict(
        norm_mix_pre=norm_mix_pre, norm_mix_post=norm_mix_post, norm_ffn_pre=norm_ffn_pre,
        norm_ffn_post=norm_ffn_post, w_in=w_in, hyena_conv_w=hyena_conv_w,
        hyena_conv_b=hyena_conv_b, filt_w1=filt_w1, filt_b1=filt_b1, filt_w2=filt_w2,
        filt_b2=filt_b2, filt_w3=filt_w3, filt_b3=filt_b3, filt_w4=filt_w4, filt_freq=filt_freq,
        hyena_skip=hyena_skip, hyena_out_norm=hyena_out_norm,
        hgrn_lower_bounds=hgrn_lower_bounds, hgrn_out_norm=hgrn_out_norm,
        ffn_conv_w=ffn_conv_w, ffn_conv_b=ffn_conv_b,
        w_in_bf16=w_in.astype(BF16), w_out_bf16=w_out.astype(BF16),
        ffn_w_up_bf16=ffn_w_up.astype(BF16), ffn_w_down_bf16=ffn_w_down.astype(BF16),
    )
    return (_trunk(x_prompt, p), _trunk(x_sample, p))
```

```python
import functools
import math

import numpy as np
import jax
import jax.numpy as jnp
from jax import lax
from jax.experimental import pallas as pl
from jax.experimental.pallas import tpu as pltpu

F32 = jnp.float32
BF16 = jnp.bfloat16
HIGHEST = lax.Precision.HIGHEST

D_MODEL = 1024
D_HYENA = 512
HYENA_GROUPS = 8
D_HGRN = 512
HGRN_HEADS = 4
HEAD_DIM = 128
D_IN = 3 * D_HYENA + 5 * D_HGRN
D_FF = 4 * D_MODEL
FILTER_EMB = 33
FILTER_BANDS = 16
FILTER_HIDDEN = 64
DECAY_TARGET = 1e-2
FAST_DECAY_PCT = 0.3
SLOW_DECAY_PCT = 1.5
EPS = 1e-6

SUBLANES = 8
LANES = 128
BF16_ROWS = 16
VMEM_LIMIT_BYTES = 56 * 1024 * 1024

FFT_FAST = 128
FFT_KS_GROUP_MAX = 20
FFT_F_TILE = 16
FFT_B_LANES = 256
FFN_HALO = 16
SCAN_CHUNK = 128
SCAN_CHUNKS_PER_STEP = 4
SCAN_LEVELS = (64, 32, 16, 8, 4, 2, 1)


def _params(*semantics):
    return pltpu.CompilerParams(dimension_semantics=semantics,
                                vmem_limit_bytes=VMEM_LIMIT_BYTES)


def _rms_scale(x):
    return lax.rsqrt(jnp.mean(x * x, axis=-1, keepdims=True) + EPS)


def _group_sums(sq, same_group):
    return jnp.dot(sq.astype(BF16), same_group, preferred_element_type=F32)


def _in_proj_kernel(x_ref, g_ref, w_ref, o_ref, h_ref):
    @pl.when(pl.program_id(1) == 0)
    def _():
        x = x_ref[...]
        h_ref[...] = (x * _rms_scale(x) * g_ref[...]).astype(BF16)

    o_ref[...] = jnp.dot(h_ref[...], w_ref[...], preferred_element_type=F32).astype(o_ref.dtype)


def _in_proj(x2d, gain, w_bf16, layer, tm=1024, tn=4096):
    T = x2d.shape[0]
    N = w_bf16.shape[2]
    return pl.pallas_call(
        _in_proj_kernel,
        grid=(T // tm, N // tn),
        in_specs=[
            pl.BlockSpec((tm, D_MODEL), lambda i, j: (i, 0)),
            pl.BlockSpec((1, D_MODEL), lambda i, j: (0, 0)),
            pl.BlockSpec((None, D_MODEL, tn), lambda i, j: (layer, 0, j)),
        ],
        out_specs=pl.BlockSpec((tm, tn), lambda i, j: (i, j)),
        out_shape=jax.ShapeDtypeStruct((T, N), BF16),
        scratch_shapes=[pltpu.VMEM((tm, D_MODEL), BF16)],
        compiler_params=_params("parallel", "arbitrary"),
    )(x2d, gain.reshape(1, D_MODEL), w_bf16)


def _shift_rows(main, prev_row, next_row):
    tm = main.shape[0]
    rows = lax.broadcasted_iota(jnp.int32, (tm, 1), 0)
    down = jnp.where(rows == 0, prev_row, pltpu.roll(main, 1, 0))
    up = jnp.where(rows == tm - 1, next_row, pltpu.roll(main, tm - 1, 0))
    return down, up


def _hyena_pre_kernel(prev_ref, main_ref, next_ref, w_ref, b_ref, x0_ref, z_ref,
                      *, tiles_per_seq):
    pos = pl.program_id(0) % tiles_per_seq
    main = main_ref[...].astype(F32)
    prev_row = jnp.where(pos == 0, 0.0, prev_ref[...].astype(F32)[BF16_ROWS - 1:BF16_ROWS, :])
    next_row = jnp.where(pos == tiles_per_seq - 1, 0.0, next_ref[...].astype(F32)[0:1, :])
    down, up = _shift_rows(main, prev_row, next_row)
    w = w_ref[...]
    u = down * w[0:1, :] + main * w[1:2, :] + up * w[2:3, :] + b_ref[...]
    x0_ref[...] = u[:, :D_HYENA].astype(x0_ref.dtype)
    z_ref[...] = u[:, D_HYENA:2 * D_HYENA] * u[:, 2 * D_HYENA:]


def _hyena_pre(proj, conv_w, conv_b, seq_len, tm=512):
    T = proj.shape[0]
    width = 3 * D_HYENA
    halo_per_tile = tm // BF16_ROWS
    last_halo = T // BF16_ROWS - 1
    return pl.pallas_call(
        functools.partial(_hyena_pre_kernel, tiles_per_seq=seq_len // tm),
        grid=(T // tm,),
        in_specs=[
            pl.BlockSpec((BF16_ROWS, width),
                         lambda i: (jnp.maximum(i * halo_per_tile - 1, 0), 0)),
            pl.BlockSpec((tm, width), lambda i: (i, 0)),
            pl.BlockSpec((BF16_ROWS, width),
                         lambda i: (jnp.minimum((i + 1) * halo_per_tile, last_halo), 0)),
            pl.BlockSpec((3, width), lambda i: (0, 0)),
            pl.BlockSpec((1, width), lambda i: (0, 0)),
        ],
        out_specs=[pl.BlockSpec((tm, D_HYENA), lambda i: (i, 0)),
                   pl.BlockSpec((tm, D_HYENA), lambda i: (i, 0))],
        out_shape=[jax.ShapeDtypeStruct((T, D_HYENA), BF16),
                   jax.ShapeDtypeStruct((T, D_HYENA), F32)],
        compiler_params=_params("parallel"),
    )(proj, proj, proj, conv_w, conv_b.reshape(1, width))


def _filter_kernel(band_ref, w1t_ref, w1c_ref, w1s_ref, b1_ref, w2_ref, b2_ref, w3_ref, b3_ref,
                   w4_ref, freq_ref, delta_ref, rev_ref, of_ref, ob_ref, *, seq_len, tr):
    i = pl.program_id(0)
    ext = tr + LANES

    def tap_index(shape, axis):
        return (i * tr + lax.broadcasted_iota(jnp.int32, shape, axis)).astype(F32)

    idx = tap_index((1, ext), 1)
    t = idx * (1.0 / (seq_len - 1))
    arg = band_ref[...] * ((2.0 * math.pi / seq_len) * idx)
    fr = freq_ref[...]

    def dense(w_ref_, a):
        return jnp.dot(w_ref_[...], a, precision=HIGHEST, preferred_element_type=F32)

    h = w1t_ref[...] * t + dense(w1c_ref, jnp.cos(arg)) - dense(w1s_ref, jnp.sin(arg))
    h = jnp.sin(fr * (h + b1_ref[...]))
    h = jnp.sin(fr * (dense(w2_ref, h) + b2_ref[...]))
    h = jnp.sin(fr * (dense(w3_ref, h) + b3_ref[...])).astype(BF16)
    taps = lax.dot_general(h, w4_ref[...].astype(BF16), (((0,), (0,)), ((), ())),
                           preferred_element_type=F32)
    window = jnp.exp(-(tap_index((ext, 1), 0) * (1.0 / (seq_len - 1))) * delta_ref[...])
    of_ref[...] = taps[:tr, :D_HYENA] * window[:tr]
    bwd = (taps[:, D_HYENA:] * window).astype(BF16)
    rows = lax.broadcasted_iota(jnp.int32, (tr, 1), 0)
    is_tap_l = (rows == 0) & (i == pl.num_programs(0) - 1)
    ob_ref[...] = jnp.where(is_tap_l, 0.0,
                            jnp.dot(rev_ref[...], bwd, preferred_element_type=F32))


def _hyena_filter(seq_len, w1, b1, w2, b2, w3, b3, w4, freq, tr=1024):
    n_tiles = seq_len // tr
    bands = np.linspace(1e-4, FILTER_BANDS - 1, FILTER_BANDS, dtype=np.float32).reshape(-1, 1)
    deltas = np.abs(np.linspace(math.log(DECAY_TARGET) / SLOW_DECAY_PCT,
                                math.log(DECAY_TARGET) / FAST_DECAY_PCT, D_HYENA,
                                dtype=np.float32)).reshape(1, D_HYENA)
    reverse = (np.arange(tr)[:, None] + np.arange(tr + LANES)[None, :] == tr).astype(np.float32)
    const = lambda i: (0, 0)
    col = lambda v: v.reshape(-1, 1)
    full = lambda a: pl.BlockSpec(a.shape, const)
    operands = [jnp.asarray(bands), col(w1[0]), w1[1:1 + FILTER_BANDS].T,
                w1[1 + FILTER_BANDS:].T, col(b1), w2.T, col(b2), w3.T, col(b3),
                w4, col(freq), jnp.asarray(deltas), jnp.asarray(reverse, BF16)]
    out = jax.ShapeDtypeStruct((seq_len, D_HYENA), F32)
    return pl.pallas_call(
        functools.partial(_filter_kernel, seq_len=seq_len, tr=tr),
        grid=(n_tiles,),
        in_specs=[full(a) for a in operands],
        out_specs=[pl.BlockSpec((tr, D_HYENA), lambda i: (i, 0)),
                   pl.BlockSpec((tr, D_HYENA), lambda i: (n_tiles - 1 - i, 0))],
        out_shape=[out, out],
        compiler_params=_params("parallel"),
    )(*operands)


class _FftPlan:
    def __init__(self, seq_len):
        self.n = 2 * seq_len
        self.fast = FFT_FAST
        self.slow = self.n // FFT_FAST
        self.slow_half = self.slow // 2
        self.ks = self.slow_half + 1
        self.ks_pad = -(-self.ks // SUBLANES) * SUBLANES
        self.ks_group = max(g for g in range(1, FFT_KS_GROUP_MAX + 1) if self.ks_pad % g == 0)
        S, F, N = self.slow, self.fast, self.n
        ks = np.arange(self.ks_pad, dtype=np.float64)[:, None]
        valid = (ks < self.ks)

        def stage_a(n_s):
            s = np.arange(n_s, dtype=np.float64)[None, :]
            ang = 2.0 * np.pi * ks * s / S
            return np.concatenate([np.where(valid, np.cos(ang), 0.0),
                                   np.where(valid, -np.sin(ang), 0.0)], axis=0)

        self.a_half = stage_a(self.slow_half).astype(np.float32)
        self.a_full = stage_a(self.slow).astype(np.float32)
        s = np.arange(self.slow_half, dtype=np.float64)[:, None]
        kk = np.arange(self.ks_pad, dtype=np.float64)[None, :]
        weight = np.where((kk == 0) | (kk == self.slow_half), 1.0, 2.0) * (kk < self.ks) / N
        ang = 2.0 * np.pi * s * kk / S
        self.a_inv = np.concatenate([weight * np.cos(ang), -weight * np.sin(ang)],
                                    axis=1).astype(np.float32)
        f = np.arange(F, dtype=np.float64)
        ang = 2.0 * np.pi * np.outer(np.arange(self.ks_pad, dtype=np.float64), f) / N
        self.tw_re = np.cos(ang).reshape(-1, 1).astype(np.float32)
        self.tw_im = (-np.sin(ang)).reshape(-1, 1).astype(np.float32)
        ang = 2.0 * np.pi * np.outer(f, f) / F
        self.f_re = np.cos(ang).astype(np.float32)
        self.f_im = (-np.sin(ang)).astype(np.float32)


def _dft_dot(a_ref, x):
    return jnp.dot(a_ref[...], x.astype(BF16), preferred_element_type=F32)


def _stage_a_kernel(a_ref, *refs, batch):
    x_refs, o_ref, stage_ref = refs[:-2], refs[-2], refs[-1]
    m, k = a_ref.shape
    kp = k // len(x_refs)
    x2ds = [x_ref.reshape(batch * kp * FFT_F_TILE, LANES) for x_ref in x_refs]
    x2d = x2ds[0]
    o2d = o_ref.reshape(batch * m * FFT_F_TILE, LANES)
    stage_in = x_refs[0].dtype != F32
    for b in range(batch):
        x_rows = pl.ds(b * k * FFT_F_TILE, k * FFT_F_TILE)
        o_rows = pl.ds(b * m * FFT_F_TILE, m * FFT_F_TILE)
        if stage_in:
            stage_ref[...] = x2d[x_rows, :].astype(F32)
            slabs = [stage_ref[pl.ds(j, k, stride=FFT_F_TILE), :] for j in range(FFT_F_TILE)]
        else:
            slabs = [jnp.concatenate(
                [part[pl.ds(b * kp * FFT_F_TILE + j, kp, stride=FFT_F_TILE), :] for part in x2ds],
                axis=0) for j in range(FFT_F_TILE)]
        r = _dft_dot(a_ref, jnp.concatenate(slabs, axis=-1))
        for j in range(FFT_F_TILE):
            r_j = r[:, j * LANES:(j + 1) * LANES]
            if stage_in:
                o2d[pl.ds(b * m * FFT_F_TILE + j, m, stride=FFT_F_TILE), :] = r_j
            else:
                stage_ref[pl.ds(j, m, stride=FFT_F_TILE), :] = r_j
        if not stage_in:
            o2d[o_rows, :] = stage_ref[...].astype(o_ref.dtype)


def _stage_a(a, xs, inverse):
    F, C = xs[0].shape[-2:]
    B = xs[0].shape[0]
    M, K = a.shape
    x_dims, o_dims = ((2, K // 2), (M,)) if inverse else ((K // len(xs),), (2, M // 2))

    def spec(dims):
        zeros = (0,) * len(dims)
        return pl.BlockSpec((B,) + dims + (FFT_F_TILE, LANES), lambda f, c: (0,) + zeros + (f, c))

    return pl.pallas_call(
        functools.partial(_stage_a_kernel, batch=B),
        grid=(F // FFT_F_TILE, C // LANES),
        in_specs=[pl.BlockSpec((M, K), lambda f, c: (0, 0))] + [spec(x_dims)] * len(xs),
        out_specs=spec(o_dims),
        out_shape=jax.ShapeDtypeStruct((B,) + o_dims + (F, C), F32 if inverse else BF16),
        scratch_shapes=[pltpu.VMEM(((K if inverse else M) * FFT_F_TILE, LANES), F32)],
        compiler_params=_params("parallel", "parallel"),
    )(jnp.asarray(a, BF16), *xs)


def _stage_b_dft(fr_ref, fi_ref, tiles, conj=False):
    w = tiles[0][0].shape[-1]
    both = jnp.concatenate([part for tile in tiles for part in tile], axis=-1)
    pr = _dft_dot(fr_ref, both)
    pi = _dft_dot(fi_ref, both)
    out = []
    for g in range(len(tiles)):
        re, im = slice(2 * g * w, (2 * g + 1) * w), slice((2 * g + 1) * w, (2 * g + 2) * w)
        if conj:
            out.append((pr[:, re] + pi[:, im], pr[:, im] - pi[:, re]))
        else:
            out.append((pr[:, re] - pi[:, im], pr[:, im] + pi[:, re]))
    return out


def _twiddled_tiles(a_ref, twr_ref, twi_ref):
    tiles, twiddles = [], []
    for g in range(a_ref.shape[2] // FFT_FAST):
        rows = slice(g * FFT_FAST, (g + 1) * FFT_FAST)
        tr, ti = twr_ref[rows, :], twi_ref[rows, :]
        for c in range(a_ref.shape[3] // LANES):
            at = rows, slice(c * LANES, (c + 1) * LANES)
            ar, ai = a_ref[(0, 0) + at].astype(F32), a_ref[(0, 1) + at].astype(F32)
            tiles.append((ar * tr - ai * ti, ar * ti + ai * tr))
            twiddles.append((at, tr, ti))
    return tiles, twiddles


def _fft_fwd_kernel(a_ref, twr_ref, twi_ref, fr_ref, fi_ref, o_ref):
    tiles, twiddles = _twiddled_tiles(a_ref, twr_ref, twi_ref)
    for (at, _, _), (xr, xi) in zip(twiddles, _stage_b_dft(fr_ref, fi_ref, tiles)):
        o_ref[(0, 0) + at] = xr.astype(o_ref.dtype)
        o_ref[(0, 1) + at] = xi.astype(o_ref.dtype)


def _fft_conv_kernel(a_ref, k_ref, twr_ref, twi_ref, fr_ref, fi_ref, o_ref):
    tiles, twiddles = _twiddled_tiles(a_ref, twr_ref, twi_ref)
    products = []
    for (at, _, _), (xr, xi) in zip(twiddles, _stage_b_dft(fr_ref, fi_ref, tiles)):
        kr, ki = k_ref[(0, 0) + at].astype(F32), k_ref[(0, 1) + at].astype(F32)
        products.append((xr * kr - xi * ki, xr * ki + xi * kr))
    inverse = _stage_b_dft(fr_ref, fi_ref, products, conj=True)
    for (at, tr, ti), (br, bi) in zip(twiddles, inverse):
        o_ref[(0, 0) + at] = (br * tr + bi * ti).astype(o_ref.dtype)
        o_ref[(0, 1) + at] = (bi * tr - br * ti).astype(o_ref.dtype)


def _fft_stage_b(plan, a, k=None):
    B, _, R, C = a.shape
    rows = plan.ks_group * FFT_FAST
    data = pl.BlockSpec((1, 2, rows, FFT_B_LANES), lambda r, c, b: (b, 0, r, c))
    tw = pl.BlockSpec((rows, LANES), lambda r, c, b: (r, 0))
    mat = pl.BlockSpec((FFT_FAST, FFT_FAST), lambda r, c, b: (0, 0))
    lane_bcast = lambda col: jnp.broadcast_to(jnp.asarray(col), (col.shape[0], LANES))
    consts = (lane_bcast(plan.tw_re), lane_bcast(plan.tw_im),
              jnp.asarray(plan.f_re, BF16), jnp.asarray(plan.f_im, BF16))
    common = dict(
        grid=(R // rows, C // FFT_B_LANES, B),
        out_specs=data,
        out_shape=jax.ShapeDtypeStruct(a.shape, BF16),
        compiler_params=_params("parallel", "parallel", "arbitrary"),
    )
    if k is None:
        return pl.pallas_call(_fft_fwd_kernel, in_specs=[data, tw, tw, mat, mat],
                              **common)(a, *consts)
    kspec = pl.BlockSpec((1, 2, rows, FFT_B_LANES), lambda r, c, b: (0, 0, r, c))
    return pl.pallas_call(_fft_conv_kernel, in_specs=[data, kspec, tw, tw, mat, mat],
                          **common)(a, k, *consts)


def _fft_long_conv(z, kern_halves, plan):
    B, L, C = z.shape
    S, F, P = plan.slow, plan.fast, plan.ks_pad
    ka = _stage_a(plan.a_full, [h.reshape(1, S // 2, F, C) for h in kern_halves], inverse=False)
    k_spec = _fft_stage_b(plan, ka.reshape(1, 2, P * F, C))
    za = _stage_a(plan.a_half, [z.reshape(B, S // 2, F, C)], inverse=False)
    ya = _fft_stage_b(plan, za.reshape(B, 2, P * F, C), k_spec)
    y = _stage_a(plan.a_inv, [ya.reshape(B, 2, P, F, C)], inverse=True)
    return y.reshape(B, L, C)


def _scan_constants():
    C = SCAN_CHUNK
    n_lv = len(SCAN_LEVELS)
    tri = np.tril(np.ones((C, C), np.float32))
    upper = np.zeros((n_lv, C, 1), np.float32)
    mask = np.zeros((n_lv + 1, C, C), np.float32)
    for li, h in enumerate(SCAN_LEVELS):
        for t in range(C):
            r = (t // (2 * h)) * 2 * h + h
            if t % (2 * h) >= h:
                upper[li, t, 0] = 1.0
                mask[li, t, r - h:r] = 1.0
    mask[n_lv] = np.eye(C, dtype=np.float32)
    flip = lambda m: m[:, ::-1, ::-1]
    return (np.stack([tri, tri[::-1, ::-1]]), np.stack([upper, upper[:, ::-1]]),
            np.stack([mask, flip(mask)]))


def _ref_rows(h, reverse):
    rows = []
    for g in range(SCAN_CHUNK // SUBLANES):
        pair = []
        for t in (g * SUBLANES, g * SUBLANES + SUBLANES // 2):
            start = (t // (2 * h)) * 2 * h
            pair.append(start + h if reverse else start + h - 1)
        rows.append(tuple(pair))
    return rows


def _scan_chunk(q_ref, f_ref, v_ref, lb_table, tri, up_ref, mask_ref, o_ref, b_ref, state_ref,
                *, layer, reverse):
    n_lv = len(SCAN_LEVELS)
    C = SCAN_CHUNK

    e = jnp.exp(lb_table - jnp.max(lb_table, axis=0, keepdims=True))
    prob = e / jnp.sum(e, axis=0, keepdims=True)
    lb = jnp.zeros((1, D_HGRN), F32)
    for l in range(1, layer + 1):
        lb = lb + prob[l:l + 1, :]

    one = jnp.ones((), BF16)
    q = q_ref[...]
    half = jnp.asarray(0.5, BF16)
    half_q = half * q
    q = (half_q + half_q * jnp.tanh(half_q)).astype(F32)
    z = f_ref[...]
    log_sig = jnp.minimum(z, 0) - jnp.log(one + jnp.exp(-jnp.abs(z)))
    key = ((half - half * lb.astype(BF16)) * (one - jnp.tanh(half * z))).astype(F32)
    log_a = jnp.log(lb).astype(BF16)
    log_b = jnp.log1p(-lb).astype(BF16) + log_sig
    gate = jnp.maximum(log_a, log_b) + jnp.log(one + jnp.exp(-jnp.abs(log_a - log_b)))
    vb = v_ref[...]

    gate2 = gate * jnp.asarray(math.log2(math.e), BF16)
    b_inc = jnp.dot(tri, gate2, preferred_element_type=F32)
    for hd in range(HGRN_HEADS):
        b_ref[hd] = b_inc[:, hd * HEAD_DIM:(hd + 1) * HEAD_DIM]

    def bcast_row(r):
        return jnp.concatenate([b_ref.at[hd][pl.ds(r, SUBLANES, stride=0), :]
                                for hd in range(HGRN_HEADS)], axis=-1)

    def by_role(h):
        blocks = []
        for start in range(0, C, h):
            upper = (start // h) % 2 == 1
            src = q if upper != reverse else key
            blocks.append(src[start:start + h])
        return jnp.concatenate(blocks, axis=0)

    first_half = lax.broadcasted_iota(jnp.int32, (SUBLANES, 1), 0) < SUBLANES // 2
    scores = [None] * HGRN_HEADS
    for li, h in enumerate(SCAN_LEVELS):
        if h == 1:
            x = jnp.where(up_ref[li] > 0.5, q * (1.0 - key), key)
        else:
            groups = []
            for r0, r1 in _ref_rows(h, reverse):
                ref = bcast_row(r0)
                groups.append(ref if r1 == r0 else jnp.where(first_half, ref, bcast_row(r1)))
            diff = pltpu.bitcast(b_inc - jnp.concatenate(groups, axis=0), jnp.int32)
            neg_dist = pltpu.bitcast(diff | jnp.int32(-2 ** 31), F32)
            roles = by_role(h) if h >= SUBLANES else jnp.where(up_ref[li] > 0.5, q, key)
            x = roles * jnp.exp2(neg_dist)
        xb = x.astype(BF16)
        keep = mask_ref[li] > 0.5
        for hd in range(HGRN_HEADS):
            xh = xb[:, hd * HEAD_DIM:(hd + 1) * HEAD_DIM]
            p = lax.dot_general(xh, xh, (((1,), (1,)), ((), ())), preferred_element_type=F32)
            scores[hd] = jnp.where(keep, p, 0.0 if scores[hd] is None else scores[hd])

    b_total = bcast_row(0 if reverse else C - 1)
    b_rest = jnp.concatenate([b_total] * (C // SUBLANES), axis=0) - b_inc
    q_dec = (q * jnp.exp2(b_inc)).astype(BF16)
    k_dec = (key * jnp.exp2(b_rest)).astype(BF16)
    total = jnp.exp2(b_total[0:1, :])
    qb, kb = q.astype(BF16), key.astype(BF16)
    on_diag = mask_ref[n_lv] > 0.5
    for hd in range(HGRN_HEADS):
        cols = slice(hd * HEAD_DIM, (hd + 1) * HEAD_DIM)
        diag = lax.dot_general(qb[:, cols], kb[:, cols], (((1,), (1,)), ((), ())),
                               preferred_element_type=F32)
        p = jnp.where(on_diag, diag, scores[hd]).astype(BF16)
        st = state_ref[hd]
        o = jnp.dot(p, vb[:, cols], preferred_element_type=F32)
        o = o + lax.dot_general(q_dec[:, cols], st.astype(BF16), (((1,), (1,)), ((), ())),
                                preferred_element_type=F32)
        o_ref[:, cols] = o.astype(o_ref.dtype)
        upd = lax.dot_general(vb[:, cols], k_dec[:, cols], (((0,), (0,)), ((), ())),
                              preferred_element_type=F32)
        state_ref[hd] = st * total[:, cols] + upd


def _hgrn_scan_kernel(qf_ref, ff_ref, vf_ref, qb_ref, fb_ref, vb_ref, lbt_ref, tri_ref, up_ref,
                      mask_ref, of_ref, ob_ref, b_ref, state_ref, *, layer):
    @pl.when(pl.program_id(1) == 0)
    def _():
        state_ref[...] = jnp.zeros_like(state_ref)

    for s in range(SCAN_CHUNKS_PER_STEP):
        rows = pl.ds(s * SCAN_CHUNK, SCAN_CHUNK)
        _scan_chunk(qf_ref.at[rows], ff_ref.at[rows], vf_ref.at[rows], lbt_ref[0], tri_ref[0],
                    up_ref.at[0], mask_ref.at[0], of_ref.at[rows], b_ref.at[0], state_ref.at[0],
                    layer=layer, reverse=False)
        rows = pl.ds((SCAN_CHUNKS_PER_STEP - 1 - s) * SCAN_CHUNK, SCAN_CHUNK)
        _scan_chunk(qb_ref.at[rows], fb_ref.at[rows], vb_ref.at[rows], lbt_ref[1], tri_ref[1],
                    up_ref.at[1], mask_ref.at[1], ob_ref.at[rows], b_ref.at[1], state_ref.at[1],
                    layer=layer, reverse=True)


def _hgrn_scan(proj, lb_table, layer, batch, seq_len):
    T = proj.shape[0]
    C = SCAN_CHUNK
    rows_per_step = SCAN_CHUNKS_PER_STEP * C
    n_chunks = seq_len // rows_per_step
    tri, upper, mask = _scan_constants()
    col0 = (3 * D_HYENA) // D_HGRN

    fwd = lambda b, c: b * n_chunks + c
    bwd = lambda b, c: b * n_chunks + n_chunks - 1 - c
    chunk = lambda rows, col: pl.BlockSpec((rows_per_step, D_HGRN), lambda b, c: (rows(b, c), col))
    whole = lambda a: pl.BlockSpec(a.shape, lambda b, c: (0,) * a.ndim)
    out = jax.ShapeDtypeStruct((T, D_HGRN), BF16)
    return pl.pallas_call(
        functools.partial(_hgrn_scan_kernel, layer=layer),
        grid=(batch, n_chunks),
        in_specs=[
            chunk(fwd, col0), chunk(fwd, col0 + 1), chunk(fwd, col0 + 3),
            chunk(bwd, col0), chunk(bwd, col0 + 2), chunk(bwd, col0 + 3),
            whole(lb_table), whole(tri), whole(upper), whole(mask),
        ],
        out_specs=[chunk(fwd, 0), chunk(bwd, 0)],
        out_shape=[out, out],
        scratch_shapes=[pltpu.VMEM((2, HGRN_HEADS, C, HEAD_DIM), F32),
                        pltpu.VMEM((2, HGRN_HEADS, HEAD_DIM, HEAD_DIM), F32)],
        compiler_params=_params("parallel", "arbitrary"),
    )(proj, proj, proj, proj, proj, proj, lb_table, jnp.asarray(tri, BF16), jnp.asarray(upper),
      jnp.asarray(mask))


def _group_mean_matrix(group):
    idx = np.arange(D_HYENA) // group
    return (idx[:, None] == idx[None, :]).astype(np.float32)


def _mix_out_kernel(x_ref, y_ref, z_ref, x0_ref, of_ref, ob_ref, g_ref, skip_ref,
                    hy_gain_ref, hg_gain_ref, grp_hy_ref, grp_hg_ref, w_ref, post_ref, o_ref):
    z = z_ref[...]
    yh = x0_ref[...].astype(F32) * (y_ref[...] + skip_ref[...] * z)
    ms = _group_sums(yh * yh, grp_hy_ref[...]) * (HYENA_GROUPS / D_HYENA)
    yh = yh * lax.rsqrt(ms + EPS) * hy_gain_ref[...]
    o = of_ref[...].astype(F32) + ob_ref[...].astype(F32)
    ms = _group_sums(o * o, grp_hg_ref[...]) * (1.0 / HEAD_DIM)
    g = g_ref[...].astype(F32)
    half_g = 0.5 * g
    silu_g = half_g + half_g * jnp.tanh(half_g)
    o = o * lax.rsqrt(ms + EPS) * hg_gain_ref[...] * silu_g
    mix = (jnp.dot(yh.astype(BF16), w_ref[:D_HYENA, :], preferred_element_type=F32)
           + jnp.dot(o.astype(BF16), w_ref[D_HYENA:, :], preferred_element_type=F32))
    o_ref[...] = x_ref[...] + mix * _rms_scale(mix) * post_ref[...]


def _mix_out(x2d, y, z, x0, o_fwd, o_bwd, proj, skip, hy_gain, hg_gain, w_bf16, layer,
             post_gain, tm=1024):
    T = x2d.shape[0]
    gate_col = D_IN // D_HGRN - 1
    half = lambda: pl.BlockSpec((tm, D_HYENA), lambda i: (i, 0))
    vec = lambda n: pl.BlockSpec((1, n), lambda i: (0, 0))
    sq = lambda: pl.BlockSpec((D_HYENA, D_HYENA), lambda i: (0, 0))
    row = lambda v: v.reshape(1, -1)
    return pl.pallas_call(
        _mix_out_kernel,
        grid=(T // tm,),
        in_specs=[
            pl.BlockSpec((tm, D_MODEL), lambda i: (i, 0)),
            half(), half(), half(), half(), half(),
            pl.BlockSpec((tm, D_HGRN), lambda i: (i, gate_col)),
            vec(D_HYENA), vec(D_HYENA), vec(D_HGRN), sq(), sq(),
            pl.BlockSpec((None, D_MODEL, D_MODEL), lambda i: (layer, 0, 0)),
            vec(D_MODEL),
        ],
        out_specs=pl.BlockSpec((tm, D_MODEL), lambda i: (i, 0)),
        out_shape=jax.ShapeDtypeStruct((T, D_MODEL), F32),
        compiler_params=_params("parallel"),
    )(x2d, y, z, x0, o_fwd, o_bwd, proj, row(skip), row(hy_gain), row(hg_gain),
      jnp.asarray(_group_mean_matrix(D_HYENA // HYENA_GROUPS), BF16),
      jnp.asarray(_group_mean_matrix(HEAD_DIM), BF16), w_bf16, row(post_gain))


def _gelu_tanh(x):
    c = math.sqrt(2.0 / math.pi)
    half = 0.5 * x
    return half + half * jnp.tanh(x * (c + (0.044715 * c) * (x * x)))


def _ffn_kernel(prev_ref, main_ref, next_ref, pre_ref, wa_ref, wb_ref, cwa_ref, cwb_ref,
                cba_ref, cbb_ref, wd_ref, post_ref, o_ref, h_ref, acc_ref, ua_ref, ub_ref,
                *, tiles_per_seq, tm):
    j = pl.program_id(1)
    pos = pl.program_id(0) % tiles_per_seq

    @pl.when(j == 0)
    def _():
        gain = pre_ref[...]

        def normed(x):
            return (x * _rms_scale(x) * gain).astype(BF16)

        prev = normed(prev_ref[...])
        nxt = normed(next_ref[...])
        h_ref[0:FFN_HALO, :] = jnp.where(pos == 0, jnp.zeros_like(prev), prev)
        h_ref[FFN_HALO:FFN_HALO + tm, :] = normed(main_ref[...])
        h_ref[FFN_HALO + tm:, :] = jnp.where(pos == tiles_per_seq - 1, jnp.zeros_like(nxt), nxt)
        acc_ref[...] = jnp.zeros_like(acc_ref)

    h = h_ref[...]

    def conv_part(w_ref_, cw_ref_, cb_ref_, u_ref):
        u_ref[...] = jnp.dot(h, w_ref_[...], preferred_element_type=F32)
        cw = cw_ref_[...]
        down = u_ref[pl.ds(FFN_HALO - 1, tm), :]
        mid = u_ref[pl.ds(FFN_HALO, tm), :]
        up = u_ref[pl.ds(FFN_HALO + 1, tm), :]
        return down * cw[0:1, :] + mid * cw[1:2, :] + up * cw[2:3, :] + cb_ref_[...]

    a = conv_part(wa_ref, cwa_ref, cba_ref, ua_ref)
    b = conv_part(wb_ref, cwb_ref, cbb_ref, ub_ref)
    act = _gelu_tanh(a.astype(BF16)) * b.astype(BF16)
    acc_ref[...] += jnp.dot(act, wd_ref[...], preferred_element_type=F32)

    @pl.when(j == pl.num_programs(1) - 1)
    def _():
        ff = acc_ref[...]
        o_ref[...] = main_ref[...] + ff * _rms_scale(ff) * post_ref[...]


def _ffn(x2d, pre_gain, w_up_bf16, conv_w, conv_b, w_down_bf16, layer, post_gain, seq_len,
         tm=1024, tf=1024):
    T = x2d.shape[0]
    conv_b = conv_b.reshape(conv_b.shape[0], 1, -1)
    halo_per_tile = tm // FFN_HALO
    last_halo = T // FFN_HALO - 1
    n_f = D_FF // tf
    row = lambda v: v.reshape(1, -1)
    return pl.pallas_call(
        functools.partial(_ffn_kernel, tiles_per_seq=seq_len // tm, tm=tm),
        grid=(T // tm, n_f),
        in_specs=[
            pl.BlockSpec((FFN_HALO, D_MODEL),
                         lambda i, j: (jnp.maximum(i * halo_per_tile - 1, 0), 0)),
            pl.BlockSpec((tm, D_MODEL), lambda i, j: (i, 0)),
            pl.BlockSpec((FFN_HALO, D_MODEL),
                         lambda i, j: (jnp.minimum((i + 1) * halo_per_tile, last_halo), 0)),
            pl.BlockSpec((1, D_MODEL), lambda i, j: (0, 0)),
            pl.BlockSpec((None, D_MODEL, tf), lambda i, j: (layer, 0, j)),
            pl.BlockSpec((None, D_MODEL, tf), lambda i, j: (layer, 0, n_f + j)),
            pl.BlockSpec((None, 3, tf), lambda i, j: (layer, 0, j)),
            pl.BlockSpec((None, 3, tf), lambda i, j: (layer, 0, n_f + j)),
            pl.BlockSpec((None, 1, tf), lambda i, j: (layer, 0, j)),
            pl.BlockSpec((None, 1, tf), lambda i, j: (layer, 0, n_f + j)),
            pl.BlockSpec((None, tf, D_MODEL), lambda i, j: (layer, j, 0)),
            pl.BlockSpec((1, D_MODEL), lambda i, j: (0, 0)),
        ],
        out_specs=pl.BlockSpec((tm, D_MODEL), lambda i, j: (i, 0)),
        out_shape=jax.ShapeDtypeStruct((T, D_MODEL), F32),
        scratch_shapes=[pltpu.VMEM((tm + 2 * FFN_HALO, D_MODEL), BF16),
                        pltpu.VMEM((tm, D_MODEL), F32)]
        + [pltpu.VMEM((tm + 2 * FFN_HALO, tf), F32)] * 2,
        compiler_params=_params("parallel", "arbitrary"),
    )(x2d, x2d, x2d, row(pre_gain), w_up_bf16, w_up_bf16, conv_w, conv_w,
      conv_b, conv_b, w_down_bf16, row(post_gain))


def _trunk(x, p):
    B, L, _ = x.shape
    x2d = x.reshape(B * L, D_MODEL)
    plan = _FftPlan(L)
    for l in range(p["w_in"].shape[0]):
        proj = _in_proj(x2d, p["norm_mix_pre"][l], p["w_in_bf16"], l)
        x0, z = _hyena_pre(proj, p["hyena_conv_w"][l], p["hyena_conv_b"][l], L)
        kern = _hyena_filter(L, p["filt_w1"][l], p["filt_b1"][l], p["filt_w2"][l], p["filt_b2"][l],
                             p["filt_w3"][l], p["filt_b3"][l], p["filt_w4"][l], p["filt_freq"][l])
        y = _fft_long_conv(z.reshape(B, L, D_HYENA), kern, plan).reshape(B * L, D_HYENA)
        o_fwd, o_bwd = _hgrn_scan(proj, p["hgrn_lower_bounds"], l, B, L)
        x2d = _mix_out(x2d, y, z, x0, o_fwd, o_bwd, proj, p["hyena_skip"][l], p["hyena_out_norm"][l],
                       p["hgrn_out_norm"][l], p["w_out_bf16"], l, p["norm_mix_post"][l])
        x2d = _ffn(x2d, p["norm_ffn_pre"][l], p["ffn_w_up_bf16"], p["ffn_conv_w"],
                   p["ffn_conv_b"], p["ffn_w_down_bf16"], l, p["norm_ffn_post"][l], L)
    return x2d.reshape(B, L, D_MODEL)


def kernel(x_prompt, x_sample, norm_mix_pre, norm_mix_post, norm_ffn_pre, norm_ffn_post, w_in, hyena_conv_w, hyena_conv_b, filt_w1, filt_b1, filt_w2, filt_b2, filt_w3, filt_b3, filt_w4, filt_freq, hyena_skip, hyena_out_norm, hgrn_lower_bounds, hgrn_out_norm, w_out, ffn_w_up, ffn_conv_w, ffn_conv_b, ffn_w_down):
    p = dict(
        norm_mix_pre=norm_mix_pre, norm_mix_post=norm_mix_post, norm_ffn_pre=norm_ffn_pre,
        norm_ffn_post=norm_ffn_post, w_in=w_in, hyena_conv_w=hyena_conv_w,
        hyena_conv_b=hyena_conv_b, filt_w1=filt_w1, filt_b1=filt_b1, filt_w2=filt_w2,
        filt_b2=filt_b2, filt_w3=filt_w3, filt_b3=filt_b3, filt_w4=filt_w4, filt_freq=filt_freq,
        hyena_skip=hyena_skip, hyena_out_norm=hyena_out_norm,
        hgrn_lower_bounds=hgrn_lower_bounds, hgrn_out_norm=hgrn_out_norm,
        ffn_conv_w=ffn_conv_w, ffn_conv_b=ffn_conv_b,
        w_in_bf16=w_in.astype(BF16), w_out_bf16=w_out.astype(BF16),
        ffn_w_up_bf16=ffn_w_up.astype(BF16), ffn_w_down_bf16=ffn_w_down.astype(BF16),
    )
    return (_trunk(x_prompt, p), _trunk(x_sample, p))
```

```python
import functools
import math

import numpy as np
import jax
import jax.numpy as jnp
from jax import lax
from jax.experimental import pallas as pl
from jax.experimental.pallas import tpu as pltpu

F32 = jnp.float32
BF16 = jnp.bfloat16
HIGHEST = lax.Precision.HIGHEST

D_MODEL = 1024
D_HYENA = 512
HYENA_GROUPS = 8
D_HGRN = 512
HGRN_HEADS = 4
HEAD_DIM = 128
D_IN = 3 * D_HYENA + 5 * D_HGRN
D_FF = 4 * D_MODEL
FILTER_EMB = 33
FILTER_BANDS = 16
FILTER_HIDDEN = 64
DECAY_TARGET = 1e-2
FAST_DECAY_PCT = 0.3
SLOW_DECAY_PCT = 1.5
EPS = 1e-6

SUBLANES = 8
LANES = 128
BF16_ROWS = 16
VMEM_LIMIT_BYTES = 56 * 1024 * 1024

FFT_FAST = 128
FFT_KS_GROUP_MAX = 20
FFT_F_TILE = 16
FFT_B_LANES = 256
FFN_HALO = 16
SCAN_CHUNK = 128
SCAN_CHUNKS_PER_STEP = 4
SCAN_LEVELS = (64, 32, 16, 8, 4, 2, 1)


def _params(*semantics):
    return pltpu.CompilerParams(dimension_semantics=semantics,
                                vmem_limit_bytes=VMEM_LIMIT_BYTES)


def _rms_scale(x):
    return lax.rsqrt(jnp.mean(x * x, axis=-1, keepdims=True) + EPS)


def _group_sums(sq, same_group):
    return jnp.dot(sq.astype(BF16), same_group, preferred_element_type=F32)


def _in_proj_kernel(x_ref, g_ref, w_ref, o_ref, h_ref):
    @pl.when(pl.program_id(1) == 0)
    def _():
        x = x_ref[...]
        h_ref[...] = (x * _rms_scale(x) * g_ref[...]).astype(BF16)

    o_ref[...] = jnp.dot(h_ref[...], w_ref[...], preferred_element_type=F32).astype(o_ref.dtype)


def _in_proj(x2d, gain, w_bf16, layer, tm=1024, tn=4096):
    T = x2d.shape[0]
    N = w_bf16.shape[2]
    return pl.pallas_call(
        _in_proj_kernel,
        grid=(T // tm, N // tn),
        in_specs=[
            pl.BlockSpec((tm, D_MODEL), lambda i, j: (i, 0)),
            pl.BlockSpec((1, D_MODEL), lambda i, j: (0, 0)),
            pl.BlockSpec((None, D_MODEL, tn), lambda i, j: (layer, 0, j)),
        ],
        out_specs=pl.BlockSpec((tm, tn), lambda i, j: (i, j)),
        out_shape=jax.ShapeDtypeStruct((T, N), BF16),
        scratch_shapes=[pltpu.VMEM((tm, D_MODEL), BF16)],
        compiler_params=_params("parallel", "arbitrary"),
    )(x2d, gain.reshape(1, D_MODEL), w_bf16)


def _shift_rows(main, prev_row, next_row):
    tm = main.shape[0]
    rows = lax.broadcasted_iota(jnp.int32, (tm, 1), 0)
    down = jnp.where(rows == 0, prev_row, pltpu.roll(main, 1, 0))
    up = jnp.where(rows == tm - 1, next_row, pltpu.roll(main, tm - 1, 0))
    return down, up


def _hyena_pre_kernel(prev_ref, main_ref, next_ref, w_ref, b_ref, x0_ref, z_ref,
                      *, tiles_per_seq):
    pos = pl.program_id(0) % tiles_per_seq
    main = main_ref[...].astype(F32)
    prev_row = jnp.where(pos == 0, 0.0, prev_ref[...].astype(F32)[BF16_ROWS - 1:BF16_ROWS, :])
    next_row = jnp.where(pos == tiles_per_seq - 1, 0.0, next_ref[...].astype(F32)[0:1, :])
    down, up = _shift_rows(main, prev_row, next_row)
    w = w_ref[...]
    u = down * w[0:1, :] + main * w[1:2, :] + up * w[2:3, :] + b_ref[...]
    x0_ref[...] = u[:, :D_HYENA].astype(x0_ref.dtype)
    z_ref[...] = (u[:, D_HYENA:2 * D_HYENA] * u[:, 2 * D_HYENA:]).astype(z_ref.dtype)


def _hyena_pre(proj, conv_w, conv_b, seq_len, tm=512):
    T = proj.shape[0]
    width = 3 * D_HYENA
    halo_per_tile = tm // BF16_ROWS
    last_halo = T // BF16_ROWS - 1
    return pl.pallas_call(
        functools.partial(_hyena_pre_kernel, tiles_per_seq=seq_len // tm),
        grid=(T // tm,),
        in_specs=[
            pl.BlockSpec((BF16_ROWS, width),
                         lambda i: (jnp.maximum(i * halo_per_tile - 1, 0), 0)),
            pl.BlockSpec((tm, width), lambda i: (i, 0)),
            pl.BlockSpec((BF16_ROWS, width),
                         lambda i: (jnp.minimum((i + 1) * halo_per_tile, last_halo), 0)),
            pl.BlockSpec((3, width), lambda i: (0, 0)),
            pl.BlockSpec((1, width), lambda i: (0, 0)),
        ],
        out_specs=[pl.BlockSpec((tm, D_HYENA), lambda i: (i, 0)),
                   pl.BlockSpec((tm, D_HYENA), lambda i: (i, 0))],
        out_shape=[jax.ShapeDtypeStruct((T, D_HYENA), BF16),
                   jax.ShapeDtypeStruct((T, D_HYENA), BF16)],
        compiler_params=_params("parallel"),
    )(proj, proj, proj, conv_w, conv_b.reshape(1, width))


def _filter_kernel(band_ref, w1t_ref, w1c_ref, w1s_ref, b1_ref, w2_ref, b2_ref, w3_ref, b3_ref,
                   w4_ref, freq_ref, delta_ref, rev_ref, of_ref, ob_ref, *, seq_len, tr):
    i = pl.program_id(0)
    ext = tr + LANES

    def tap_index(shape, axis):
        return (i * tr + lax.broadcasted_iota(jnp.int32, shape, axis)).astype(F32)

    idx = tap_index((1, ext), 1)
    t = idx * (1.0 / (seq_len - 1))
    arg = band_ref[...] * ((2.0 * math.pi / seq_len) * idx)
    fr = freq_ref[...]

    def dense(w_ref_, a):
        return jnp.dot(w_ref_[...], a, precision=HIGHEST, preferred_element_type=F32)

    h = w1t_ref[...] * t + dense(w1c_ref, jnp.cos(arg)) - dense(w1s_ref, jnp.sin(arg))
    h = jnp.sin(fr * (h + b1_ref[...]))
    h = jnp.sin(fr * (dense(w2_ref, h) + b2_ref[...]))
    h = jnp.sin(fr * (dense(w3_ref, h) + b3_ref[...])).astype(BF16)
    taps = lax.dot_general(h, w4_ref[...].astype(BF16), (((0,), (0,)), ((), ())),
                           preferred_element_type=F32)
    window = jnp.exp(-(tap_index((ext, 1), 0) * (1.0 / (seq_len - 1))) * delta_ref[...])
    of_ref[...] = (taps[:tr, :D_HYENA] * window[:tr]).astype(of_ref.dtype)
    bwd = (taps[:, D_HYENA:] * window).astype(BF16)
    rows = lax.broadcasted_iota(jnp.int32, (tr, 1), 0)
    is_tap_l = (rows == 0) & (i == pl.num_programs(0) - 1)
    reversed_taps = jnp.dot(rev_ref[...], bwd, preferred_element_type=F32)
    ob_ref[...] = jnp.where(is_tap_l, 0.0, reversed_taps).astype(ob_ref.dtype)


def _hyena_filter(seq_len, w1, b1, w2, b2, w3, b3, w4, freq, tr=1024):
    n_tiles = seq_len // tr
    bands = np.linspace(1e-4, FILTER_BANDS - 1, FILTER_BANDS, dtype=np.float32).reshape(-1, 1)
    deltas = np.abs(np.linspace(math.log(DECAY_TARGET) / SLOW_DECAY_PCT,
                                math.log(DECAY_TARGET) / FAST_DECAY_PCT, D_HYENA,
                                dtype=np.float32)).reshape(1, D_HYENA)
    reverse = (np.arange(tr)[:, None] + np.arange(tr + LANES)[None, :] == tr).astype(np.float32)
    const = lambda i: (0, 0)
    col = lambda v: v.reshape(-1, 1)
    full = lambda a: pl.BlockSpec(a.shape, const)
    operands = [jnp.asarray(bands), col(w1[0]), w1[1:1 + FILTER_BANDS].T,
                w1[1 + FILTER_BANDS:].T, col(b1), w2.T, col(b2), w3.T, col(b3),
                w4, col(freq), jnp.asarray(deltas), jnp.asarray(reverse, BF16)]
    out = jax.ShapeDtypeStruct((seq_len, D_HYENA), BF16)
    return pl.pallas_call(
        functools.partial(_filter_kernel, seq_len=seq_len, tr=tr),
        grid=(n_tiles,),
        in_specs=[full(a) for a in operands],
        out_specs=[pl.BlockSpec((tr, D_HYENA), lambda i: (i, 0)),
                   pl.BlockSpec((tr, D_HYENA), lambda i: (n_tiles - 1 - i, 0))],
        out_shape=[out, out],
        compiler_params=_params("parallel"),
    )(*operands)


class _FftPlan:
    def __init__(self, seq_len):
        self.n = 2 * seq_len
        self.fast = FFT_FAST
        self.slow = self.n // FFT_FAST
        self.slow_half = self.slow // 2
        self.ks = self.slow_half + 1
        self.ks_pad = -(-self.ks // SUBLANES) * SUBLANES
        self.ks_group = max(g for g in range(1, FFT_KS_GROUP_MAX + 1) if self.ks_pad % g == 0)
        S, F, N = self.slow, self.fast, self.n
        ks = np.arange(self.ks_pad, dtype=np.float64)[:, None]
        valid = (ks < self.ks)

        def stage_a(n_s):
            s = np.arange(n_s, dtype=np.float64)[None, :]
            ang = 2.0 * np.pi * ks * s / S
            return np.concatenate([np.where(valid, np.cos(ang), 0.0),
                                   np.where(valid, -np.sin(ang), 0.0)], axis=0)

        self.a_half = stage_a(self.slow_half).astype(np.float32)
        self.a_full = stage_a(self.slow).astype(np.float32)
        s = np.arange(self.slow_half, dtype=np.float64)[:, None]
        kk = np.arange(self.ks_pad, dtype=np.float64)[None, :]
        weight = np.where((kk == 0) | (kk == self.slow_half), 1.0, 2.0) * (kk < self.ks) / N
        ang = 2.0 * np.pi * s * kk / S
        self.a_inv = np.concatenate([weight * np.cos(ang), -weight * np.sin(ang)],
                                    axis=1).astype(np.float32)
        f = np.arange(F, dtype=np.float64)
        ang = 2.0 * np.pi * np.outer(np.arange(self.ks_pad, dtype=np.float64), f) / N
        self.tw_re = np.cos(ang).reshape(-1, 1).astype(np.float32)
        self.tw_im = (-np.sin(ang)).reshape(-1, 1).astype(np.float32)
        ang = 2.0 * np.pi * np.outer(f, f) / F
        self.f_re = np.cos(ang).astype(np.float32)
        self.f_im = (-np.sin(ang)).astype(np.float32)


def _dft_dot(a_ref, x):
    return jnp.dot(a_ref[...], x.astype(BF16), preferred_element_type=F32)


def _stage_a_kernel(a_ref, *refs, batch):
    x_refs, o_ref, in_stage, out_stage = refs[:-3], refs[-3], refs[-2], refs[-1]
    m, k = a_ref.shape
    kp = k // len(x_refs)
    x2ds = [x_ref.reshape(batch * kp * FFT_F_TILE, LANES) for x_ref in x_refs]
    o2d = o_ref.reshape(batch * m * FFT_F_TILE, LANES)
    for b in range(batch):
        for n, x2d in enumerate(x2ds):
            rows = x2d[pl.ds(b * kp * FFT_F_TILE, kp * FFT_F_TILE), :]
            in_stage[pl.ds(n * kp * FFT_F_TILE, kp * FFT_F_TILE), :] = rows.astype(F32)
        slabs = [jnp.concatenate(
            [in_stage[pl.ds(n * kp * FFT_F_TILE + j, kp, stride=FFT_F_TILE), :]
             for n in range(len(x2ds))], axis=0) for j in range(FFT_F_TILE)]
        r = _dft_dot(a_ref, jnp.concatenate(slabs, axis=-1))
        for j in range(FFT_F_TILE):
            out_stage[pl.ds(j, m, stride=FFT_F_TILE), :] = r[:, j * LANES:(j + 1) * LANES]
        o2d[pl.ds(b * m * FFT_F_TILE, m * FFT_F_TILE), :] = out_stage[...].astype(o_ref.dtype)


def _stage_a(a, xs, inverse):
    F, C = xs[0].shape[-2:]
    B = xs[0].shape[0]
    M, K = a.shape
    x_dims, o_dims = ((2, K // 2), (M,)) if inverse else ((K // len(xs),), (2, M // 2))

    def spec(dims):
        zeros = (0,) * len(dims)
        return pl.BlockSpec((B,) + dims + (FFT_F_TILE, LANES), lambda f, c: (0,) + zeros + (f, c))

    return pl.pallas_call(
        functools.partial(_stage_a_kernel, batch=B),
        grid=(F // FFT_F_TILE, C // LANES),
        in_specs=[pl.BlockSpec((M, K), lambda f, c: (0, 0))] + [spec(x_dims)] * len(xs),
        out_specs=spec(o_dims),
        out_shape=jax.ShapeDtypeStruct((B,) + o_dims + (F, C), BF16),
        scratch_shapes=[pltpu.VMEM((K * FFT_F_TILE, LANES), F32),
                        pltpu.VMEM((M * FFT_F_TILE, LANES), F32)],
        compiler_params=_params("parallel", "parallel"),
    )(jnp.asarray(a, BF16), *xs)


def _stage_b_dft(fr_ref, fi_ref, tiles, conj=False):
    w = tiles[0][0].shape[-1]
    both = jnp.concatenate([part for tile in tiles for part in tile], axis=-1)
    pr = _dft_dot(fr_ref, both)
    pi = _dft_dot(fi_ref, both)
    out = []
    for g in range(len(tiles)):
        re, im = slice(2 * g * w, (2 * g + 1) * w), slice((2 * g + 1) * w, (2 * g + 2) * w)
        if conj:
            out.append((pr[:, re] + pi[:, im], pr[:, im] - pi[:, re]))
        else:
            out.append((pr[:, re] - pi[:, im], pr[:, im] + pi[:, re]))
    return out


def _twiddled_tiles(a_ref, twr_ref, twi_ref):
    tiles, twiddles = [], []
    for g in range(a_ref.shape[2] // FFT_FAST):
        rows = slice(g * FFT_FAST, (g + 1) * FFT_FAST)
        tr, ti = twr_ref[rows, :], twi_ref[rows, :]
        for c in range(a_ref.shape[3] // LANES):
            at = rows, slice(c * LANES, (c + 1) * LANES)
            ar, ai = a_ref[(0, 0) + at].astype(F32), a_ref[(0, 1) + at].astype(F32)
            tiles.append((ar * tr - ai * ti, ar * ti + ai * tr))
            twiddles.append((at, tr, ti))
    return tiles, twiddles


def _fft_fwd_kernel(a_ref, twr_ref, twi_ref, fr_ref, fi_ref, o_ref):
    tiles, twiddles = _twiddled_tiles(a_ref, twr_ref, twi_ref)
    for (at, _, _), (xr, xi) in zip(twiddles, _stage_b_dft(fr_ref, fi_ref, tiles)):
        o_ref[(0, 0) + at] = xr.astype(o_ref.dtype)
        o_ref[(0, 1) + at] = xi.astype(o_ref.dtype)


def _fft_conv_kernel(a_ref, k_ref, twr_ref, twi_ref, fr_ref, fi_ref, o_ref):
    tiles, twiddles = _twiddled_tiles(a_ref, twr_ref, twi_ref)
    products = []
    for (at, _, _), (xr, xi) in zip(twiddles, _stage_b_dft(fr_ref, fi_ref, tiles)):
        kr, ki = k_ref[(0, 0) + at].astype(F32), k_ref[(0, 1) + at].astype(F32)
        products.append((xr * kr - xi * ki, xr * ki + xi * kr))
    inverse = _stage_b_dft(fr_ref, fi_ref, products, conj=True)
    for (at, tr, ti), (br, bi) in zip(twiddles, inverse):
        o_ref[(0, 0) + at] = (br * tr + bi * ti).astype(o_ref.dtype)
        o_ref[(0, 1) + at] = (bi * tr - br * ti).astype(o_ref.dtype)


def _fft_stage_b(plan, a, k=None):
    B, _, R, C = a.shape
    rows = plan.ks_group * FFT_FAST
    data = pl.BlockSpec((1, 2, rows, FFT_B_LANES), lambda r, c, b: (b, 0, r, c))
    tw = pl.BlockSpec((rows, LANES), lambda r, c, b: (r, 0))
    mat = pl.BlockSpec((FFT_FAST, FFT_FAST), lambda r, c, b: (0, 0))
    lane_bcast = lambda col: jnp.broadcast_to(jnp.asarray(col), (col.shape[0], LANES))
    consts = (lane_bcast(plan.tw_re), lane_bcast(plan.tw_im),
              jnp.asarray(plan.f_re, BF16), jnp.asarray(plan.f_im, BF16))
    common = dict(
        grid=(R // rows, C // FFT_B_LANES, B),
        out_specs=data,
        out_shape=jax.ShapeDtypeStruct(a.shape, BF16),
        compiler_params=_params("parallel", "parallel", "arbitrary"),
    )
    if k is None:
        return pl.pallas_call(_fft_fwd_kernel, in_specs=[data, tw, tw, mat, mat],
                              **common)(a, *consts)
    kspec = pl.BlockSpec((1, 2, rows, FFT_B_LANES), lambda r, c, b: (0, 0, r, c))
    return pl.pallas_call(_fft_conv_kernel, in_specs=[data, kspec, tw, tw, mat, mat],
                          **common)(a, k, *consts)


def _fft_long_conv(z, kern_halves, plan):
    B, L, C = z.shape
    S, F, P = plan.slow, plan.fast, plan.ks_pad
    ka = _stage_a(plan.a_full, [h.reshape(1, S // 2, F, C) for h in kern_halves], inverse=False)
    k_spec = _fft_stage_b(plan, ka.reshape(1, 2, P * F, C))
    za = _stage_a(plan.a_half, [z.reshape(B, S // 2, F, C)], inverse=False)
    ya = _fft_stage_b(plan, za.reshape(B, 2, P * F, C), k_spec)
    y = _stage_a(plan.a_inv, [ya.reshape(B, 2, P, F, C)], inverse=True)
    return y.reshape(B, L, C)


def _scan_constants():
    C = SCAN_CHUNK
    n_lv = len(SCAN_LEVELS)
    tri = np.tril(np.ones((C, C), np.float32))
    upper = np.zeros((n_lv, C, 1), np.float32)
    mask = np.zeros((n_lv + 1, C, C), np.float32)
    for li, h in enumerate(SCAN_LEVELS):
        for t in range(C):
            r = (t // (2 * h)) * 2 * h + h
            if t % (2 * h) >= h:
                upper[li, t, 0] = 1.0
                mask[li, t, r - h:r] = 1.0
    mask[n_lv] = np.eye(C, dtype=np.float32)
    flip = lambda m: m[:, ::-1, ::-1]
    return (np.stack([tri, tri[::-1, ::-1]]), np.stack([upper, upper[:, ::-1]]),
            np.stack([mask, flip(mask)]))


def _ref_rows(h, reverse):
    rows = []
    for g in range(SCAN_CHUNK // SUBLANES):
        pair = []
        for t in (g * SUBLANES, g * SUBLANES + SUBLANES // 2):
            start = (t // (2 * h)) * 2 * h
            pair.append(start + h if reverse else start + h - 1)
        rows.append(tuple(pair))
    return rows


def _scan_chunk(q_ref, f_ref, v_ref, lb_table, tri, up_ref, mask_ref, o_ref, b_ref, state_ref,
                *, layer, reverse):
    n_lv = len(SCAN_LEVELS)
    C = SCAN_CHUNK

    e = jnp.exp(lb_table - jnp.max(lb_table, axis=0, keepdims=True))
    prob = e / jnp.sum(e, axis=0, keepdims=True)
    lb = jnp.zeros((1, D_HGRN), F32)
    for l in range(1, layer + 1):
        lb = lb + prob[l:l + 1, :]

    one = jnp.ones((), BF16)
    q = q_ref[...]
    half = jnp.asarray(0.5, BF16)
    half_q = half * q
    q = (half_q + half_q * jnp.tanh(half_q)).astype(F32)
    z = f_ref[...]
    log_sig = jnp.minimum(z, 0) - jnp.log(one + jnp.exp(-jnp.abs(z)))
    key = ((half - half * lb.astype(BF16)) * (one - jnp.tanh(half * z))).astype(F32)
    log_a = jnp.log(lb).astype(BF16)
    log_b = jnp.log1p(-lb).astype(BF16) + log_sig
    gate = jnp.maximum(log_a, log_b) + jnp.log(one + jnp.exp(-jnp.abs(log_a - log_b)))
    vb = v_ref[...]

    gate2 = gate * jnp.asarray(math.log2(math.e), BF16)
    b_inc = jnp.dot(tri, gate2, preferred_element_type=F32)
    for hd in range(HGRN_HEADS):
        b_ref[hd] = b_inc[:, hd * HEAD_DIM:(hd + 1) * HEAD_DIM]

    def bcast_row(r):
        return jnp.concatenate([b_ref.at[hd][pl.ds(r, SUBLANES, stride=0), :]
                                for hd in range(HGRN_HEADS)], axis=-1)

    def by_role(h):
        blocks = []
        for start in range(0, C, h):
            upper = (start // h) % 2 == 1
            src = q if upper != reverse else key
            blocks.append(src[start:start + h])
        return jnp.concatenate(blocks, axis=0)

    first_half = lax.broadcasted_iota(jnp.int32, (SUBLANES, 1), 0) < SUBLANES // 2
    scores = [None] * HGRN_HEADS
    for li, h in enumerate(SCAN_LEVELS):
        if h == 1:
            x = jnp.where(up_ref[li] > 0.5, q * (1.0 - key), key)
        else:
            groups = []
            for r0, r1 in _ref_rows(h, reverse):
                ref = bcast_row(r0)
                groups.append(ref if r1 == r0 else jnp.where(first_half, ref, bcast_row(r1)))
            diff = pltpu.bitcast(b_inc - jnp.concatenate(groups, axis=0), jnp.int32)
            neg_dist = pltpu.bitcast(diff | jnp.int32(-2 ** 31), F32)
            roles = by_role(h) if h >= SUBLANES else jnp.where(up_ref[li] > 0.5, q, key)
            x = roles * jnp.exp2(neg_dist)
        xb = x.astype(BF16)
        keep = mask_ref[li] > 0.5
        for hd in range(HGRN_HEADS):
            xh = xb[:, hd * HEAD_DIM:(hd + 1) * HEAD_DIM]
            p = lax.dot_general(xh, xh, (((1,), (1,)), ((), ())), preferred_element_type=F32)
            scores[hd] = jnp.where(keep, p, 0.0 if scores[hd] is None else scores[hd])

    b_total = bcast_row(0 if reverse else C - 1)
    b_rest = jnp.concatenate([b_total] * (C // SUBLANES), axis=0) - b_inc
    q_dec = (q * jnp.exp2(b_inc)).astype(BF16)
    k_dec = (key * jnp.exp2(b_rest)).astype(BF16)
    total = jnp.exp2(b_total[0:1, :])
    qb, kb = q.astype(BF16), key.astype(BF16)
    on_diag = mask_ref[n_lv] > 0.5
    for hd in range(HGRN_HEADS):
        cols = slice(hd * HEAD_DIM, (hd + 1) * HEAD_DIM)
        diag = lax.dot_general(qb[:, cols], kb[:, cols], (((1,), (1,)), ((), ())),
                               preferred_element_type=F32)
        p = jnp.where(on_diag, diag, scores[hd]).astype(BF16)
        st = state_ref[hd]
        o = jnp.dot(p, vb[:, cols], preferred_element_type=F32)
        o = o + lax.dot_general(q_dec[:, cols], st.astype(BF16), (((1,), (1,)), ((), ())),
                                preferred_element_type=F32)
        o_ref[:, cols] = o.astype(o_ref.dtype)
        upd = lax.dot_general(vb[:, cols], k_dec[:, cols], (((0,), (0,)), ((), ())),
                              preferred_element_type=F32)
        state_ref[hd] = st * total[:, cols] + upd


def _hgrn_scan_kernel(qf_ref, ff_ref, vf_ref, qb_ref, fb_ref, vb_ref, lbt_ref, tri_ref, up_ref,
                      mask_ref, of_ref, ob_ref, b_ref, state_ref, *, layer):
    @pl.when(pl.program_id(1) == 0)
    def _():
        state_ref[...] = jnp.zeros_like(state_ref)

    for s in range(SCAN_CHUNKS_PER_STEP):
        rows = pl.ds(s * SCAN_CHUNK, SCAN_CHUNK)
        _scan_chunk(qf_ref.at[rows], ff_ref.at[rows], vf_ref.at[rows], lbt_ref[0], tri_ref[0],
                    up_ref.at[0], mask_ref.at[0], of_ref.at[rows], b_ref.at[0], state_ref.at[0],
                    layer=layer, reverse=False)
        rows = pl.ds((SCAN_CHUNKS_PER_STEP - 1 - s) * SCAN_CHUNK, SCAN_CHUNK)
        _scan_chunk(qb_ref.at[rows], fb_ref.at[rows], vb_ref.at[rows], lbt_ref[1], tri_ref[1],
                    up_ref.at[1], mask_ref.at[1], ob_ref.at[rows], b_ref.at[1], state_ref.at[1],
                    layer=layer, reverse=True)


def _hgrn_scan(proj, lb_table, layer, batch, seq_len):
    T = proj.shape[0]
    C = SCAN_CHUNK
    rows_per_step = SCAN_CHUNKS_PER_STEP * C
    n_chunks = seq_len // rows_per_step
    tri, upper, mask = _scan_constants()
    col0 = (3 * D_HYENA) // D_HGRN

    fwd = lambda b, c: b * n_chunks + c
    bwd = lambda b, c: b * n_chunks + n_chunks - 1 - c
    chunk = lambda rows, col: pl.BlockSpec((rows_per_step, D_HGRN), lambda b, c: (rows(b, c), col))
    whole = lambda a: pl.BlockSpec(a.shape, lambda b, c: (0,) * a.ndim)
    out = jax.ShapeDtypeStruct((T, D_HGRN), BF16)
    return pl.pallas_call(
        functools.partial(_hgrn_scan_kernel, layer=layer),
        grid=(batch, n_chunks),
        in_specs=[
            chunk(fwd, col0), chunk(fwd, col0 + 1), chunk(fwd, col0 + 3),
            chunk(bwd, col0), chunk(bwd, col0 + 2), chunk(bwd, col0 + 3),
            whole(lb_table), whole(tri), whole(upper), whole(mask),
        ],
        out_specs=[chunk(fwd, 0), chunk(bwd, 0)],
        out_shape=[out, out],
        scratch_shapes=[pltpu.VMEM((2, HGRN_HEADS, C, HEAD_DIM), F32),
                        pltpu.VMEM((2, HGRN_HEADS, HEAD_DIM, HEAD_DIM), F32)],
        compiler_params=_params("parallel", "arbitrary"),
    )(proj, proj, proj, proj, proj, proj, lb_table, jnp.asarray(tri, BF16), jnp.asarray(upper),
      jnp.asarray(mask))


def _group_mean_matrix(group):
    idx = np.arange(D_HYENA) // group
    return (idx[:, None] == idx[None, :]).astype(np.float32)


def _mix_out_kernel(x_ref, y_ref, z_ref, x0_ref, of_ref, ob_ref, g_ref, skip_ref,
                    hy_gain_ref, hg_gain_ref, grp_hy_ref, grp_hg_ref, w_ref, post_ref, o_ref):
    z = z_ref[...].astype(F32)
    yh = x0_ref[...].astype(F32) * (y_ref[...].astype(F32) + skip_ref[...] * z)
    ms = _group_sums(yh * yh, grp_hy_ref[...]) * (HYENA_GROUPS / D_HYENA)
    yh = yh * lax.rsqrt(ms + EPS) * hy_gain_ref[...]
    o = of_ref[...].astype(F32) + ob_ref[...].astype(F32)
    ms = _group_sums(o * o, grp_hg_ref[...]) * (1.0 / HEAD_DIM)
    g = g_ref[...].astype(F32)
    half_g = 0.5 * g
    silu_g = half_g + half_g * jnp.tanh(half_g)
    o = o * lax.rsqrt(ms + EPS) * hg_gain_ref[...] * silu_g
    mix = (jnp.dot(yh.astype(BF16), w_ref[:D_HYENA, :], preferred_element_type=F32)
           + jnp.dot(o.astype(BF16), w_ref[D_HYENA:, :], preferred_element_type=F32))
    o_ref[...] = x_ref[...] + mix * _rms_scale(mix) * post_ref[...]


def _mix_out(x2d, y, z, x0, o_fwd, o_bwd, proj, skip, hy_gain, hg_gain, w_bf16, layer,
             post_gain, tm=1024):
    T = x2d.shape[0]
    gate_col = D_IN // D_HGRN - 1
    half = lambda: pl.BlockSpec((tm, D_HYENA), lambda i: (i, 0))
    vec = lambda n: pl.BlockSpec((1, n), lambda i: (0, 0))
    sq = lambda: pl.BlockSpec((D_HYENA, D_HYENA), lambda i: (0, 0))
    row = lambda v: v.reshape(1, -1)
    return pl.pallas_call(
        _mix_out_kernel,
        grid=(T // tm,),
        in_specs=[
            pl.BlockSpec((tm, D_MODEL), lambda i: (i, 0)),
            half(), half(), half(), half(), half(),
            pl.BlockSpec((tm, D_HGRN), lambda i: (i, gate_col)),
            vec(D_HYENA), vec(D_HYENA), vec(D_HGRN), sq(), sq(),
            pl.BlockSpec((None, D_MODEL, D_MODEL), lambda i: (layer, 0, 0)),
            vec(D_MODEL),
        ],
        out_specs=pl.BlockSpec((tm, D_MODEL), lambda i: (i, 0)),
        out_shape=jax.ShapeDtypeStruct((T, D_MODEL), F32),
        compiler_params=_params("parallel"),
    )(x2d, y, z, x0, o_fwd, o_bwd, proj, row(skip), row(hy_gain), row(hg_gain),
      jnp.asarray(_group_mean_matrix(D_HYENA // HYENA_GROUPS), BF16),
      jnp.asarray(_group_mean_matrix(HEAD_DIM), BF16), w_bf16, row(post_gain))


def _gelu_tanh(x):
    c = math.sqrt(2.0 / math.pi)
    half = 0.5 * x
    return half + half * jnp.tanh(x * (c + (0.044715 * c) * (x * x)))


def _ffn_kernel(prev_ref, main_ref, next_ref, pre_ref, wa_ref, wb_ref, cwa_ref, cwb_ref,
                cba_ref, cbb_ref, wd_ref, post_ref, o_ref, h_ref, acc_ref, ua_ref, ub_ref,
                *, tiles_per_seq, tm):
    j = pl.program_id(1)
    pos = pl.program_id(0) % tiles_per_seq

    @pl.when(j == 0)
    def _():
        gain = pre_ref[...]

        def normed(x):
            return (x * _rms_scale(x) * gain).astype(BF16)

        prev = normed(prev_ref[...])
        nxt = normed(next_ref[...])
        h_ref[0:FFN_HALO, :] = jnp.where(pos == 0, jnp.zeros_like(prev), prev)
        h_ref[FFN_HALO:FFN_HALO + tm, :] = normed(main_ref[...])
        h_ref[FFN_HALO + tm:, :] = jnp.where(pos == tiles_per_seq - 1, jnp.zeros_like(nxt), nxt)
        acc_ref[...] = jnp.zeros_like(acc_ref)

    h = h_ref[...]

    def conv_part(w_ref_, cw_ref_, cb_ref_, u_ref):
        u_ref[...] = jnp.dot(h, w_ref_[...], preferred_element_type=F32)
        cw = cw_ref_[...]
        down = u_ref[pl.ds(FFN_HALO - 1, tm), :]
        mid = u_ref[pl.ds(FFN_HALO, tm), :]
        up = u_ref[pl.ds(FFN_HALO + 1, tm), :]
        return down * cw[0:1, :] + mid * cw[1:2, :] + up * cw[2:3, :] + cb_ref_[...]

    a = conv_part(wa_ref, cwa_ref, cba_ref, ua_ref)
    b = conv_part(wb_ref, cwb_ref, cbb_ref, ub_ref)
    act = _gelu_tanh(a.astype(BF16)) * b.astype(BF16)
    acc_ref[...] += jnp.dot(act, wd_ref[...], preferred_element_type=F32)

    @pl.when(j == pl.num_programs(1) - 1)
    def _():
        ff = acc_ref[...]
        o_ref[...] = main_ref[...] + ff * _rms_scale(ff) * post_ref[...]


def _ffn(x2d, pre_gain, w_up_bf16, conv_w, conv_b, w_down_bf16, layer, post_gain, seq_len,
         tm=1024, tf=1024):
    T = x2d.shape[0]
    conv_b = conv_b.reshape(conv_b.shape[0], 1, -1)
    halo_per_tile = tm // FFN_HALO
    last_halo = T // FFN_HALO - 1
    n_f = D_FF // tf
    row = lambda v: v.reshape(1, -1)
    return pl.pallas_call(
        functools.partial(_ffn_kernel, tiles_per_seq=seq_len // tm, tm=tm),
        grid=(T // tm, n_f),
        in_specs=[
            pl.BlockSpec((FFN_HALO, D_MODEL),
                         lambda i, j: (jnp.maximum(i * halo_per_tile - 1, 0), 0)),
            pl.BlockSpec((tm, D_MODEL), lambda i, j: (i, 0)),
            pl.BlockSpec((FFN_HALO, D_MODEL),
                         lambda i, j: (jnp.minimum((i + 1) * halo_per_tile, last_halo), 0)),
            pl.BlockSpec((1, D_MODEL), lambda i, j: (0, 0)),
            pl.BlockSpec((None, D_MODEL, tf), lambda i, j: (layer, 0, j)),
            pl.BlockSpec((None, D_MODEL, tf), lambda i, j: (layer, 0, n_f + j)),
            pl.BlockSpec((None, 3, tf), lambda i, j: (layer, 0, j)),
            pl.BlockSpec((None, 3, tf), lambda i, j: (layer, 0, n_f + j)),
            pl.BlockSpec((None, 1, tf), lambda i, j: (layer, 0, j)),
            pl.BlockSpec((None, 1, tf), lambda i, j: (layer, 0, n_f + j)),
            pl.BlockSpec((None, tf, D_MODEL), lambda i, j: (layer, j, 0)),
            pl.BlockSpec((1, D_MODEL), lambda i, j: (0, 0)),
        ],
        out_specs=pl.BlockSpec((tm, D_MODEL), lambda i, j: (i, 0)),
        out_shape=jax.ShapeDtypeStruct((T, D_MODEL), F32),
        scratch_shapes=[pltpu.VMEM((tm + 2 * FFN_HALO, D_MODEL), BF16),
                        pltpu.VMEM((tm, D_MODEL), F32)]
        + [pltpu.VMEM((tm + 2 * FFN_HALO, tf), F32)] * 2,
        compiler_params=_params("parallel", "arbitrary"),
    )(x2d, x2d, x2d, row(pre_gain), w_up_bf16, w_up_bf16, conv_w, conv_w,
      conv_b, conv_b, w_down_bf16, row(post_gain))


def _trunk(x, p):
    B, L, _ = x.shape
    x2d = x.reshape(B * L, D_MODEL)
    plan = _FftPlan(L)
    for l in range(p["w_in"].shape[0]):
        proj = _in_proj(x2d, p["norm_mix_pre"][l], p["w_in_bf16"], l)
        x0, z = _hyena_pre(proj, p["hyena_conv_w"][l], p["hyena_conv_b"][l], L)
        kern = _hyena_filter(L, p["filt_w1"][l], p["filt_b1"][l], p["filt_w2"][l], p["filt_b2"][l],
                             p["filt_w3"][l], p["filt_b3"][l], p["filt_w4"][l], p["filt_freq"][l])
        y = _fft_long_conv(z.reshape(B, L, D_HYENA), kern, plan).reshape(B * L, D_HYENA)
        o_fwd, o_bwd = _hgrn_scan(proj, p["hgrn_lower_bounds"], l, B, L)
        x2d = _mix_out(x2d, y, z, x0, o_fwd, o_bwd, proj, p["hyena_skip"][l], p["hyena_out_norm"][l],
                       p["hgrn_out_norm"][l], p["w_out_bf16"], l, p["norm_mix_post"][l])
        x2d = _ffn(x2d, p["norm_ffn_pre"][l], p["ffn_w_up_bf16"], p["ffn_conv_w"],
                   p["ffn_conv_b"], p["ffn_w_down_bf16"], l, p["norm_ffn_post"][l], L)
    return x2d.reshape(B, L, D_MODEL)


def kernel(x_prompt, x_sample, norm_mix_pre, norm_mix_post, norm_ffn_pre, norm_ffn_post, w_in, hyena_conv_w, hyena_conv_b, filt_w1, filt_b1, filt_w2, filt_b2, filt_w3, filt_b3, filt_w4, filt_freq, hyena_skip, hyena_out_norm, hgrn_lower_bounds, hgrn_out_norm, w_out, ffn_w_up, ffn_conv_w, ffn_conv_b, ffn_w_down):
    p = dict(
        norm_mix_pre=norm_mix_pre, norm_mix_post=norm_mix_post, norm_ffn_pre=norm_ffn_pre,
        norm_ffn_post=norm_ffn_post, w_in=w_in, hyena_conv_w=hyena_conv_w,
        hyena_conv_b=hyena_conv_b, filt_w1=filt_w1, filt_b1=filt_b1, filt_w2=filt_w2,
        filt_b2=filt_b2, filt_w3=filt_w3, filt_b3=filt_b3, filt_w4=filt_w4, filt_freq=filt_freq,
        hyena_skip=hyena_skip, hyena_out_norm=hyena_out_norm,
        hgrn_lower_bounds=hgrn_lower_bounds, hgrn_out_norm=hgrn_out_norm,
        ffn_conv_w=ffn_conv_w, ffn_conv_b=ffn_conv_b,
        w_in_bf16=w_in.astype(BF16), w_out_bf16=w_out.astype(BF16),
        ffn_w_up_bf16=ffn_w_up.astype(BF16), ffn_w_down_bf16=ffn_w_down.astype(BF16),
    )
    return (_trunk(x_prompt, p), _trunk(x_sample, p))
```

```python
import functools
import math

import numpy as np
import jax
import jax.numpy as jnp
from jax import lax
from jax.experimental import pallas as pl
from jax.experimental.pallas import tpu as pltpu

F32 = jnp.float32
BF16 = jnp.bfloat16
HIGHEST = lax.Precision.HIGHEST

D_MODEL = 1024
D_HYENA = 512
HYENA_GROUPS = 8
D_HGRN = 512
HGRN_HEADS = 4
HEAD_DIM = 128
D_IN = 3 * D_HYENA + 5 * D_HGRN
D_FF = 4 * D_MODEL
FILTER_EMB = 33
FILTER_BANDS = 16
FILTER_HIDDEN = 64
DECAY_TARGET = 1e-2
FAST_DECAY_PCT = 0.3
SLOW_DECAY_PCT = 1.5
EPS = 1e-6

SUBLANES = 8
LANES = 128
BF16_ROWS = 16
VMEM_LIMIT_BYTES = 56 * 1024 * 1024

FFT_FAST = 128
FFT_KS_GROUP_MAX = 20
FFT_F_TILE = 16
FFT_B_LANES = 256
FFN_HALO = 16
SCAN_CHUNK = 128
SCAN_CHUNKS_PER_STEP = 4
SCAN_LEVELS = (64, 32, 16, 8, 4, 2, 1)


def _params(*semantics):
    return pltpu.CompilerParams(dimension_semantics=semantics,
                                vmem_limit_bytes=VMEM_LIMIT_BYTES)


def _rms_scale(x):
    return lax.rsqrt(jnp.mean(x * x, axis=-1, keepdims=True) + EPS)


def _group_sums(sq, same_group):
    return jnp.dot(sq.astype(BF16), same_group, preferred_element_type=F32)


def _in_proj_kernel(x_ref, g_ref, w_ref, o_ref, h_ref):
    @pl.when(pl.program_id(1) == 0)
    def _():
        x = x_ref[...]
        h_ref[...] = (x * _rms_scale(x) * g_ref[...]).astype(BF16)

    o_ref[...] = jnp.dot(h_ref[...], w_ref[...], preferred_element_type=F32).astype(o_ref.dtype)


def _in_proj(x2d, gain, w_bf16, layer, tm=1024, tn=4096):
    T = x2d.shape[0]
    N = w_bf16.shape[2]
    return pl.pallas_call(
        _in_proj_kernel,
        grid=(T // tm, N // tn),
        in_specs=[
            pl.BlockSpec((tm, D_MODEL), lambda i, j: (i, 0)),
            pl.BlockSpec((1, D_MODEL), lambda i, j: (0, 0)),
            pl.BlockSpec((None, D_MODEL, tn), lambda i, j: (layer, 0, j)),
        ],
        out_specs=pl.BlockSpec((tm, tn), lambda i, j: (i, j)),
        out_shape=jax.ShapeDtypeStruct((T, N), BF16),
        scratch_shapes=[pltpu.VMEM((tm, D_MODEL), BF16)],
        compiler_params=_params("parallel", "arbitrary"),
    )(x2d, gain.reshape(1, D_MODEL), w_bf16)


def _shift_rows(main, prev_row, next_row):
    tm = main.shape[0]
    rows = lax.broadcasted_iota(jnp.int32, (tm, 1), 0)
    down = jnp.where(rows == 0, prev_row, pltpu.roll(main, 1, 0))
    up = jnp.where(rows == tm - 1, next_row, pltpu.roll(main, tm - 1, 0))
    return down, up


def _hyena_pre_kernel(prev_ref, main_ref, next_ref, w_ref, b_ref, x0_ref, z_ref,
                      *, tiles_per_seq):
    pos = pl.program_id(0) % tiles_per_seq
    main = main_ref[...].astype(F32)
    prev_row = jnp.where(pos == 0, 0.0, prev_ref[...].astype(F32)[BF16_ROWS - 1:BF16_ROWS, :])
    next_row = jnp.where(pos == tiles_per_seq - 1, 0.0, next_ref[...].astype(F32)[0:1, :])
    down, up = _shift_rows(main, prev_row, next_row)
    w = w_ref[...]
    u = down * w[0:1, :] + main * w[1:2, :] + up * w[2:3, :] + b_ref[...]
    x0_ref[...] = u[:, :D_HYENA].astype(x0_ref.dtype)
    z_ref[...] = (u[:, D_HYENA:2 * D_HYENA] * u[:, 2 * D_HYENA:]).astype(z_ref.dtype)


def _hyena_pre(proj, conv_w, conv_b, seq_len, tm=512):
    T = proj.shape[0]
    width = 3 * D_HYENA
    halo_per_tile = tm // BF16_ROWS
    last_halo = T // BF16_ROWS - 1
    return pl.pallas_call(
        functools.partial(_hyena_pre_kernel, tiles_per_seq=seq_len // tm),
        grid=(T // tm,),
        in_specs=[
            pl.BlockSpec((BF16_ROWS, width),
                         lambda i: (jnp.maximum(i * halo_per_tile - 1, 0), 0)),
            pl.BlockSpec((tm, width), lambda i: (i, 0)),
            pl.BlockSpec((BF16_ROWS, width),
                         lambda i: (jnp.minimum((i + 1) * halo_per_tile, last_halo), 0)),
            pl.BlockSpec((3, width), lambda i: (0, 0)),
            pl.BlockSpec((1, width), lambda i: (0, 0)),
        ],
        out_specs=[pl.BlockSpec((tm, D_HYENA), lambda i: (i, 0)),
                   pl.BlockSpec((tm, D_HYENA), lambda i: (i, 0))],
        out_shape=[jax.ShapeDtypeStruct((T, D_HYENA), BF16),
                   jax.ShapeDtypeStruct((T, D_HYENA), BF16)],
        compiler_params=_params("parallel"),
    )(proj, proj, proj, conv_w, conv_b.reshape(1, width))


def _filter_kernel(band_ref, w1t_ref, w1c_ref, w1s_ref, b1_ref, w2_ref, b2_ref, w3_ref, b3_ref,
                   w4_ref, freq_ref, delta_ref, rev_ref, of_ref, ob_ref, *, seq_len, tr):
    i = pl.program_id(0)
    ext = tr + LANES

    def tap_index(shape, axis):
        return (i * tr + lax.broadcasted_iota(jnp.int32, shape, axis)).astype(F32)

    idx = tap_index((1, ext), 1)
    t = idx * (1.0 / (seq_len - 1))
    arg = band_ref[...] * ((2.0 * math.pi / seq_len) * idx)
    fr = freq_ref[...]

    def dense(w_ref_, a):
        return jnp.dot(w_ref_[...], a, precision=HIGHEST, preferred_element_type=F32)

    h = w1t_ref[...] * t + dense(w1c_ref, jnp.cos(arg)) - dense(w1s_ref, jnp.sin(arg))
    h = jnp.sin(fr * (h + b1_ref[...]))
    h = jnp.sin(fr * (dense(w2_ref, h) + b2_ref[...]))
    h = jnp.sin(fr * (dense(w3_ref, h) + b3_ref[...])).astype(BF16)
    taps = lax.dot_general(h, w4_ref[...].astype(BF16), (((0,), (0,)), ((), ())),
                           preferred_element_type=F32)
    window = jnp.exp(-(tap_index((ext, 1), 0) * (1.0 / (seq_len - 1))) * delta_ref[...])
    of_ref[...] = (taps[:tr, :D_HYENA] * window[:tr]).astype(of_ref.dtype)
    bwd = (taps[:, D_HYENA:] * window).astype(BF16)
    rows = lax.broadcasted_iota(jnp.int32, (tr, 1), 0)
    is_tap_l = (rows == 0) & (i == pl.num_programs(0) - 1)
    reversed_taps = jnp.dot(rev_ref[...], bwd, preferred_element_type=F32)
    ob_ref[...] = jnp.where(is_tap_l, 0.0, reversed_taps).astype(ob_ref.dtype)


def _hyena_filter(seq_len, w1, b1, w2, b2, w3, b3, w4, freq, tr=1024):
    n_tiles = seq_len // tr
    bands = np.linspace(1e-4, FILTER_BANDS - 1, FILTER_BANDS, dtype=np.float32).reshape(-1, 1)
    deltas = np.abs(np.linspace(math.log(DECAY_TARGET) / SLOW_DECAY_PCT,
                                math.log(DECAY_TARGET) / FAST_DECAY_PCT, D_HYENA,
                                dtype=np.float32)).reshape(1, D_HYENA)
    reverse = (np.arange(tr)[:, None] + np.arange(tr + LANES)[None, :] == tr).astype(np.float32)
    const = lambda i: (0, 0)
    col = lambda v: v.reshape(-1, 1)
    full = lambda a: pl.BlockSpec(a.shape, const)
    operands = [jnp.asarray(bands), col(w1[0]), w1[1:1 + FILTER_BANDS].T,
                w1[1 + FILTER_BANDS:].T, col(b1), w2.T, col(b2), w3.T, col(b3),
                w4, col(freq), jnp.asarray(deltas), jnp.asarray(reverse, BF16)]
    out = jax.ShapeDtypeStruct((seq_len, D_HYENA), BF16)
    return pl.pallas_call(
        functools.partial(_filter_kernel, seq_len=seq_len, tr=tr),
        grid=(n_tiles,),
        in_specs=[full(a) for a in operands],
        out_specs=[pl.BlockSpec((tr, D_HYENA), lambda i: (i, 0)),
                   pl.BlockSpec((tr, D_HYENA), lambda i: (n_tiles - 1 - i, 0))],
        out_shape=[out, out],
        compiler_params=_params("parallel"),
    )(*operands)


class _FftPlan:
    def __init__(self, seq_len):
        self.n = 2 * seq_len
        self.fast = FFT_FAST
        self.slow = self.n // FFT_FAST
        self.slow_half = self.slow // 2
        self.ks = self.slow_half + 1
        self.ks_pad = -(-self.ks // SUBLANES) * SUBLANES
        self.ks_group = max(g for g in range(1, FFT_KS_GROUP_MAX + 1) if self.ks_pad % g == 0)
        S, F, N = self.slow, self.fast, self.n
        ks = np.arange(self.ks_pad, dtype=np.float64)[:, None]
        valid = (ks < self.ks)

        def stage_a(n_s):
            s = np.arange(n_s, dtype=np.float64)[None, :]
            ang = 2.0 * np.pi * ks * s / S
            return np.concatenate([np.where(valid, np.cos(ang), 0.0),
                                   np.where(valid, -np.sin(ang), 0.0)], axis=0)

        self.a_half = stage_a(self.slow_half).astype(np.float32)
        self.a_full = stage_a(self.slow).astype(np.float32)
        s = np.arange(self.slow_half, dtype=np.float64)[:, None]
        kk = np.arange(self.ks_pad, dtype=np.float64)[None, :]
        weight = np.where((kk == 0) | (kk == self.slow_half), 1.0, 2.0) * (kk < self.ks) / N
        ang = 2.0 * np.pi * s * kk / S
        self.a_inv = np.concatenate([weight * np.cos(ang), -weight * np.sin(ang)],
                                    axis=1).astype(np.float32)
        f = np.arange(F, dtype=np.float64)
        ang = 2.0 * np.pi * np.outer(np.arange(self.ks_pad, dtype=np.float64), f) / N
        self.tw_re = np.cos(ang).reshape(-1, 1).astype(np.float32)
        self.tw_im = (-np.sin(ang)).reshape(-1, 1).astype(np.float32)
        ang = 2.0 * np.pi * np.outer(f, f) / F
        self.f_re = np.cos(ang).astype(np.float32)
        self.f_im = (-np.sin(ang)).astype(np.float32)


def _dft_dot(a_ref, x):
    return jnp.dot(a_ref[...], x.astype(BF16), preferred_element_type=F32)


def _stage_a_kernel(a_ref, *refs, batch):
    x_refs, o_ref, in_stage, out_stage = refs[:-3], refs[-3], refs[-2], refs[-1]
    m, k = a_ref.shape
    kp = k // len(x_refs)
    x2ds = [x_ref.reshape(batch * kp * FFT_F_TILE, LANES) for x_ref in x_refs]
    o2d = o_ref.reshape(batch * m * FFT_F_TILE, LANES)
    for b in range(batch):
        for n, x2d in enumerate(x2ds):
            rows = x2d[pl.ds(b * kp * FFT_F_TILE, kp * FFT_F_TILE), :]
            in_stage[pl.ds(n * kp * FFT_F_TILE, kp * FFT_F_TILE), :] = rows.astype(F32)
        slabs = [jnp.concatenate(
            [in_stage[pl.ds(n * kp * FFT_F_TILE + j, kp, stride=FFT_F_TILE), :]
             for n in range(len(x2ds))], axis=0) for j in range(FFT_F_TILE)]
        r = _dft_dot(a_ref, jnp.concatenate(slabs, axis=-1))
        for j in range(FFT_F_TILE):
            out_stage[pl.ds(j, m, stride=FFT_F_TILE), :] = r[:, j * LANES:(j + 1) * LANES]
        o2d[pl.ds(b * m * FFT_F_TILE, m * FFT_F_TILE), :] = out_stage[...].astype(o_ref.dtype)


def _stage_a(a, xs, inverse):
    F, C = xs[0].shape[-2:]
    B = xs[0].shape[0]
    M, K = a.shape
    x_dims, o_dims = ((2, K // 2), (M,)) if inverse else ((K // len(xs),), (2, M // 2))

    def spec(dims):
        zeros = (0,) * len(dims)
        return pl.BlockSpec((B,) + dims + (FFT_F_TILE, LANES), lambda f, c: (0,) + zeros + (f, c))

    return pl.pallas_call(
        functools.partial(_stage_a_kernel, batch=B),
        grid=(F // FFT_F_TILE, C // LANES),
        in_specs=[pl.BlockSpec((M, K), lambda f, c: (0, 0))] + [spec(x_dims)] * len(xs),
        out_specs=spec(o_dims),
        out_shape=jax.ShapeDtypeStruct((B,) + o_dims + (F, C), BF16),
        scratch_shapes=[pltpu.VMEM((K * FFT_F_TILE, LANES), F32),
                        pltpu.VMEM((M * FFT_F_TILE, LANES), F32)],
        compiler_params=_params("parallel", "parallel"),
    )(jnp.asarray(a, BF16), *xs)


def _stage_b_dft(fr_ref, fi_ref, tiles, conj=False):
    w = tiles[0][0].shape[-1]
    both = jnp.concatenate([part for tile in tiles for part in tile], axis=-1)
    pr = _dft_dot(fr_ref, both)
    pi = _dft_dot(fi_ref, both)
    out = []
    for g in range(len(tiles)):
        re, im = slice(2 * g * w, (2 * g + 1) * w), slice((2 * g + 1) * w, (2 * g + 2) * w)
        if conj:
            out.append((pr[:, re] + pi[:, im], pr[:, im] - pi[:, re]))
        else:
            out.append((pr[:, re] - pi[:, im], pr[:, im] + pi[:, re]))
    return out


def _twiddled_tiles(a_ref, twr_ref, twi_ref):
    tiles, twiddles = [], []
    for g in range(a_ref.shape[2] // FFT_FAST):
        rows = slice(g * FFT_FAST, (g + 1) * FFT_FAST)
        tr, ti = twr_ref[rows, :], twi_ref[rows, :]
        for c in range(a_ref.shape[3] // LANES):
            at = rows, slice(c * LANES, (c + 1) * LANES)
            ar, ai = a_ref[(0, 0) + at].astype(F32), a_ref[(0, 1) + at].astype(F32)
            tiles.append((ar * tr - ai * ti, ar * ti + ai * tr))
            twiddles.append((at, tr, ti))
    return tiles, twiddles


def _fft_fwd_kernel(a_ref, twr_ref, twi_ref, fr_ref, fi_ref, o_ref):
    tiles, twiddles = _twiddled_tiles(a_ref, twr_ref, twi_ref)
    for (at, _, _), (xr, xi) in zip(twiddles, _stage_b_dft(fr_ref, fi_ref, tiles)):
        o_ref[(0, 0) + at] = xr.astype(o_ref.dtype)
        o_ref[(0, 1) + at] = xi.astype(o_ref.dtype)


def _fft_conv_kernel(a_ref, k_ref, twr_ref, twi_ref, fr_ref, fi_ref, o_ref):
    tiles, twiddles = _twiddled_tiles(a_ref, twr_ref, twi_ref)
    products = []
    for (at, _, _), (xr, xi) in zip(twiddles, _stage_b_dft(fr_ref, fi_ref, tiles)):
        kr, ki = k_ref[(0, 0) + at].astype(F32), k_ref[(0, 1) + at].astype(F32)
        products.append((xr * kr - xi * ki, xr * ki + xi * kr))
    inverse = _stage_b_dft(fr_ref, fi_ref, products, conj=True)
    for (at, tr, ti), (br, bi) in zip(twiddles, inverse):
        o_ref[(0, 0) + at] = (br * tr + bi * ti).astype(o_ref.dtype)
        o_ref[(0, 1) + at] = (bi * tr - br * ti).astype(o_ref.dtype)


def _fft_stage_b(plan, a, k=None):
    B, _, R, C = a.shape
    rows = plan.ks_group * FFT_FAST
    data = pl.BlockSpec((1, 2, rows, FFT_B_LANES), lambda r, c, b: (b, 0, r, c))
    tw = pl.BlockSpec((rows, LANES), lambda r, c, b: (r, 0))
    mat = pl.BlockSpec((FFT_FAST, FFT_FAST), lambda r, c, b: (0, 0))
    lane_bcast = lambda col: jnp.broadcast_to(jnp.asarray(col), (col.shape[0], LANES))
    consts = (lane_bcast(plan.tw_re), lane_bcast(plan.tw_im),
              jnp.asarray(plan.f_re, BF16), jnp.asarray(plan.f_im, BF16))
    common = dict(
        grid=(R // rows, C // FFT_B_LANES, B),
        out_specs=data,
        out_shape=jax.ShapeDtypeStruct(a.shape, BF16),
        compiler_params=_params("parallel", "parallel", "arbitrary"),
    )
    if k is None:
        return pl.pallas_call(_fft_fwd_kernel, in_specs=[data, tw, tw, mat, mat],
                              **common)(a, *consts)
    kspec = pl.BlockSpec((1, 2, rows, FFT_B_LANES), lambda r, c, b: (0, 0, r, c))
    return pl.pallas_call(_fft_conv_kernel, in_specs=[data, kspec, tw, tw, mat, mat],
                          **common)(a, k, *consts)


def _fft_long_conv(z, kern_halves, plan):
    B, L, C = z.shape
    S, F, P = plan.slow, plan.fast, plan.ks_pad
    ka = _stage_a(plan.a_full, [h.reshape(1, S // 2, F, C) for h in kern_halves], inverse=False)
    k_spec = _fft_stage_b(plan, ka.reshape(1, 2, P * F, C))
    za = _stage_a(plan.a_half, [z.reshape(B, S // 2, F, C)], inverse=False)
    ya = _fft_stage_b(plan, za.reshape(B, 2, P * F, C), k_spec)
    y = _stage_a(plan.a_inv, [ya.reshape(B, 2, P, F, C)], inverse=True)
    return y.reshape(B, L, C)


def _scan_constants():
    C = SCAN_CHUNK
    n_lv = len(SCAN_LEVELS)
    tri = np.tril(np.ones((C, C), np.float32))
    upper = np.zeros((n_lv, C, 1), np.float32)
    mask = np.zeros((n_lv + 1, C, C), np.float32)
    for li, h in enumerate(SCAN_LEVELS):
        for t in range(C):
            r = (t // (2 * h)) * 2 * h + h
            if t % (2 * h) >= h:
                upper[li, t, 0] = 1.0
                mask[li, t, r - h:r] = 1.0
    mask[n_lv] = np.eye(C, dtype=np.float32)
    flip = lambda m: m[:, ::-1, ::-1]
    return (np.stack([tri, tri[::-1, ::-1]]), np.stack([upper, upper[:, ::-1]]),
            np.stack([mask, flip(mask)]))


def _ref_rows(h, reverse):
    rows = []
    for g in range(SCAN_CHUNK // SUBLANES):
        pair = []
        for t in (g * SUBLANES, g * SUBLANES + SUBLANES // 2):
            start = (t // (2 * h)) * 2 * h
            pair.append(start + h if reverse else start + h - 1)
        rows.append(tuple(pair))
    return rows


def _scan_chunk(q_ref, f_ref, v_ref, lb_table, tri, up_ref, mask_ref, o_ref, b_ref, state_ref,
                *, layer, reverse):
    n_lv = len(SCAN_LEVELS)
    C = SCAN_CHUNK

    e = jnp.exp(lb_table - jnp.max(lb_table, axis=0, keepdims=True))
    prob = e / jnp.sum(e, axis=0, keepdims=True)
    lb = jnp.zeros((1, D_HGRN), F32)
    for l in range(1, layer + 1):
        lb = lb + prob[l:l + 1, :]

    one = jnp.ones((), BF16)
    q = q_ref[...]
    half = jnp.asarray(0.5, BF16)
    half_q = half * q
    q = (half_q + half_q * jnp.tanh(half_q)).astype(F32)
    z = f_ref[...]
    log_sig = jnp.minimum(z, 0) - jnp.log(one + jnp.exp(-jnp.abs(z)))
    key = ((half - half * lb.astype(BF16)) * (one - jnp.tanh(half * z))).astype(F32)
    log_a = jnp.log(lb).astype(BF16)
    log_b = jnp.log1p(-lb).astype(BF16) + log_sig
    gate = jnp.maximum(log_a, log_b) + jnp.log(one + jnp.exp(-jnp.abs(log_a - log_b)))
    vb = v_ref[...]

    gate2 = gate * jnp.asarray(math.log2(math.e), BF16)
    b_inc = jnp.dot(tri, gate2, preferred_element_type=F32)
    for hd in range(HGRN_HEADS):
        b_ref[hd] = b_inc[:, hd * HEAD_DIM:(hd + 1) * HEAD_DIM]

    def bcast_row(r):
        return jnp.concatenate([b_ref.at[hd][pl.ds(r, SUBLANES, stride=0), :]
                                for hd in range(HGRN_HEADS)], axis=-1)

    def by_role(h):
        blocks = []
        for start in range(0, C, h):
            upper = (start // h) % 2 == 1
            src = q if upper != reverse else key
            blocks.append(src[start:start + h])
        return jnp.concatenate(blocks, axis=0)

    first_half = lax.broadcasted_iota(jnp.int32, (SUBLANES, 1), 0) < SUBLANES // 2
    scores = [None] * HGRN_HEADS
    for li, h in enumerate(SCAN_LEVELS):
        if h == 1:
            x = jnp.where(up_ref[li] > 0.5, q * (1.0 - key), key)
        else:
            groups = []
            for r0, r1 in _ref_rows(h, reverse):
                ref = bcast_row(r0)
                groups.append(ref if r1 == r0 else jnp.where(first_half, ref, bcast_row(r1)))
            diff = pltpu.bitcast(b_inc - jnp.concatenate(groups, axis=0), jnp.int32)
            neg_dist = pltpu.bitcast(diff | jnp.int32(-2 ** 31), F32)
            roles = by_role(h) if h >= SUBLANES else jnp.where(up_ref[li] > 0.5, q, key)
            x = roles * jnp.exp2(neg_dist)
        xb = x.astype(BF16)
        keep = mask_ref[li] > 0.5
        for hd in range(HGRN_HEADS):
            xh = xb[:, hd * HEAD_DIM:(hd + 1) * HEAD_DIM]
            p = lax.dot_general(xh, xh, (((1,), (1,)), ((), ())), preferred_element_type=F32)
            scores[hd] = jnp.where(keep, p, 0.0 if scores[hd] is None else scores[hd])

    b_total = bcast_row(0 if reverse else C - 1)
    b_rest = jnp.concatenate([b_total] * (C // SUBLANES), axis=0) - b_inc
    q_dec = (q * jnp.exp2(b_inc)).astype(BF16)
    k_dec = (key * jnp.exp2(b_rest)).astype(BF16)
    total = jnp.exp2(b_total[0:1, :])
    qb, kb = q.astype(BF16), key.astype(BF16)
    on_diag = mask_ref[n_lv] > 0.5
    for hd in range(HGRN_HEADS):
        cols = slice(hd * HEAD_DIM, (hd + 1) * HEAD_DIM)
        diag = lax.dot_general(qb[:, cols], kb[:, cols], (((1,), (1,)), ((), ())),
                               preferred_element_type=F32)
        p = jnp.where(on_diag, diag, scores[hd]).astype(BF16)
        st = state_ref[hd]
        o = jnp.dot(p, vb[:, cols], preferred_element_type=F32)
        o = o + lax.dot_general(q_dec[:, cols], st.astype(BF16), (((1,), (1,)), ((), ())),
                                preferred_element_type=F32)
        o_ref[:, cols] = o.astype(o_ref.dtype)
        upd = lax.dot_general(vb[:, cols], k_dec[:, cols], (((0,), (0,)), ((), ())),
                              preferred_element_type=F32)
        state_ref[hd] = st * total[:, cols] + upd


def _hgrn_scan_kernel(qf_ref, ff_ref, vf_ref, qb_ref, fb_ref, vb_ref, lbt_ref, tri_ref, up_ref,
                      mask_ref, of_ref, ob_ref, b_ref, state_ref, *, layer):
    @pl.when(pl.program_id(1) == 0)
    def _():
        state_ref[...] = jnp.zeros_like(state_ref)

    for s in range(SCAN_CHUNKS_PER_STEP):
        rows = pl.ds(s * SCAN_CHUNK, SCAN_CHUNK)
        _scan_chunk(qf_ref.at[rows], ff_ref.at[rows], vf_ref.at[rows], lbt_ref[0], tri_ref[0],
                    up_ref.at[0], mask_ref.at[0], of_ref.at[rows], b_ref.at[0], state_ref.at[0],
                    layer=layer, reverse=False)
        rows = pl.ds((SCAN_CHUNKS_PER_STEP - 1 - s) * SCAN_CHUNK, SCAN_CHUNK)
        _scan_chunk(qb_ref.at[rows], fb_ref.at[rows], vb_ref.at[rows], lbt_ref[1], tri_ref[1],
                    up_ref.at[1], mask_ref.at[1], ob_ref.at[rows], b_ref.at[1], state_ref.at[1],
                    layer=layer, reverse=True)


def _hgrn_scan(proj, lb_table, layer, batch, seq_len):
    T = proj.shape[0]
    C = SCAN_CHUNK
    rows_per_step = SCAN_CHUNKS_PER_STEP * C
    n_chunks = seq_len // rows_per_step
    tri, upper, mask = _scan_constants()
    col0 = (3 * D_HYENA) // D_HGRN

    fwd = lambda b, c: b * n_chunks + c
    bwd = lambda b, c: b * n_chunks + n_chunks - 1 - c
    chunk = lambda rows, col: pl.BlockSpec((rows_per_step, D_HGRN), lambda b, c: (rows(b, c), col))
    whole = lambda a: pl.BlockSpec(a.shape, lambda b, c: (0,) * a.ndim)
    out = jax.ShapeDtypeStruct((T, D_HGRN), BF16)
    return pl.pallas_call(
        functools.partial(_hgrn_scan_kernel, layer=layer),
        grid=(batch, n_chunks),
        in_specs=[
            chunk(fwd, col0), chunk(fwd, col0 + 1), chunk(fwd, col0 + 3),
            chunk(bwd, col0), chunk(bwd, col0 + 2), chunk(bwd, col0 + 3),
            whole(lb_table), whole(tri), whole(upper), whole(mask),
        ],
        out_specs=[chunk(fwd, 0), chunk(bwd, 0)],
        out_shape=[out, out],
        scratch_shapes=[pltpu.VMEM((2, HGRN_HEADS, C, HEAD_DIM), F32),
                        pltpu.VMEM((2, HGRN_HEADS, HEAD_DIM, HEAD_DIM), F32)],
        compiler_params=_params("parallel", "arbitrary"),
    )(proj, proj, proj, proj, proj, proj, lb_table, jnp.asarray(tri, BF16), jnp.asarray(upper),
      jnp.asarray(mask))


def _group_mean_matrix(group):
    idx = np.arange(D_HYENA) // group
    return (idx[:, None] == idx[None, :]).astype(np.float32)


def _mix_out_kernel(x_ref, y_ref, z_ref, x0_ref, of_ref, ob_ref, g_ref, skip_ref,
                    hy_gain_ref, hg_gain_ref, grp_hy_ref, grp_hg_ref, w_ref, post_ref, o_ref):
    z = z_ref[...].astype(F32)
    yh = x0_ref[...].astype(F32) * (y_ref[...].astype(F32) + skip_ref[...] * z)
    ms = _group_sums(yh * yh, grp_hy_ref[...]) * (HYENA_GROUPS / D_HYENA)
    yh = yh * lax.rsqrt(ms + EPS) * hy_gain_ref[...]
    o = of_ref[...].astype(F32) + ob_ref[...].astype(F32)
    ms = _group_sums(o * o, grp_hg_ref[...]) * (1.0 / HEAD_DIM)
    g = g_ref[...].astype(F32)
    half_g = 0.5 * g
    silu_g = half_g + half_g * jnp.tanh(half_g)
    o = o * lax.rsqrt(ms + EPS) * hg_gain_ref[...] * silu_g
    mix = (jnp.dot(yh.astype(BF16), w_ref[:D_HYENA, :], preferred_element_type=F32)
           + jnp.dot(o.astype(BF16), w_ref[D_HYENA:, :], preferred_element_type=F32))
    o_ref[...] = x_ref[...] + mix * _rms_scale(mix) * post_ref[...]


def _mix_out(x2d, y, z, x0, o_fwd, o_bwd, proj, skip, hy_gain, hg_gain, w_bf16, layer,
             post_gain, tm=1024):
    T = x2d.shape[0]
    gate_col = D_IN // D_HGRN - 1
    half = lambda: pl.BlockSpec((tm, D_HYENA), lambda i: (i, 0))
    vec = lambda n: pl.BlockSpec((1, n), lambda i: (0, 0))
    sq = lambda: pl.BlockSpec((D_HYENA, D_HYENA), lambda i: (0, 0))
    row = lambda v: v.reshape(1, -1)
    return pl.pallas_call(
        _mix_out_kernel,
        grid=(T // tm,),
        in_specs=[
            pl.BlockSpec((tm, D_MODEL), lambda i: (i, 0)),
            half(), half(), half(), half(), half(),
            pl.BlockSpec((tm, D_HGRN), lambda i: (i, gate_col)),
            vec(D_HYENA), vec(D_HYENA), vec(D_HGRN), sq(), sq(),
            pl.BlockSpec((None, D_MODEL, D_MODEL), lambda i: (layer, 0, 0)),
            vec(D_MODEL),
        ],
        out_specs=pl.BlockSpec((tm, D_MODEL), lambda i: (i, 0)),
        out_shape=jax.ShapeDtypeStruct((T, D_MODEL), F32),
        compiler_params=_params("parallel"),
    )(x2d, y, z, x0, o_fwd, o_bwd, proj, row(skip), row(hy_gain), row(hg_gain),
      jnp.asarray(_group_mean_matrix(D_HYENA // HYENA_GROUPS), BF16),
      jnp.asarray(_group_mean_matrix(HEAD_DIM), BF16), w_bf16, row(post_gain))


def _gelu_tanh(x):
    c = math.sqrt(2.0 / math.pi)
    half = 0.5 * x
    return half + half * jnp.tanh(x * (c + (0.044715 * c) * (x * x)))


def _ffn_kernel(prev_ref, main_ref, next_ref, pre_ref, wa_ref, wb_ref, cwa_ref, cwb_ref,
                cba_ref, cbb_ref, wd_ref, post_ref, o_ref, h_ref, acc_ref, ua_ref, ub_ref,
                *, tiles_per_seq, tm):
    j = pl.program_id(1)
    pos = pl.program_id(0) % tiles_per_seq

    @pl.when(j == 0)
    def _():
        gain = pre_ref[...]

        def normed(x):
            return (x * _rms_scale(x) * gain).astype(BF16)

        prev = normed(prev_ref[...])
        nxt = normed(next_ref[...])
        h_ref[0:FFN_HALO, :] = jnp.where(pos == 0, jnp.zeros_like(prev), prev)
        h_ref[FFN_HALO:FFN_HALO + tm, :] = normed(main_ref[...])
        h_ref[FFN_HALO + tm:, :] = jnp.where(pos == tiles_per_seq - 1, jnp.zeros_like(nxt), nxt)
        acc_ref[...] = jnp.zeros_like(acc_ref)

    h = h_ref[...]

    def conv_part(w_ref_, cw_ref_, cb_ref_, u_ref):
        u_ref[...] = jnp.dot(h, w_ref_[...], preferred_element_type=F32)
        cw = cw_ref_[...].astype(BF16)
        down = u_ref[pl.ds(FFN_HALO - 1, tm), :].astype(BF16)
        mid = u_ref[pl.ds(FFN_HALO, tm), :].astype(BF16)
        up = u_ref[pl.ds(FFN_HALO + 1, tm), :].astype(BF16)
        return (down * cw[0:1, :] + mid * cw[1:2, :] + up * cw[2:3, :]
                + cb_ref_[...].astype(BF16))

    a = conv_part(wa_ref, cwa_ref, cba_ref, ua_ref)
    b = conv_part(wb_ref, cwb_ref, cbb_ref, ub_ref)
    act = _gelu_tanh(a) * b
    acc_ref[...] += jnp.dot(act, wd_ref[...], preferred_element_type=F32)

    @pl.when(j == pl.num_programs(1) - 1)
    def _():
        ff = acc_ref[...]
        o_ref[...] = main_ref[...] + ff * _rms_scale(ff) * post_ref[...]


def _ffn(x2d, pre_gain, w_up_bf16, conv_w, conv_b, w_down_bf16, layer, post_gain, seq_len,
         tm=1024, tf=1024):
    T = x2d.shape[0]
    conv_b = conv_b.reshape(conv_b.shape[0], 1, -1)
    halo_per_tile = tm // FFN_HALO
    last_halo = T // FFN_HALO - 1
    n_f = D_FF // tf
    row = lambda v: v.reshape(1, -1)
    return pl.pallas_call(
        functools.partial(_ffn_kernel, tiles_per_seq=seq_len // tm, tm=tm),
        grid=(T // tm, n_f),
        in_specs=[
            pl.BlockSpec((FFN_HALO, D_MODEL),
                         lambda i, j: (jnp.maximum(i * halo_per_tile - 1, 0), 0)),
            pl.BlockSpec((tm, D_MODEL), lambda i, j: (i, 0)),
            pl.BlockSpec((FFN_HALO, D_MODEL),
                         lambda i, j: (jnp.minimum((i + 1) * halo_per_tile, last_halo), 0)),
            pl.BlockSpec((1, D_MODEL), lambda i, j: (0, 0)),
            pl.BlockSpec((None, D_MODEL, tf), lambda i, j: (layer, 0, j)),
            pl.BlockSpec((None, D_MODEL, tf), lambda i, j: (layer, 0, n_f + j)),
            pl.BlockSpec((None, 3, tf), lambda i, j: (layer, 0, j)),
            pl.BlockSpec((None, 3, tf), lambda i, j: (layer, 0, n_f + j)),
            pl.BlockSpec((None, 1, tf), lambda i, j: (layer, 0, j)),
            pl.BlockSpec((None, 1, tf), lambda i, j: (layer, 0, n_f + j)),
            pl.BlockSpec((None, tf, D_MODEL), lambda i, j: (layer, j, 0)),
            pl.BlockSpec((1, D_MODEL), lambda i, j: (0, 0)),
        ],
        out_specs=pl.BlockSpec((tm, D_MODEL), lambda i, j: (i, 0)),
        out_shape=jax.ShapeDtypeStruct((T, D_MODEL), F32),
        scratch_shapes=[pltpu.VMEM((tm + 2 * FFN_HALO, D_MODEL), BF16),
                        pltpu.VMEM((tm, D_MODEL), F32)]
        + [pltpu.VMEM((tm + 2 * FFN_HALO, tf), F32)] * 2,
        compiler_params=_params("parallel", "arbitrary"),
    )(x2d, x2d, x2d, row(pre_gain), w_up_bf16, w_up_bf16, conv_w, conv_w,
      conv_b, conv_b, w_down_bf16, row(post_gain))


def _trunk(x, p):
    B, L, _ = x.shape
    x2d = x.reshape(B * L, D_MODEL)
    plan = _FftPlan(L)
    for l in range(p["w_in"].shape[0]):
        proj = _in_proj(x2d, p["norm_mix_pre"][l], p["w_in_bf16"], l)
        x0, z = _hyena_pre(proj, p["hyena_conv_w"][l], p["hyena_conv_b"][l], L)
        kern = _hyena_filter(L, p["filt_w1"][l], p["filt_b1"][l], p["filt_w2"][l], p["filt_b2"][l],
                             p["filt_w3"][l], p["filt_b3"][l], p["filt_w4"][l], p["filt_freq"][l])
        y = _fft_long_conv(z.reshape(B, L, D_HYENA), kern, plan).reshape(B * L, D_HYENA)
        o_fwd, o_bwd = _hgrn_scan(proj, p["hgrn_lower_bounds"], l, B, L)
        x2d = _mix_out(x2d, y, z, x0, o_fwd, o_bwd, proj, p["hyena_skip"][l], p["hyena_out_norm"][l],
                       p["hgrn_out_norm"][l], p["w_out_bf16"], l, p["norm_mix_post"][l])
        x2d = _ffn(x2d, p["norm_ffn_pre"][l], p["ffn_w_up_bf16"], p["ffn_conv_w"],
                   p["ffn_conv_b"], p["ffn_w_down_bf16"], l, p["norm_ffn_post"][l], L)
    return x2d.reshape(B, L, D_MODEL)


def kernel(x_prompt, x_sample, norm_mix_pre, norm_mix_post, norm_ffn_pre, norm_ffn_post, w_in, hyena_conv_w, hyena_conv_b, filt_w1, filt_b1, filt_w2, filt_b2, filt_w3, filt_b3, filt_w4, filt_freq, hyena_skip, hyena_out_norm, hgrn_lower_bounds, hgrn_out_norm, w_out, ffn_w_up, ffn_conv_w, ffn_conv_b, ffn_w_down):
    p = dict(
        norm_mix_pre=norm_mix_pre, norm_mix_post=norm_mix_post, norm_ffn_pre=norm_ffn_pre,
        norm_ffn_post=norm_ffn_post, w_in=w_in, hyena_conv_w=hyena_conv_w,
        hyena_conv_b=hyena_conv_b, filt_w1=filt_w1, filt_b1=filt_b1, filt_w2=filt_w2,
        filt_b2=filt_b2, filt_w3=filt_w3, filt_b3=filt_b3, filt_w4=filt_w4, filt_freq=filt_freq,
        hyena_skip=hyena_skip, hyena_out_norm=hyena_out_norm,
        hgrn_lower_bounds=hgrn_lower_bounds, hgrn_out_norm=hgrn_out_norm,
        ffn_conv_w=ffn_conv_w, ffn_conv_b=ffn_conv_b,
        w_in_bf16=w_in.astype(BF16), w_out_bf16=w_out.astype(BF16),
        ffn_w_up_bf16=ffn_w_up.astype(BF16), ffn_w_down_bf16=ffn_w_down.astype(BF16),
    )
    return (_trunk(x_prompt, p), _trunk(x_sample, p))
```

```python
import functools
import math

import numpy as np
import jax
import jax.numpy as jnp
from jax import lax
from jax.experimental import pallas as pl
from jax.experimental.pallas import tpu as pltpu

F32 = jnp.float32
BF16 = jnp.bfloat16
HIGHEST = lax.Precision.HIGHEST

D_MODEL = 1024
D_HYENA = 512
HYENA_GROUPS = 8
D_HGRN = 512
HGRN_HEADS = 4
HEAD_DIM = 128
D_IN = 3 * D_HYENA + 5 * D_HGRN
D_FF = 4 * D_MODEL
FILTER_BANDS = 16
FILTER_HIDDEN = 64
DECAY_TARGET = 1e-2
FAST_DECAY_PCT = 0.3
SLOW_DECAY_PCT = 1.5
EPS = 1e-6

SUBLANES = 8
LANES = 128
BF16_ROWS = 16
VMEM_LIMIT_BYTES = 56 * 1024 * 1024

FFT_FAST = 128
FFT_KS_GROUP_MAX = 20
FFT_F_TILE = 16
FFT_B_LANES = 256
FFN_HALO = BF16_ROWS
SCAN_CHUNK = 128
SCAN_CHUNKS_PER_STEP = 4
SCAN_LEVELS = (64, 32, 16, 8, 4, 2, 1)


def _params(*semantics):
    return pltpu.CompilerParams(dimension_semantics=semantics,
                                vmem_limit_bytes=VMEM_LIMIT_BYTES)


def _rms_scale(x):
    return lax.rsqrt(jnp.mean(x * x, axis=-1, keepdims=True) + EPS)


def _group_sums(sq, same_group):
    return jnp.dot(sq.astype(BF16), same_group, preferred_element_type=F32)


def _in_proj_kernel(x_ref, g_ref, w_ref, o_ref, h_ref):
    @pl.when(pl.program_id(1) == 0)
    def _():
        x = x_ref[...]
        h_ref[...] = (x * _rms_scale(x) * g_ref[...]).astype(BF16)

    o_ref[...] = jnp.dot(h_ref[...], w_ref[...], preferred_element_type=F32).astype(o_ref.dtype)


def _in_proj(x2d, gain, w_bf16, layer, tm=1024, tn=4096):
    T = x2d.shape[0]
    N = w_bf16.shape[2]
    return pl.pallas_call(
        _in_proj_kernel,
        grid=(T // tm, N // tn),
        in_specs=[
            pl.BlockSpec((tm, D_MODEL), lambda i, j: (i, 0)),
            pl.BlockSpec((1, D_MODEL), lambda i, j: (0, 0)),
            pl.BlockSpec((None, D_MODEL, tn), lambda i, j: (layer, 0, j)),
        ],
        out_specs=pl.BlockSpec((tm, tn), lambda i, j: (i, j)),
        out_shape=jax.ShapeDtypeStruct((T, N), BF16),
        scratch_shapes=[pltpu.VMEM((tm, D_MODEL), BF16)],
        compiler_params=_params("parallel", "arbitrary"),
    )(x2d, gain.reshape(1, D_MODEL), w_bf16)


def _shift_rows(main, prev_row, next_row):
    tm = main.shape[0]
    rows = lax.broadcasted_iota(jnp.int32, (tm, 1), 0)
    down = jnp.where(rows == 0, prev_row, pltpu.roll(main, 1, 0))
    up = jnp.where(rows == tm - 1, next_row, pltpu.roll(main, tm - 1, 0))
    return down, up


def _hyena_pre_kernel(prev_ref, main_ref, next_ref, w_ref, b_ref, x0_ref, z_ref,
                      *, tiles_per_seq):
    pos = pl.program_id(0) % tiles_per_seq
    main = main_ref[...].astype(F32)
    prev_row = jnp.where(pos == 0, 0.0, prev_ref[...].astype(F32)[BF16_ROWS - 1:BF16_ROWS, :])
    next_row = jnp.where(pos == tiles_per_seq - 1, 0.0, next_ref[...].astype(F32)[0:1, :])
    down, up = _shift_rows(main, prev_row, next_row)
    w = w_ref[...]
    u = down * w[0:1, :] + main * w[1:2, :] + up * w[2:3, :] + b_ref[...]
    x0_ref[...] = u[:, :D_HYENA].astype(x0_ref.dtype)
    z_ref[...] = (u[:, D_HYENA:2 * D_HYENA] * u[:, 2 * D_HYENA:]).astype(z_ref.dtype)


def _hyena_pre(proj, conv_w, conv_b, seq_len, tm=1024):
    T = proj.shape[0]
    width = 3 * D_HYENA
    halo_per_tile = tm // BF16_ROWS
    last_halo = T // BF16_ROWS - 1
    return pl.pallas_call(
        functools.partial(_hyena_pre_kernel, tiles_per_seq=seq_len // tm),
        grid=(T // tm,),
        in_specs=[
            pl.BlockSpec((BF16_ROWS, width),
                         lambda i: (jnp.maximum(i * halo_per_tile - 1, 0), 0)),
            pl.BlockSpec((tm, width), lambda i: (i, 0)),
            pl.BlockSpec((BF16_ROWS, width),
                         lambda i: (jnp.minimum((i + 1) * halo_per_tile, last_halo), 0)),
            pl.BlockSpec((3, width), lambda i: (0, 0)),
            pl.BlockSpec((1, width), lambda i: (0, 0)),
        ],
        out_specs=[pl.BlockSpec((tm, D_HYENA), lambda i: (i, 0)),
                   pl.BlockSpec((tm, D_HYENA), lambda i: (i, 0))],
        out_shape=[jax.ShapeDtypeStruct((T, D_HYENA), BF16),
                   jax.ShapeDtypeStruct((T, D_HYENA), BF16)],
        compiler_params=_params("parallel"),
    )(proj, proj, proj, conv_w, conv_b.reshape(1, width))


def _filter_kernel(band_ref, w1t_ref, w1c_ref, w1s_ref, b1_ref, w2_ref, b2_ref, w3_ref, b3_ref,
                   w4_ref, freq_ref, delta_ref, rev_ref, of_ref, ob_ref, *, seq_len, tr):
    i = pl.program_id(0)
    ext = tr + LANES

    def tap_index(shape, axis):
        return (i * tr + lax.broadcasted_iota(jnp.int32, shape, axis)).astype(F32)

    idx = tap_index((1, ext), 1)
    t = idx * (1.0 / (seq_len - 1))
    arg = band_ref[...] * ((2.0 * math.pi / seq_len) * idx)
    fr = freq_ref[...]

    def dense(w_ref_, a):
        return jnp.dot(w_ref_[...], a, precision=HIGHEST, preferred_element_type=F32)

    h = w1t_ref[...] * t + dense(w1c_ref, jnp.cos(arg)) - dense(w1s_ref, jnp.sin(arg))
    h = jnp.sin(fr * (h + b1_ref[...]))
    h = jnp.sin(fr * (dense(w2_ref, h) + b2_ref[...]))
    h = jnp.sin(fr * (dense(w3_ref, h) + b3_ref[...])).astype(BF16)
    taps = lax.dot_general(h, w4_ref[...].astype(BF16), (((0,), (0,)), ((), ())),
                           preferred_element_type=F32)
    window = jnp.exp(-(tap_index((ext, 1), 0) * (1.0 / (seq_len - 1))) * delta_ref[...])
    of_ref[...] = (taps[:tr, :D_HYENA] * window[:tr]).astype(of_ref.dtype)
    bwd = (taps[:, D_HYENA:] * window).astype(BF16)
    rows = lax.broadcasted_iota(jnp.int32, (tr, 1), 0)
    is_tap_l = (rows == 0) & (i == pl.num_programs(0) - 1)
    reversed_taps = jnp.dot(rev_ref[...], bwd, preferred_element_type=F32)
    ob_ref[...] = jnp.where(is_tap_l, 0.0, reversed_taps).astype(ob_ref.dtype)


def _hyena_filter(seq_len, w1, b1, w2, b2, w3, b3, w4, freq, tr=1024):
    n_tiles = seq_len // tr
    bands = np.linspace(1e-4, FILTER_BANDS - 1, FILTER_BANDS, dtype=np.float32).reshape(-1, 1)
    deltas = np.abs(np.linspace(math.log(DECAY_TARGET) / SLOW_DECAY_PCT,
                                math.log(DECAY_TARGET) / FAST_DECAY_PCT, D_HYENA,
                                dtype=np.float32)).reshape(1, D_HYENA)
    reverse = (np.arange(tr)[:, None] + np.arange(tr + LANES)[None, :] == tr).astype(np.float32)
    const = lambda i: (0, 0)
    col = lambda v: v.reshape(-1, 1)
    full = lambda a: pl.BlockSpec(a.shape, const)
    operands = [jnp.asarray(bands), col(w1[0]), w1[1:1 + FILTER_BANDS].T,
                w1[1 + FILTER_BANDS:].T, col(b1), w2.T, col(b2), w3.T, col(b3),
                w4, col(freq), jnp.asarray(deltas), jnp.asarray(reverse, BF16)]
    out = jax.ShapeDtypeStruct((seq_len, D_HYENA), BF16)
    return pl.pallas_call(
        functools.partial(_filter_kernel, seq_len=seq_len, tr=tr),
        grid=(n_tiles,),
        in_specs=[full(a) for a in operands],
        out_specs=[pl.BlockSpec((tr, D_HYENA), lambda i: (i, 0)),
                   pl.BlockSpec((tr, D_HYENA), lambda i: (n_tiles - 1 - i, 0))],
        out_shape=[out, out],
        compiler_params=_params("parallel"),
    )(*operands)


class _FftPlan:
    def __init__(self, seq_len):
        self.n = 2 * seq_len
        self.fast = FFT_FAST
        self.slow = self.n // FFT_FAST
        self.slow_half = self.slow // 2
        self.ks = self.slow_half + 1
        self.ks_pad = -(-self.ks // SUBLANES) * SUBLANES
        self.ks_group = max(g for g in range(1, FFT_KS_GROUP_MAX + 1) if self.ks_pad % g == 0)
        S, F, N = self.slow, self.fast, self.n
        ks = np.arange(self.ks_pad, dtype=np.float64)[:, None]
        valid = (ks < self.ks)

        def stage_a(n_s):
            s = np.arange(n_s, dtype=np.float64)[None, :]
            ang = 2.0 * np.pi * ks * s / S
            return np.concatenate([np.where(valid, np.cos(ang), 0.0),
                                   np.where(valid, -np.sin(ang), 0.0)], axis=0)

        self.a_half = stage_a(self.slow_half).astype(np.float32)
        self.a_full = stage_a(self.slow).astype(np.float32)
        s = np.arange(self.slow_half, dtype=np.float64)[:, None]
        kk = np.arange(self.ks_pad, dtype=np.float64)[None, :]
        weight = np.where((kk == 0) | (kk == self.slow_half), 1.0, 2.0) * (kk < self.ks) / N
        ang = 2.0 * np.pi * s * kk / S
        self.a_inv = np.concatenate([weight * np.cos(ang), -weight * np.sin(ang)],
                                    axis=1).astype(np.float32)
        f = np.arange(F, dtype=np.float64)
        ang = 2.0 * np.pi * np.outer(np.arange(self.ks_pad, dtype=np.float64), f) / N
        self.tw_re = np.cos(ang).reshape(-1, 1).astype(np.float32)
        self.tw_im = (-np.sin(ang)).reshape(-1, 1).astype(np.float32)
        ang = 2.0 * np.pi * np.outer(f, f) / F
        self.f_re = np.cos(ang).astype(np.float32)
        self.f_im = (-np.sin(ang)).astype(np.float32)


def _dft_dot(a_ref, x):
    return jnp.dot(a_ref[...], x.astype(BF16), preferred_element_type=F32)


def _stage_a_kernel(a_ref, *refs, batch):
    x_refs, o_ref, in_stage, out_stage = refs[:-3], refs[-3], refs[-2], refs[-1]
    m, k = a_ref.shape
    kp = k // len(x_refs)
    x2ds = [x_ref.reshape(batch * kp * FFT_F_TILE, LANES) for x_ref in x_refs]
    o2d = o_ref.reshape(batch * m * FFT_F_TILE, LANES)
    for b in range(batch):
        for n, x2d in enumerate(x2ds):
            rows = x2d[pl.ds(b * kp * FFT_F_TILE, kp * FFT_F_TILE), :]
            in_stage[pl.ds(n * kp * FFT_F_TILE, kp * FFT_F_TILE), :] = rows.astype(F32)
        slabs = [jnp.concatenate(
            [in_stage[pl.ds(n * kp * FFT_F_TILE + j, kp, stride=FFT_F_TILE), :]
             for n in range(len(x2ds))], axis=0) for j in range(FFT_F_TILE)]
        r = _dft_dot(a_ref, jnp.concatenate(slabs, axis=-1))
        for j in range(FFT_F_TILE):
            out_stage[pl.ds(j, m, stride=FFT_F_TILE), :] = r[:, j * LANES:(j + 1) * LANES]
        o2d[pl.ds(b * m * FFT_F_TILE, m * FFT_F_TILE), :] = out_stage[...].astype(o_ref.dtype)


def _stage_a(a, xs, inverse):
    F, C = xs[0].shape[-2:]
    B = xs[0].shape[0]
    M, K = a.shape
    x_dims, o_dims = ((2, K // 2), (M,)) if inverse else ((K // len(xs),), (2, M // 2))

    def spec(dims):
        zeros = (0,) * len(dims)
        return pl.BlockSpec((B,) + dims + (FFT_F_TILE, LANES), lambda f, c: (0,) + zeros + (f, c))

    return pl.pallas_call(
        functools.partial(_stage_a_kernel, batch=B),
        grid=(F // FFT_F_TILE, C // LANES),
        in_specs=[pl.BlockSpec((M, K), lambda f, c: (0, 0))] + [spec(x_dims)] * len(xs),
        out_specs=spec(o_dims),
        out_shape=jax.ShapeDtypeStruct((B,) + o_dims + (F, C), BF16),
        scratch_shapes=[pltpu.VMEM((K * FFT_F_TILE, LANES), F32),
                        pltpu.VMEM((M * FFT_F_TILE, LANES), F32)],
        compiler_params=_params("parallel", "parallel"),
    )(jnp.asarray(a, BF16), *xs)


def _stage_b_dft(fr_ref, fi_ref, tiles, conj=False):
    w = tiles[0][0].shape[-1]
    both = jnp.concatenate([part for tile in tiles for part in tile], axis=-1)
    pr = _dft_dot(fr_ref, both)
    pi = _dft_dot(fi_ref, both)
    out = []
    for g in range(len(tiles)):
        re, im = slice(2 * g * w, (2 * g + 1) * w), slice((2 * g + 1) * w, (2 * g + 2) * w)
        if conj:
            out.append((pr[:, re] + pi[:, im], pr[:, im] - pi[:, re]))
        else:
            out.append((pr[:, re] - pi[:, im], pr[:, im] + pi[:, re]))
    return out


def _twiddled_tiles(a_ref, twr_ref, twi_ref):
    tiles, twiddles = [], []
    for g in range(a_ref.shape[2] // FFT_FAST):
        rows = slice(g * FFT_FAST, (g + 1) * FFT_FAST)
        tr, ti = twr_ref[rows, :], twi_ref[rows, :]
        for c in range(a_ref.shape[3] // LANES):
            at = rows, slice(c * LANES, (c + 1) * LANES)
            ar, ai = a_ref[(0, 0) + at].astype(F32), a_ref[(0, 1) + at].astype(F32)
            tiles.append((ar * tr - ai * ti, ar * ti + ai * tr))
            twiddles.append((at, tr, ti))
    return tiles, twiddles


def _fft_fwd_kernel(a_ref, twr_ref, twi_ref, fr_ref, fi_ref, o_ref):
    tiles, twiddles = _twiddled_tiles(a_ref, twr_ref, twi_ref)
    for (at, _, _), (xr, xi) in zip(twiddles, _stage_b_dft(fr_ref, fi_ref, tiles)):
        o_ref[(0, 0) + at] = xr.astype(o_ref.dtype)
        o_ref[(0, 1) + at] = xi.astype(o_ref.dtype)


def _fft_conv_kernel(a_ref, k_ref, twr_ref, twi_ref, fr_ref, fi_ref, o_ref):
    tiles, twiddles = _twiddled_tiles(a_ref, twr_ref, twi_ref)
    products = []
    for (at, _, _), (xr, xi) in zip(twiddles, _stage_b_dft(fr_ref, fi_ref, tiles)):
        kr, ki = k_ref[(0, 0) + at].astype(F32), k_ref[(0, 1) + at].astype(F32)
        products.append((xr * kr - xi * ki, xr * ki + xi * kr))
    inverse = _stage_b_dft(fr_ref, fi_ref, products, conj=True)
    for (at, tr, ti), (br, bi) in zip(twiddles, inverse):
        o_ref[(0, 0) + at] = (br * tr + bi * ti).astype(o_ref.dtype)
        o_ref[(0, 1) + at] = (bi * tr - br * ti).astype(o_ref.dtype)


def _fft_stage_b(plan, a, k=None):
    B, _, R, C = a.shape
    rows = plan.ks_group * FFT_FAST
    data = pl.BlockSpec((1, 2, rows, FFT_B_LANES), lambda r, c, b: (b, 0, r, c))
    tw = pl.BlockSpec((rows, LANES), lambda r, c, b: (r, 0))
    mat = pl.BlockSpec((FFT_FAST, FFT_FAST), lambda r, c, b: (0, 0))
    lane_bcast = lambda col: jnp.broadcast_to(jnp.asarray(col), (col.shape[0], LANES))
    consts = (lane_bcast(plan.tw_re), lane_bcast(plan.tw_im),
              jnp.asarray(plan.f_re, BF16), jnp.asarray(plan.f_im, BF16))
    common = dict(
        grid=(R // rows, C // FFT_B_LANES, B),
        out_specs=data,
        out_shape=jax.ShapeDtypeStruct(a.shape, BF16),
        compiler_params=_params("parallel", "parallel", "arbitrary"),
    )
    if k is None:
        return pl.pallas_call(_fft_fwd_kernel, in_specs=[data, tw, tw, mat, mat],
                              **common)(a, *consts)
    kspec = pl.BlockSpec((1, 2, rows, FFT_B_LANES), lambda r, c, b: (0, 0, r, c))
    return pl.pallas_call(_fft_conv_kernel, in_specs=[data, kspec, tw, tw, mat, mat],
                          **common)(a, k, *consts)


def _fft_long_conv(z, kern_halves, plan):
    B, L, C = z.shape
    S, F, P = plan.slow, plan.fast, plan.ks_pad
    ka = _stage_a(plan.a_full, [h.reshape(1, S // 2, F, C) for h in kern_halves], inverse=False)
    k_spec = _fft_stage_b(plan, ka.reshape(1, 2, P * F, C))
    za = _stage_a(plan.a_half, [z.reshape(B, S // 2, F, C)], inverse=False)
    ya = _fft_stage_b(plan, za.reshape(B, 2, P * F, C), k_spec)
    y = _stage_a(plan.a_inv, [ya.reshape(B, 2, P, F, C)], inverse=True)
    return y.reshape(B, L, C)


def _scan_constants():
    C = SCAN_CHUNK
    n_lv = len(SCAN_LEVELS)
    tri = np.tril(np.ones((C, C), np.float32))
    upper = np.zeros((n_lv, C, 1), np.float32)
    mask = np.zeros((n_lv + 1, C, C), np.float32)
    for li, h in enumerate(SCAN_LEVELS):
        for t in range(C):
            r = (t // (2 * h)) * 2 * h + h
            if t % (2 * h) >= h:
                upper[li, t, 0] = 1.0
                mask[li, t, r - h:r] = 1.0
    mask[n_lv] = np.eye(C, dtype=np.float32)
    flip = lambda m: m[:, ::-1, ::-1]
    return (np.stack([tri, tri[::-1, ::-1]]), np.stack([upper, upper[:, ::-1]]),
            np.stack([mask, flip(mask)]))


def _ref_rows(h, reverse):
    rows = []
    for g in range(SCAN_CHUNK // SUBLANES):
        pair = []
        for t in (g * SUBLANES, g * SUBLANES + SUBLANES // 2):
            start = (t // (2 * h)) * 2 * h
            pair.append(start + h if reverse else start + h - 1)
        rows.append(tuple(pair))
    return rows


def _scan_chunk(q_ref, f_ref, v_ref, lb_table, tri, up_ref, mask_ref, o_ref, b_ref, state_ref,
                *, layer, reverse):
    n_lv = len(SCAN_LEVELS)
    C = SCAN_CHUNK

    e = jnp.exp(lb_table - jnp.max(lb_table, axis=0, keepdims=True))
    prob = e / jnp.sum(e, axis=0, keepdims=True)
    lb = jnp.zeros((1, D_HGRN), F32)
    for l in range(1, layer + 1):
        lb = lb + prob[l:l + 1, :]

    one = jnp.ones((), BF16)
    q = q_ref[...]
    half = jnp.asarray(0.5, BF16)
    half_q = half * q
    q = (half_q + half_q * jnp.tanh(half_q)).astype(F32)
    z = f_ref[...]
    log_sig = jnp.minimum(z, 0) - jnp.log(one + jnp.exp(-jnp.abs(z)))
    key = ((half - half * lb.astype(BF16)) * (one - jnp.tanh(half * z))).astype(F32)
    log_a = jnp.log(lb).astype(BF16)
    log_b = jnp.log1p(-lb).astype(BF16) + log_sig
    gate = jnp.maximum(log_a, log_b) + jnp.log(one + jnp.exp(-jnp.abs(log_a - log_b)))
    vb = v_ref[...]

    gate2 = gate * jnp.asarray(math.log2(math.e), BF16)
    b_inc = jnp.dot(tri, gate2, preferred_element_type=F32)
    for hd in range(HGRN_HEADS):
        b_ref[hd] = b_inc[:, hd * HEAD_DIM:(hd + 1) * HEAD_DIM]

    def bcast_row(r):
        return jnp.concatenate([b_ref.at[hd][pl.ds(r, SUBLANES, stride=0), :]
                                for hd in range(HGRN_HEADS)], axis=-1)

    def by_role(h):
        blocks = []
        for start in range(0, C, h):
            upper = (start // h) % 2 == 1
            src = q if upper != reverse else key
            blocks.append(src[start:start + h])
        return jnp.concatenate(blocks, axis=0)

    first_half = lax.broadcasted_iota(jnp.int32, (SUBLANES, 1), 0) < SUBLANES // 2
    scores = [None] * HGRN_HEADS
    for li, h in enumerate(SCAN_LEVELS):
        if h == 1:
            x = jnp.where(up_ref[li] > 0.5, q * (1.0 - key), key)
        else:
            groups = []
            for r0, r1 in _ref_rows(h, reverse):
                ref = bcast_row(r0)
                groups.append(ref if r1 == r0 else jnp.where(first_half, ref, bcast_row(r1)))
            diff = pltpu.bitcast(b_inc - jnp.concatenate(groups, axis=0), jnp.int32)
            neg_dist = pltpu.bitcast(diff | jnp.int32(-2 ** 31), F32)
            roles = by_role(h) if h >= SUBLANES else jnp.where(up_ref[li] > 0.5, q, key)
            x = roles * jnp.exp2(neg_dist)
        xb = x.astype(BF16)
        keep = mask_ref[li] > 0.5
        for hd in range(HGRN_HEADS):
            xh = xb[:, hd * HEAD_DIM:(hd + 1) * HEAD_DIM]
            p = lax.dot_general(xh, xh, (((1,), (1,)), ((), ())), preferred_element_type=F32)
            scores[hd] = jnp.where(keep, p, 0.0 if scores[hd] is None else scores[hd])

    b_total = bcast_row(0 if reverse else C - 1)
    b_rest = jnp.concatenate([b_total] * (C // SUBLANES), axis=0) - b_inc
    q_dec = (q * jnp.exp2(b_inc)).astype(BF16)
    k_dec = (key * jnp.exp2(b_rest)).astype(BF16)
    total = jnp.exp2(b_total[0:1, :])
    qb, kb = q.astype(BF16), key.astype(BF16)
    on_diag = mask_ref[n_lv] > 0.5
    for hd in range(HGRN_HEADS):
        cols = slice(hd * HEAD_DIM, (hd + 1) * HEAD_DIM)
        diag = lax.dot_general(qb[:, cols], kb[:, cols], (((1,), (1,)), ((), ())),
                               preferred_element_type=F32)
        p = jnp.where(on_diag, diag, scores[hd]).astype(BF16)
        st = state_ref[hd]
        o = jnp.dot(p, vb[:, cols], preferred_element_type=F32)
        o = o + lax.dot_general(q_dec[:, cols], st.astype(BF16), (((1,), (1,)), ((), ())),
                                preferred_element_type=F32)
        o_ref[:, cols] = o.astype(o_ref.dtype)
        upd = lax.dot_general(vb[:, cols], k_dec[:, cols], (((0,), (0,)), ((), ())),
                              preferred_element_type=F32)
        state_ref[hd] = st * total[:, cols] + upd


def _hgrn_scan_kernel(qf_ref, ff_ref, vf_ref, qb_ref, fb_ref, vb_ref, lbt_ref, tri_ref, up_ref,
                      mask_ref, of_ref, ob_ref, b_ref, state_ref, *, layer):
    @pl.when(pl.program_id(1) == 0)
    def _():
        state_ref[...] = jnp.zeros_like(state_ref)

    for s in range(SCAN_CHUNKS_PER_STEP):
        rows = pl.ds(s * SCAN_CHUNK, SCAN_CHUNK)
        _scan_chunk(qf_ref.at[rows], ff_ref.at[rows], vf_ref.at[rows], lbt_ref[0], tri_ref[0],
                    up_ref.at[0], mask_ref.at[0], of_ref.at[rows], b_ref.at[0], state_ref.at[0],
                    layer=layer, reverse=False)
        rows = pl.ds((SCAN_CHUNKS_PER_STEP - 1 - s) * SCAN_CHUNK, SCAN_CHUNK)
        _scan_chunk(qb_ref.at[rows], fb_ref.at[rows], vb_ref.at[rows], lbt_ref[1], tri_ref[1],
                    up_ref.at[1], mask_ref.at[1], ob_ref.at[rows], b_ref.at[1], state_ref.at[1],
                    layer=layer, reverse=True)


def _hgrn_scan(proj, lb_table, layer, batch, seq_len):
    T = proj.shape[0]
    C = SCAN_CHUNK
    rows_per_step = SCAN_CHUNKS_PER_STEP * C
    n_chunks = seq_len // rows_per_step
    tri, upper, mask = _scan_constants()
    col0 = (3 * D_HYENA) // D_HGRN

    fwd = lambda b, c: b * n_chunks + c
    bwd = lambda b, c: b * n_chunks + n_chunks - 1 - c
    chunk = lambda rows, col: pl.BlockSpec((rows_per_step, D_HGRN), lambda b, c: (rows(b, c), col))
    whole = lambda a: pl.BlockSpec(a.shape, lambda b, c: (0,) * a.ndim)
    out = jax.ShapeDtypeStruct((T, D_HGRN), BF16)
    return pl.pallas_call(
        functools.partial(_hgrn_scan_kernel, layer=layer),
        grid=(batch, n_chunks),
        in_specs=[
            chunk(fwd, col0), chunk(fwd, col0 + 1), chunk(fwd, col0 + 3),
            chunk(bwd, col0), chunk(bwd, col0 + 2), chunk(bwd, col0 + 3),
            whole(lb_table), whole(tri), whole(upper), whole(mask),
        ],
        out_specs=[chunk(fwd, 0), chunk(bwd, 0)],
        out_shape=[out, out],
        scratch_shapes=[pltpu.VMEM((2, HGRN_HEADS, C, HEAD_DIM), F32),
                        pltpu.VMEM((2, HGRN_HEADS, HEAD_DIM, HEAD_DIM), F32)],
        compiler_params=_params("parallel", "arbitrary"),
    )(proj, proj, proj, proj, proj, proj, lb_table, jnp.asarray(tri, BF16), jnp.asarray(upper),
      jnp.asarray(mask))


def _group_mean_matrix(group):
    idx = np.arange(D_HYENA) // group
    return (idx[:, None] == idx[None, :]).astype(np.float32)


def _mix_out_kernel(x_ref, y_ref, z_ref, x0_ref, of_ref, ob_ref, g_ref, skip_ref,
                    hy_gain_ref, hg_gain_ref, grp_hy_ref, grp_hg_ref, w_ref, post_ref, o_ref):
    z = z_ref[...].astype(F32)
    yh = x0_ref[...].astype(F32) * (y_ref[...].astype(F32) + skip_ref[...] * z)
    ms = _group_sums(yh * yh, grp_hy_ref[...]) * (HYENA_GROUPS / D_HYENA)
    yh = yh * lax.rsqrt(ms + EPS) * hy_gain_ref[...]
    o = of_ref[...].astype(F32) + ob_ref[...].astype(F32)
    ms = _group_sums(o * o, grp_hg_ref[...]) * (1.0 / HEAD_DIM)
    g = g_ref[...].astype(F32)
    half_g = 0.5 * g
    silu_g = half_g + half_g * jnp.tanh(half_g)
    o = o * lax.rsqrt(ms + EPS) * hg_gain_ref[...] * silu_g
    mix = (jnp.dot(yh.astype(BF16), w_ref[:D_HYENA, :], preferred_element_type=F32)
           + jnp.dot(o.astype(BF16), w_ref[D_HYENA:, :], preferred_element_type=F32))
    o_ref[...] = x_ref[...] + mix * _rms_scale(mix) * post_ref[...]


def _mix_out(x2d, y, z, x0, o_fwd, o_bwd, proj, skip, hy_gain, hg_gain, w_bf16, layer,
             post_gain, tm=1024):
    T = x2d.shape[0]
    gate_col = D_IN // D_HGRN - 1
    half = lambda: pl.BlockSpec((tm, D_HYENA), lambda i: (i, 0))
    vec = lambda n: pl.BlockSpec((1, n), lambda i: (0, 0))
    sq = lambda: pl.BlockSpec((D_HYENA, D_HYENA), lambda i: (0, 0))
    row = lambda v: v.reshape(1, -1)
    return pl.pallas_call(
        _mix_out_kernel,
        grid=(T // tm,),
        in_specs=[
            pl.BlockSpec((tm, D_MODEL), lambda i: (i, 0)),
            half(), half(), half(), half(), half(),
            pl.BlockSpec((tm, D_HGRN), lambda i: (i, gate_col)),
            vec(D_HYENA), vec(D_HYENA), vec(D_HGRN), sq(), sq(),
            pl.BlockSpec((None, D_MODEL, D_MODEL), lambda i: (layer, 0, 0)),
            vec(D_MODEL),
        ],
        out_specs=pl.BlockSpec((tm, D_MODEL), lambda i: (i, 0)),
        out_shape=jax.ShapeDtypeStruct((T, D_MODEL), F32),
        compiler_params=_params("parallel"),
    )(x2d, y, z, x0, o_fwd, o_bwd, proj, row(skip), row(hy_gain), row(hg_gain),
      jnp.asarray(_group_mean_matrix(D_HYENA // HYENA_GROUPS), BF16),
      jnp.asarray(_group_mean_matrix(HEAD_DIM), BF16), w_bf16, row(post_gain))


def _gelu_tanh(x):
    c = math.sqrt(2.0 / math.pi)
    half = 0.5 * x
    return half + half * jnp.tanh(x * (c + (0.044715 * c) * (x * x)))


def _ffn_kernel(prev_ref, main_ref, next_ref, pre_ref, wa_ref, wb_ref, cwa_ref, cwb_ref,
                cba_ref, cbb_ref, wd_ref, post_ref, o_ref, h_ref, acc_ref, ua_ref, ub_ref,
                *, tiles_per_seq, tm):
    j = pl.program_id(1)
    pos = pl.program_id(0) % tiles_per_seq

    @pl.when(j == 0)
    def _():
        gain = pre_ref[...]

        def normed(x):
            return (x * _rms_scale(x) * gain).astype(BF16)

        prev = normed(prev_ref[...])
        nxt = normed(next_ref[...])
        h_ref[0:FFN_HALO, :] = jnp.where(pos == 0, jnp.zeros_like(prev), prev)
        h_ref[FFN_HALO:FFN_HALO + tm, :] = normed(main_ref[...])
        h_ref[FFN_HALO + tm:, :] = jnp.where(pos == tiles_per_seq - 1, jnp.zeros_like(nxt), nxt)
        acc_ref[...] = jnp.zeros_like(acc_ref)

    h = h_ref[...]

    def conv_part(w_ref_, cw_ref_, cb_ref_, u_ref):
        u_ref[...] = jnp.dot(h, w_ref_[...], preferred_element_type=F32)
        cw = cw_ref_[...].astype(BF16)
        down = u_ref[pl.ds(FFN_HALO - 1, tm), :].astype(BF16)
        mid = u_ref[pl.ds(FFN_HALO, tm), :].astype(BF16)
        up = u_ref[pl.ds(FFN_HALO + 1, tm), :].astype(BF16)
        return (down * cw[0:1, :] + mid * cw[1:2, :] + up * cw[2:3, :]
                + cb_ref_[...].astype(BF16))

    a = conv_part(wa_ref, cwa_ref, cba_ref, ua_ref)
    b = conv_part(wb_ref, cwb_ref, cbb_ref, ub_ref)
    act = _gelu_tanh(a) * b
    acc_ref[...] += jnp.dot(act, wd_ref[...], preferred_element_type=F32)

    @pl.when(j == pl.num_programs(1) - 1)
    def _():
        ff = acc_ref[...]
        o_ref[...] = main_ref[...] + ff * _rms_scale(ff) * post_ref[...]


def _ffn(x2d, pre_gain, w_up_bf16, conv_w, conv_b, w_down_bf16, layer, post_gain, seq_len,
         tm=1024, tf=1024):
    T = x2d.shape[0]
    conv_b = conv_b.reshape(conv_b.shape[0], 1, -1)
    halo_per_tile = tm // FFN_HALO
    last_halo = T // FFN_HALO - 1
    n_f = D_FF // tf
    row = lambda v: v.reshape(1, -1)
    return pl.pallas_call(
        functools.partial(_ffn_kernel, tiles_per_seq=seq_len // tm, tm=tm),
        grid=(T // tm, n_f),
        in_specs=[
            pl.BlockSpec((FFN_HALO, D_MODEL),
                         lambda i, j: (jnp.maximum(i * halo_per_tile - 1, 0), 0)),
            pl.BlockSpec((tm, D_MODEL), lambda i, j: (i, 0)),
            pl.BlockSpec((FFN_HALO, D_MODEL),
                         lambda i, j: (jnp.minimum((i + 1) * halo_per_tile, last_halo), 0)),
            pl.BlockSpec((1, D_MODEL), lambda i, j: (0, 0)),
            pl.BlockSpec((None, D_MODEL, tf), lambda i, j: (layer, 0, j)),
            pl.BlockSpec((None, D_MODEL, tf), lambda i, j: (layer, 0, n_f + j)),
            pl.BlockSpec((None, 3, tf), lambda i, j: (layer, 0, j)),
            pl.BlockSpec((None, 3, tf), lambda i, j: (layer, 0, n_f + j)),
            pl.BlockSpec((None, 1, tf), lambda i, j: (layer, 0, j)),
            pl.BlockSpec((None, 1, tf), lambda i, j: (layer, 0, n_f + j)),
            pl.BlockSpec((None, tf, D_MODEL), lambda i, j: (layer, j, 0)),
            pl.BlockSpec((1, D_MODEL), lambda i, j: (0, 0)),
        ],
        out_specs=pl.BlockSpec((tm, D_MODEL), lambda i, j: (i, 0)),
        out_shape=jax.ShapeDtypeStruct((T, D_MODEL), F32),
        scratch_shapes=[pltpu.VMEM((tm + 2 * FFN_HALO, D_MODEL), BF16),
                        pltpu.VMEM((tm, D_MODEL), F32)]
        + [pltpu.VMEM((tm + 2 * FFN_HALO, tf), F32)] * 2,
        compiler_params=_params("parallel", "arbitrary"),
    )(x2d, x2d, x2d, row(pre_gain), w_up_bf16, w_up_bf16, conv_w, conv_w,
      conv_b, conv_b, w_down_bf16, row(post_gain))


def _trunk(x, p):
    B, L, _ = x.shape
    x2d = x.reshape(B * L, D_MODEL)
    plan = _FftPlan(L)
    for l in range(p["w_in"].shape[0]):
        proj = _in_proj(x2d, p["norm_mix_pre"][l], p["w_in_bf16"], l)
        x0, z = _hyena_pre(proj, p["hyena_conv_w"][l], p["hyena_conv_b"][l], L)
        kern = _hyena_filter(L, p["filt_w1"][l], p["filt_b1"][l], p["filt_w2"][l], p["filt_b2"][l],
                             p["filt_w3"][l], p["filt_b3"][l], p["filt_w4"][l], p["filt_freq"][l])
        y = _fft_long_conv(z.reshape(B, L, D_HYENA), kern, plan).reshape(B * L, D_HYENA)
        o_fwd, o_bwd = _hgrn_scan(proj, p["hgrn_lower_bounds"], l, B, L)
        x2d = _mix_out(x2d, y, z, x0, o_fwd, o_bwd, proj, p["hyena_skip"][l], p["hyena_out_norm"][l],
                       p["hgrn_out_norm"][l], p["w_out_bf16"], l, p["norm_mix_post"][l])
        x2d = _ffn(x2d, p["norm_ffn_pre"][l], p["ffn_w_up_bf16"], p["ffn_conv_w"],
                   p["ffn_conv_b"], p["ffn_w_down_bf16"], l, p["norm_ffn_post"][l], L)
    return x2d.reshape(B, L, D_MODEL)


def kernel(x_prompt, x_sample, norm_mix_pre, norm_mix_post, norm_ffn_pre, norm_ffn_post, w_in, hyena_conv_w, hyena_conv_b, filt_w1, filt_b1, filt_w2, filt_b2, filt_w3, filt_b3, filt_w4, filt_freq, hyena_skip, hyena_out_norm, hgrn_lower_bounds, hgrn_out_norm, w_out, ffn_w_up, ffn_conv_w, ffn_conv_b, ffn_w_down):
    p = dict(
        norm_mix_pre=norm_mix_pre, norm_mix_post=norm_mix_post, norm_ffn_pre=norm_ffn_pre,
        norm_ffn_post=norm_ffn_post, w_in=w_in, hyena_conv_w=hyena_conv_w,
        hyena_conv_b=hyena_conv_b, filt_w1=filt_w1, filt_b1=filt_b1, filt_w2=filt_w2,
        filt_b2=filt_b2, filt_w3=filt_w3, filt_b3=filt_b3, filt_w4=filt_w4, filt_freq=filt_freq,
        hyena_skip=hyena_skip, hyena_out_norm=hyena_out_norm,
        hgrn_lower_bounds=hgrn_lower_bounds, hgrn_out_norm=hgrn_out_norm,
        ffn_conv_w=ffn_conv_w, ffn_conv_b=ffn_conv_b,
        w_in_bf16=w_in.astype(BF16), w_out_bf16=w_out.astype(BF16),
        ffn_w_up_bf16=ffn_w_up.astype(BF16), ffn_w_down_bf16=ffn_w_down.astype(BF16),
    )
    return (_trunk(x_prompt, p), _trunk(x_sample, p))
```
